```python
import math
import jax
import jax.numpy as jnp
from jax import lax
import numpy as np

D_MODEL = 1024
BATCH = 4
SEQ = 4096
DEPTH = 1

CTX_LEN = 256
GRID_W = 64
NORM_EPS = 1e-6
A_GROUPS = 8
A_WIDTH = 1024
A_GW = A_WIDTH // A_GROUPS
A_CHUNK = 128
B_HEADS = 8
B_HEAD_DIM = 128
B_WIDTH = B_HEADS * B_HEAD_DIM
B_CHUNK = 64
SHORT_CONV = 3
D_FF = 2816
FFN_CONV = 3
OFF_K = 0
OFF_V = OFF_K + B_WIDTH
OFF_BA = OFF_V + B_WIDTH
OFF_Q = OFF_BA + 4 * B_HEADS
OFF_Z = OFF_Q + B_WIDTH
OFF_U = OFF_Z + B_WIDTH
OFF_AV = OFF_U + A_WIDTH
OFF_GA = OFF_AV + A_WIDTH
OFF_GB = OFF_GA + D_MODEL
IN_COLS = OFF_GB + D_MODEL
STATE_COLS = OFF_Q

kernel_name = 'hybrid_gmlp_gdn_dit_block'


def rms_norm(x, g):
    xf = x.astype(jnp.float32)
    y = xf * lax.rsqrt(jnp.mean(xf * xf, axis=-1, keepdims=True) + NORM_EPS)
    return (y * g.astype(jnp.float32)).astype(x.dtype)


def layer_norm(x, g, b):
    xf = x.astype(jnp.float32)
    mu = jnp.mean(xf, axis=-1, keepdims=True)
    xc = xf - mu
    y = xc * lax.rsqrt(jnp.mean(xc * xc, axis=-1, keepdims=True) + NORM_EPS)
    return (y * g.astype(jnp.float32) + b.astype(jnp.float32)).astype(x.dtype)


def l2norm(t):
    return t * lax.rsqrt(jnp.sum(t * t, axis=-1, keepdims=True) + NORM_EPS)


def modulate(h, shift, scale):
    return h * (1.0 + scale) + shift


def to_heads(t):
    bsz, length, _ = t.shape
    return t.reshape(bsz, length, B_HEADS, B_HEAD_DIM).transpose(0, 2, 1, 3)


def from_heads(t):
    bsz, nh, length, hd = t.shape
    return t.transpose(0, 2, 1, 3).reshape(bsz, length, nh * hd)


def dwconv(x, w, n_rows):
    bsz, length, ch = x.shape
    row_len = length // n_rows
    taps = w.shape[0]
    pad = taps // 2
    xp = jnp.pad(x.reshape(bsz, n_rows, row_len, ch), ((0, 0), (0, 0), (pad, pad), (0, 0)))
    y = xp[:, :, 0:row_len] * w[0]
    for t in range(1, taps):
        y = y + xp[:, :, t:t + row_len] * w[t]
    return y.reshape(bsz, length, ch)


def gated_delta_chunked(q, k, v, beta, g, s0):
    bsz, nh, length, _ = k.shape
    dv = v.shape[-1]
    n = length // B_CHUNK

    def chunks(t):
        return jnp.moveaxis(t.reshape(bsz, nh, n, B_CHUNK, *t.shape[3:]), 2, 0)

    k, v, beta, g = chunks(k), chunks(v), chunks(beta), chunks(g)
    gc = jnp.cumsum(g, axis=-1)
    pos = jnp.arange(B_CHUNK)
    incl = pos[:, None] >= pos[None, :]
    strict = pos[:, None] > pos[None, :]
    diff = gc[..., :, None] - gc[..., None, :]
    gamma = jnp.where(incl, jnp.exp(jnp.where(incl, diff, 0.0)), 0.0)
    kb = k * beta[..., None]
    m = jnp.where(strict, jnp.einsum('nbhid,nbhjd->nbhij', kb, k) * gamma, 0.0)
    eye = jnp.eye(B_CHUNK, dtype=k.dtype)
    t_inv = lax.linalg.triangular_solve(eye + m, jnp.broadcast_to(eye, m.shape), left_side=True, lower=True)
    w_c = t_inv @ (kb * jnp.exp(gc)[..., None])
    u_c = t_inv @ (v * beta[..., None])
    g_end = gc[..., -1]
    kd = k * jnp.exp(g_end[..., None] - gc)[..., None]

    def update(s, w_i, u_i, kd_i, ge_i):
        v_new = u_i - jnp.einsum('bhck,bhkv->bhcv', w_i, s)
        s_next = s * jnp.exp(ge_i)[..., None, None] + jnp.einsum('bhck,bhcv->bhkv', kd_i, v_new)
        return v_new, s_next

    if q is None:
        def step_state(s, xs):
            _, s_next = update(s, *xs)
            return s_next, None
        s_fin, _ = lax.scan(step_state, s0, (w_c, u_c, kd, g_end))
        return None, s_fin

    q = chunks(q)
    qk = jnp.einsum('nbhid,nbhjd->nbhij', q, k) * gamma
    qg = q * jnp.exp(gc)[..., None]

    def step(s, xs):
        w_i, u_i, kd_i, ge_i, qg_i, qk_i = xs
        v_new, s_next = update(s, w_i, u_i, kd_i, ge_i)
        o = jnp.einsum('bhck,bhkv->bhcv', qg_i, s) + jnp.einsum('bhij,bhjv->bhiv', qk_i, v_new)
        return s_next, o

    s_fin, o = lax.scan(step, s0, (w_c, u_c, kd, g_end, qg, qk))
    o = jnp.moveaxis(o, 0, 2).reshape(bsz, nh, length, dv)
    return o, s_fin


def delta_prep(p, conv_w, a_log, dt_bias, with_q):
    f32 = jnp.float32
    bsz, length = p.shape[0], p.shape[1]
    kv = jax.nn.silu(dwconv(p[..., OFF_K:OFF_BA], conv_w[:, :2 * B_WIDTH], 1)).astype(f32)
    k = l2norm(to_heads(kv[..., :B_WIDTH]))
    v = to_heads(kv[..., B_WIDTH:])
    ba = p[..., OFF_BA:OFF_Q].astype(f32).reshape(bsz, length, 4, B_HEADS).transpose(2, 0, 3, 1)
    beta = jax.nn.sigmoid(ba[:2])
    g = -jnp.exp(a_log.astype(f32))[:, None, :, None] * jax.nn.softplus(
        ba[2:] + dt_bias.astype(f32)[:, None, :, None])
    q = None
    if with_q:
        qr = jax.nn.silu(dwconv(p[..., OFF_Q:OFF_Z], conv_w[:, 2 * B_WIDTH:], 1)).astype(f32)
        q = l2norm(to_heads(qr)) * (B_HEAD_DIM ** -0.5)
    return q, k, v, beta, g


def flip_seq(t):
    return jnp.flip(t, axis=2)


def bidir_delta(q, k, v, beta, g, s0):
    o_f, s_f = gated_delta_chunked(q, k, v, beta[0], g[0], s0[0])
    o_b, s_b = gated_delta_chunked(flip_seq(q), flip_seq(k), flip_seq(v), flip_seq(beta[1]),
                                   flip_seq(g[1]), s0[1])
    return o_f + flip_seq(o_b), jnp.stack([s_f, s_b])


def bidir_delta_state(k, v, beta, g):
    bsz = k.shape[0]
    zero = jnp.zeros((bsz, B_HEADS, B_HEAD_DIM, B_HEAD_DIM), jnp.float32)
    _, s_f = gated_delta_chunked(None, k, v, beta[0], g[0], zero)
    _, s_b = gated_delta_chunked(None, flip_seq(k), flip_seq(v), flip_seq(beta[1]), flip_seq(g[1]), zero)
    return jnp.stack([s_f, s_b])


def chunk_mlp(u, v, ln_g, ln_b, w_s, b_s):
    bsz, length, _ = u.shape
    v = layer_norm(v, ln_g, ln_b)
    vc = v.reshape(bsz, length // A_CHUNK, A_CHUNK, A_GROUPS, A_GW)
    s = jnp.einsum('gij,bnjgc->bnigc', w_s, vc) + b_s.T[None, None, :, :, None]
    return u * s.reshape(bsz, length, A_WIDTH)


def token_mixers(h, s0, w_in, conv_qkv, a_log, dt_bias, onorm_g, w_proj_b,
                 a_ln_g, a_ln_b, a_ws, a_bs, w_proj_a, w_out):
    p = h @ w_in
    q, k, v, beta, g = delta_prep(p, conv_qkv, a_log, dt_bias, True)
    o, states = bidir_delta(q, k, v, beta, g, s0)
    z = to_heads(p[..., OFF_Z:OFF_U]).astype(jnp.float32)
    o = rms_norm(o, onorm_g) * jax.nn.silu(z)
    y_b = from_heads(o).astype(h.dtype) @ w_proj_b
    u = jax.nn.gelu(p[..., OFF_U:OFF_AV])
    va = jax.nn.gelu(p[..., OFF_AV:OFF_GA])
    y_a = chunk_mlp(u, va, a_ln_g, a_ln_b, a_ws, a_bs) @ w_proj_a
    g_a = jax.nn.sigmoid(p[..., OFF_GA:OFF_GB])
    g_b = jax.nn.sigmoid(p[..., OFF_GB:IN_COLS])
    return (g_a * y_a + g_b * y_b) @ w_out, states


def conv_ffn(h, n_rows, w_up, conv_w, conv_b, w_down):
    a, b = jnp.split(h @ w_up, 2, axis=-1)
    a = dwconv(a, conv_w, n_rows) + conv_b
    return (jax.nn.gelu(a) * b) @ w_down


def setup_inputs(seed: int = 0) -> dict:
    key = jax.random.key(seed)
    ks = iter(jax.random.split(key, 32))
    f32 = jnp.float32

    def nrm(shape, scale):
        return jax.random.normal(next(ks), shape, f32) * scale

    def gain(shape):
        return 1.0 + 0.02 * jax.random.normal(next(ks), shape, f32)

    x = nrm((BATCH, SEQ, D_MODEL), 1.0)
    c = nrm((BATCH, D_MODEL), 1.0)
    ctx = nrm((BATCH, CTX_LEN, D_MODEL), 1.0)
    c_ctx = nrm((D_MODEL,), 1.0)
    w_mod = nrm((DEPTH, D_MODEL, 6 * D_MODEL), 0.5 * D_MODEL ** -0.5)
    b_mod = nrm((DEPTH, 6 * D_MODEL), 0.02)
    norm1_g = gain((DEPTH, D_MODEL))
    w_in = nrm((DEPTH, D_MODEL, IN_COLS), D_MODEL ** -0.5)
    conv_qkv = nrm((DEPTH, SHORT_CONV, 3 * B_WIDTH), SHORT_CONV ** -0.5)
    a_log = jnp.log(jax.random.uniform(next(ks), (DEPTH, 2, B_HEADS), f32, 1.0, 16.0))
    dt = jnp.exp(jax.random.uniform(next(ks), (DEPTH, 2, B_HEADS), f32, math.log(1e-3), math.log(1e-1)))
    dt_bias = dt + jnp.log(-jnp.expm1(-dt))
    onorm_g = gain((DEPTH, B_HEAD_DIM))
    w_proj_b = nrm((DEPTH, B_WIDTH, D_MODEL), B_WIDTH ** -0.5)
    a_ln_g = gain((DEPTH, A_WIDTH))
    a_ln_b = nrm((DEPTH, A_WIDTH), 0.02)
    a_ws = nrm((DEPTH, A_GROUPS, A_CHUNK, A_CHUNK), A_CHUNK ** -0.5)
    a_bs = 1.0 + nrm((DEPTH, A_GROUPS, A_CHUNK), 0.02)
    w_proj_a = nrm((DEPTH, A_WIDTH, D_MODEL), A_WIDTH ** -0.5)
    w_out = nrm((DEPTH, D_MODEL, D_MODEL), D_MODEL ** -0.5)
    norm2_g = gain((DEPTH, D_MODEL))
    w_up = nrm((DEPTH, D_MODEL, 2 * D_FF), D_MODEL ** -0.5)
    ffn_conv_w = nrm((DEPTH, FFN_CONV, D_FF), FFN_CONV ** -0.5)
    ffn_conv_b = nrm((DEPTH, D_FF), 0.02)
    w_down = nrm((DEPTH, D_FF, D_MODEL), D_FF ** -0.5)
    final_g = gain((D_MODEL,))
    return {'x': x, 'c': c, 'ctx': ctx, 'c_ctx': c_ctx, 'w_mod': w_mod, 'b_mod': b_mod,
            'norm1_g': norm1_g, 'w_in': w_in, 'conv_qkv': conv_qkv, 'a_log': a_log,
            'dt_bias': dt_bias, 'onorm_g': onorm_g, 'w_proj_b': w_proj_b, 'a_ln_g': a_ln_g,
            'a_ln_b': a_ln_b, 'a_ws': a_ws, 'a_bs': a_bs, 'w_proj_a': w_proj_a, 'w_out': w_out,
            'norm2_g': norm2_g, 'w_up': w_up, 'ffn_conv_w': ffn_conv_w, 'ffn_conv_b': ffn_conv_b,
            'w_down': w_down, 'final_g': final_g}


def reference(x, c, ctx, c_ctx, w_mod, b_mod, norm1_g, w_in, conv_qkv, a_log, dt_bias, onorm_g,
              w_proj_b, a_ln_g, a_ln_b, a_ws, a_bs, w_proj_a, w_out, norm2_g, w_up, ffn_conv_w,
              ffn_conv_b, w_down, final_g):
    rows = x.shape[1] // GRID_W
    cond = jax.nn.silu(c)
    cond_ctx = jax.nn.silu(c_ctx)
    for l in range(DEPTH):
        last = l == DEPTH - 1
        mod = (cond @ w_mod[l] + b_mod[l])[:, None, :]
        mod_c = cond_ctx @ w_mod[l] + b_mod[l]
        sh1, sc1, gt1, sh2, sc2, gt2 = jnp.split(mod, 6, axis=-1)
        csh1, csc1, cgt1, csh2, csc2, cgt2 = jnp.split(mod_c, 6, axis=-1)
        mix_w = (conv_qkv[l], a_log[l], dt_bias[l], onorm_g[l], w_proj_b[l], a_ln_g[l], a_ln_b[l],
                 a_ws[l], a_bs[l], w_proj_a[l], w_out[l])
        hc = modulate(rms_norm(ctx, norm1_g[l]), csh1, csc1)
        if last:
            pc = hc @ w_in[l][:, :STATE_COLS]
            _, kc, vc, betac, gcx = delta_prep(pc, conv_qkv[l], a_log[l], dt_bias[l], False)
            s_ctx = bidir_delta_state(kc, vc, betac, gcx)
        else:
            zero = jnp.zeros((2, ctx.shape[0], B_HEADS, B_HEAD_DIM, B_HEAD_DIM), jnp.float32)
            yc, s_ctx = token_mixers(hc, zero, w_in[l], *mix_w)
        hx = modulate(rms_norm(x, norm1_g[l]), sh1, sc1)
        y, _ = token_mixers(hx, s_ctx, w_in[l], *mix_w)
        x = x + gt1 * y
        hx2 = modulate(rms_norm(x, norm2_g[l]), sh2, sc2)
        x = x + gt2 * conv_ffn(hx2, rows, w_up[l], ffn_conv_w[l], ffn_conv_b[l], w_down[l])
        if not last:
            ctx = ctx + cgt1 * yc
            hc2 = modulate(rms_norm(ctx, norm2_g[l]), csh2, csc2)
            ctx = ctx + cgt2 * conv_ffn(hc2, 1, w_up[l], ffn_conv_w[l], ffn_conv_b[l], w_down[l])
    return rms_norm(x, final_g)
```

```python
import functools

import jax
import jax.numpy as jnp
from jax import lax
from jax.experimental import pallas as pl
from jax.experimental.pallas import tpu as pltpu

F32 = jnp.float32
BF16 = jnp.bfloat16

D_MODEL = 1024
GRID_W = 64
NORM_EPS = 1e-6
N_HEADS = 8
HEAD_DIM = 128
B_WIDTH = N_HEADS * HEAD_DIM
A_GROUPS = 8
A_CHUNK = 128
D_FF = 2816
OFF_BA = 2 * B_WIDTH
OFF_Q = OFF_BA + 4 * N_HEADS
LANES = 128
P_COLS = 8 * D_MODEL + LANES
COLBLK_BA = 8 * D_MODEL // LANES
CHUNK = 128

VMEM_LIMIT = 56 * 1024 * 1024


def _silu(x):
    return x * jax.nn.sigmoid(x)


def _dot(a, b):
    return jnp.dot(a, b, preferred_element_type=F32)


def _dot_nt(a, b):
    return lax.dot_general(a, b, (((1,), (1,)), ((), ())), preferred_element_type=F32)


def _mod_kernel(c_ref, w_ref, b_ref, o_ref):
    cond = _silu(c_ref[...])
    o_ref[...] = jnp.dot(cond, w_ref[...], preferred_element_type=F32,
                         precision=lax.Precision.HIGHEST) + b_ref[...]


def _mod_call(cond8, w_mod, b_mod):
    n = w_mod.shape[1]
    tn = 1536
    return pl.pallas_call(
        _mod_kernel,
        grid=(n // tn,),
        in_specs=[pl.BlockSpec((8, D_MODEL), lambda j: (0, 0)),
                  pl.BlockSpec((D_MODEL, tn), lambda j: (0, j)),
                  pl.BlockSpec((1, tn), lambda j: (0, j))],
        out_specs=pl.BlockSpec((8, tn), lambda j: (0, j)),
        out_shape=jax.ShapeDtypeStruct((8, n), F32),
        name="mod",
    )(cond8, w_mod, b_mod.reshape(1, n))


def _norm_mod(x, g, shift, scale):
    ms = jnp.mean(x * x, axis=-1, keepdims=True)
    y = x * lax.rsqrt(ms + NORM_EPS) * g
    return y * (1.0 + scale) + shift


def _inproj_kernel(x_ref, mod_ref, g_ref, w_ref, o_ref, h_ref):
    @pl.when(pl.program_id(1) == 0)
    def _():
        h = _norm_mod(x_ref[...], g_ref[...], mod_ref[:, 0:D_MODEL], mod_ref[:, D_MODEL:2 * D_MODEL])
        h_ref[...] = h.astype(BF16)

    o_ref[...] = _dot(h_ref[...], w_ref[...])


def _inproj_call(x2d, mod3, mod_row_of_tile, norm_g, w, tm, tn, name):
    m = x2d.shape[0]
    n = w.shape[1]
    return pl.pallas_call(
        _inproj_kernel,
        grid=(m // tm, n // tn),
        in_specs=[pl.BlockSpec((tm, D_MODEL), lambda i, j: (i, 0)),
                  pl.BlockSpec((None, 1, 6 * D_MODEL), lambda i, j: (mod_row_of_tile(i), 0, 0)),
                  pl.BlockSpec((1, D_MODEL), lambda i, j: (0, 0)),
                  pl.BlockSpec((D_MODEL, tn), lambda i, j: (0, j))],
        out_specs=pl.BlockSpec((tm, tn), lambda i, j: (i, j)),
        out_shape=jax.ShapeDtypeStruct((m, n), F32),
        scratch_shapes=[pltpu.VMEM((tm, D_MODEL), BF16)],
        compiler_params=pltpu.CompilerParams(
            dimension_semantics=("parallel", "arbitrary"), vmem_limit_bytes=VMEM_LIMIT),
        name=name,
    )(x2d, mod3, norm_g, w)


def _gates_kernel(ba_ref, a_ref, dt_ref, col_ref, row_ref, *, n_sub):
    lane = lax.broadcasted_iota(jnp.int32, (CHUNK, LANES), 1)
    ri = lax.broadcasted_iota(jnp.int32, (CHUNK, CHUNK), 0)
    ci = lax.broadcasted_iota(jnp.int32, (CHUNK, CHUNK), 1)
    tri_lo = (ri >= ci).astype(F32)
    tri_up = (ri <= ci).astype(F32)
    for s in range(n_sub):
        x = ba_ref[s * CHUNK:(s + 1) * CHUNK, :]
        beta = jax.nn.sigmoid(x)
        y = x + dt_ref[...]
        softplus = jnp.maximum(y, 0.0) + jnp.log1p(jnp.exp(-jnp.abs(y)))
        g = jnp.where((lane >= 2 * N_HEADS) & (lane < 4 * N_HEADS), -a_ref[...] * softplus, 0.0)
        pre = jnp.dot(tri_lo, g, preferred_element_type=F32, precision=lax.Precision.HIGHEST)
        suf = jnp.dot(tri_up, g, preferred_element_type=F32, precision=lax.Precision.HIGHEST)
        col = jnp.where(lane < 2 * N_HEADS, beta, jnp.where(lane < 3 * N_HEADS, pre, suf))
        col_ref[s * CHUNK:(s + 1) * CHUNK, :] = col
        row_ref[:, s * CHUNK:(s + 1) * CHUNK] = col.T


def _gates_call(p3, col_block, a_lane, dt_lane, name):
    bsz, length, _ = p3.shape
    tl = min(length, 1024)
    return pl.pallas_call(
        functools.partial(_gates_kernel, n_sub=tl // CHUNK),
        grid=(bsz, length // tl),
        in_specs=[pl.BlockSpec((None, tl, LANES), lambda b, i: (b, i, col_block)),
                  pl.BlockSpec((1, LANES), lambda b, i: (0, 0)),
                  pl.BlockSpec((1, LANES), lambda b, i: (0, 0))],
        out_specs=[pl.BlockSpec((None, tl, LANES), lambda b, i: (b, i, 0)),
                   pl.BlockSpec((None, LANES, tl), lambda b, i: (b, 0, i))],
        out_shape=[jax.ShapeDtypeStruct((bsz, length, LANES), F32),
                   jax.ShapeDtypeStruct((bsz, LANES, length), F32)],
        name=name,
    )(p3, a_lane, dt_lane)


def _tri_inverse(n_mat, ri, ci):
    def same_block(size):
        shift = size.bit_length() - 1
        return (ri >> shift) == (ci >> shift)

    eye = (ri == ci).astype(F32)
    n16 = jnp.where(same_block(16), n_mat, 0.0)
    t = eye + n16
    p = n16.astype(BF16)
    p = _dot(p, p)
    for _ in range(2):
        pb = p.astype(BF16)
        t = t + _dot(t.astype(BF16), pb)
        p = _dot(pb, pb)
    t = t + _dot(t.astype(BF16), p.astype(BF16))
    for size in (32, 64, 128):
        off = jnp.where(same_block(size) & jnp.logical_not(same_block(size // 2)), n_mat, 0.0)
        tb = t.astype(BF16)
        t = t + _dot(tb, _dot(off.astype(BF16), tb).astype(BF16))
    return t


def _chunk_start(c):
    return c * CHUNK if isinstance(c, int) else pl.multiple_of(c * CHUNK, CHUNK)


def _conv_act(p_ref, w_ref, c, n_chunks):
    length = n_chunks * CHUNK
    r0 = _chunk_start(c)
    cur = p_ref[pl.ds(r0, CHUNK), :]
    if isinstance(c, int):
        lo, hi = max(r0 - 8, 0), min(r0 + CHUNK, length - 8)
    else:
        lo = pl.multiple_of(jnp.maximum(r0 - 8, 0), 8)
        hi = pl.multiple_of(jnp.minimum(r0 + CHUNK, length - 8), 8)
    prev_row = jnp.where(c > 0, p_ref[pl.ds(lo, 8), :][7:8, :], 0.0)
    next_row = jnp.where(c < n_chunks - 1, p_ref[pl.ds(hi, 8), :][0:1, :], 0.0)
    row = lax.broadcasted_iota(jnp.int32, (CHUNK, LANES), 0)
    xp = jnp.where(row == 0, prev_row, pltpu.roll(cur, 1, 0))
    xn = jnp.where(row == CHUNK - 1, next_row, pltpu.roll(cur, CHUNK - 1, 0))
    y = xp * w_ref[0:1, :] + cur * w_ref[1:2, :] + xn * w_ref[2:3, :]
    return _silu(y)


def _l2norm(t):
    return t * lax.rsqrt(jnp.sum(t * t, axis=-1, keepdims=True) + NORM_EPS)


def _prep_chunk(c, n_chunks, h, pk_ref, pv_ref, pq_ref, gcol_ref, grow_ref, wk_ref, wv_ref, wq_ref):
    r0 = _chunk_start(c)
    k = _l2norm(_conv_act(pk_ref, wk_ref, c, n_chunks))
    v = _conv_act(pv_ref, wv_ref, c, n_chunks)
    kb16 = k.astype(BF16)
    if pq_ref is not None:
        q = _l2norm(_conv_act(pq_ref, wq_ref, c, n_chunks)) * (HEAD_DIM ** -0.5)
        aa = _dot_nt(jnp.concatenate([kb16, q.astype(BF16)], axis=0), kb16)
        a_kk, a_qk = aa[:CHUNK], aa[CHUNK:]
    else:
        q = None
        a_kk = _dot_nt(kb16, kb16)

    gates = gcol_ref[pl.ds(r0, CHUNK), :]
    lane = lax.broadcasted_iota(jnp.int32, (CHUNK, LANES), 1)

    def col(idx):
        return jnp.sum(jnp.where(lane == idx, gates, 0.0), axis=1, keepdims=True)

    ri = lax.broadcasted_iota(jnp.int32, (CHUNK, CHUNK), 0)
    ci = lax.broadcasted_iota(jnp.int32, (CHUNK, CHUNK), 1)
    head_row = lax.broadcasted_iota(jnp.int32, (N_HEADS, CHUNK), 0)
    out = []
    for d in range(2):
        beta = col(d * N_HEADS + h)
        gc = col((2 + d) * N_HEADS + h)
        all_heads = grow_ref[(2 + d) * N_HEADS:(3 + d) * N_HEADS, pl.ds(r0, CHUNK)]
        gc_row = jnp.sum(jnp.where(head_row == h, all_heads, 0.0), axis=0, keepdims=True)
        incl = (ri >= ci) if d == 0 else (ri <= ci)
        strict = (ri > ci) if d == 0 else (ri < ci)
        gamma = jnp.where(incl, jnp.exp(jnp.where(incl, gc - gc_row, 0.0)), 0.0)
        neg_m = jnp.where(strict, -(a_kk * beta * gamma), 0.0)
        t_inv = _tri_inverse(neg_m, ri, ci)
        eg = jnp.exp(gc)
        rhs = jnp.concatenate([k * (beta * eg), v * beta], axis=1).astype(BF16)
        wu = _dot(t_inv.astype(BF16), rhs)
        w, u = wu[:, :HEAD_DIM], wu[:, HEAD_DIM:]
        g_end = gc[CHUNK - 1:CHUNK, :] if d == 0 else gc[0:1, :]
        kd = k * jnp.exp(g_end - gc)
        decay = jnp.broadcast_to(jnp.exp(g_end), (8, LANES))
        if q is not None:
            qg = q * eg
            qk = jnp.where(incl, a_qk * gamma, 0.0)
        else:
            qg = qk = None
        out.append((w, u, kd, decay, qg, qk))
    return out


def _delta_kernel(pk_ref, pv_ref, pq_ref, pz_ref, gcol_ref, grow_ref, ck_ref, cv_ref, cgcol_ref, cgrow_ref,
                  wk_ref, wv_ref, wq_ref, onorm_ref, og_ref, wq_sc, kq_sc, u_sc, ge_sc, o_sc,
                  *, n_chunks, n_ctx_chunks):
    h = pl.program_id(1)

    def update(s, w, u, kd_t, decay):
        v_new = u - _dot(w.astype(BF16), s.astype(BF16))
        return s * decay[0:1, :] + _dot(kd_t.astype(BF16), v_new.astype(BF16))

    zero = jnp.zeros((HEAD_DIM, HEAD_DIM), F32)
    ctx = [_prep_chunk(c, n_ctx_chunks, h, ck_ref, cv_ref, None, cgcol_ref, cgrow_ref, wk_ref, wv_ref, None)
           for c in range(n_ctx_chunks)]
    s_f, s_b = zero, zero
    for c in range(n_ctx_chunks):
        w, u, kd, decay, _, _ = ctx[c][0]
        s_f = update(s_f, w, u, kd.T, decay)
        w, u, kd, decay, _, _ = ctx[n_ctx_chunks - 1 - c][1]
        s_b = update(s_b, w, u, kd.T, decay)

    def prep_body(c, carry):
        r0 = _chunk_start(c)
        both = _prep_chunk(c, n_chunks, h, pk_ref, pv_ref, pq_ref, gcol_ref, grow_ref, wk_ref, wv_ref, wq_ref)
        for d in range(2):
            w, u, kd, decay, qg, qk = both[d]
            wq_sc[d, c] = jnp.concatenate([w, qg], axis=0).astype(BF16)
            kq_sc[d, c] = jnp.concatenate([kd.T, qk], axis=0).astype(BF16)
            u_sc[d, pl.ds(r0, CHUNK), :] = u
            ge_sc[d, c] = decay
        return carry

    lax.fori_loop(0, n_chunks, prep_body, 0)

    def seq_step(s, d, c):
        r0 = _chunk_start(c)
        ws = _dot(wq_sc[d, c], s.astype(BF16))
        v_new = u_sc[d, pl.ds(r0, CHUNK), :] - ws[:CHUNK]
        kv = _dot(kq_sc[d, c], v_new.astype(BF16))
        s_next = s * ge_sc[d, c][0:1, :] + kv[:CHUNK]
        return s_next, ws[CHUNK:] + kv[CHUNK:]

    def finish(c, o):
        r0 = _chunk_start(c)
        y = o * lax.rsqrt(jnp.mean(o * o, axis=-1, keepdims=True) + NORM_EPS) * onorm_ref[...]
        og_ref[pl.ds(r0, CHUNK), :] = (y * _silu(pz_ref[pl.ds(r0, CHUNK), :])).astype(og_ref.dtype)

    def first_half(i, carry):
        sf, sb = carry
        cf, cb = i, n_chunks - 1 - i
        sf, o_f = seq_step(sf, 0, cf)
        sb, o_b = seq_step(sb, 1, cb)
        o_sc[pl.ds(_chunk_start(cf), CHUNK), :] = o_f
        o_sc[pl.ds(_chunk_start(cb), CHUNK), :] = o_b
        return sf, sb

    def second_half(i, carry):
        sf, sb = carry
        cf, cb = i, n_chunks - 1 - i
        sf, o_f = seq_step(sf, 0, cf)
        sb, o_b = seq_step(sb, 1, cb)
        finish(cf, o_sc[pl.ds(_chunk_start(cf), CHUNK), :] + o_f)
        finish(cb, o_sc[pl.ds(_chunk_start(cb), CHUNK), :] + o_b)
        return sf, sb

    carry = lax.fori_loop(0, n_chunks // 2, first_half, (s_f, s_b))
    lax.fori_loop(n_chunks // 2, n_chunks, second_half, carry)


def _delta_call(p3, gcol, grow, pc3, cgcol, cgrow, conv_w, onorm_g):
    bsz, length, _ = p3.shape
    ctx_len = pc3.shape[1]
    n_chunks = length // CHUNK
    n_ctx_chunks = ctx_len // CHUNK
    assert n_chunks % 2 == 0

    def col_spec(rows, first_block):
        return pl.BlockSpec((None, rows, LANES), lambda b, h: (b, 0, first_block + h))

    return pl.pallas_call(
        functools.partial(_delta_kernel, n_chunks=n_chunks, n_ctx_chunks=n_ctx_chunks),
        grid=(bsz, N_HEADS),
        in_specs=[col_spec(length, 0), col_spec(length, N_HEADS), col_spec(length, 2 * N_HEADS),
                  col_spec(length, 3 * N_HEADS),
                  pl.BlockSpec((None, length, LANES), lambda b, h: (b, 0, 0)),
                  pl.BlockSpec((None, LANES, length), lambda b, h: (b, 0, 0)),
                  col_spec(ctx_len, 0), col_spec(ctx_len, N_HEADS),
                  pl.BlockSpec((None, ctx_len, LANES), lambda b, h: (b, 0, 0)),
                  pl.BlockSpec((None, LANES, ctx_len), lambda b, h: (b, 0, 0)),
                  pl.BlockSpec((3, LANES), lambda b, h: (0, h)),
                  pl.BlockSpec((3, LANES), lambda b, h: (0, N_HEADS + h)),
                  pl.BlockSpec((3, LANES), lambda b, h: (0, 2 * N_HEADS + h)),
                  pl.BlockSpec((1, LANES), lambda b, h: (0, 0))],
        out_specs=pl.BlockSpec((None, length, LANES), lambda b, h: (b, 0, h)),
        out_shape=jax.ShapeDtypeStruct((bsz, length, B_WIDTH), BF16),
        scratch_shapes=[pltpu.VMEM((2, n_chunks, 2 * CHUNK, LANES), BF16),
                        pltpu.VMEM((2, n_chunks, 2 * CHUNK, LANES), BF16),
                        pltpu.VMEM((2, length, LANES), F32),
                        pltpu.VMEM((2, n_chunks, 8, LANES), F32),
                        pltpu.VMEM((length, LANES), F32)],
        compiler_params=pltpu.CompilerParams(
            dimension_semantics=("parallel", "parallel"), vmem_limit_bytes=VMEM_LIMIT),
        name="delta",
    )(p3, p3, p3, p3, gcol, grow, pc3, pc3, cgcol, cgrow, conv_w, conv_w, conv_w, onorm_g)


def _gmlp_kernel(u_ref, v_ref, lng_ref, lnb_ref, ws_ref, bs_ref, o_ref, *, n_sub):
    v = jax.nn.gelu(v_ref[...])
    mu = jnp.mean(v, axis=-1, keepdims=True)
    vc = v - mu
    vn = vc * lax.rsqrt(jnp.mean(vc * vc, axis=-1, keepdims=True) + NORM_EPS) * lng_ref[...] + lnb_ref[...]
    vb = vn.astype(BF16)
    for n in range(n_sub):
        rows = slice(n * A_CHUNK, (n + 1) * A_CHUNK)
        for g in range(A_GROUPS):
            cols = slice(g * LANES, (g + 1) * LANES)
            s = _dot(ws_ref[g], vb[rows, cols]) + bs_ref[:, cols]
            o_ref[rows, cols] = (jax.nn.gelu(u_ref[rows, cols]) * s).astype(o_ref.dtype)


def _gmlp_call(p2d, ln_g, ln_b, ws16, bs_cols, tm):
    m = p2d.shape[0]
    return pl.pallas_call(
        functools.partial(_gmlp_kernel, n_sub=tm // A_CHUNK),
        grid=(m // tm,),
        in_specs=[pl.BlockSpec((tm, D_MODEL), lambda i: (i, 4)),
                  pl.BlockSpec((tm, D_MODEL), lambda i: (i, 5)),
                  pl.BlockSpec((1, D_MODEL), lambda i: (0, 0)),
                  pl.BlockSpec((1, D_MODEL), lambda i: (0, 0)),
                  pl.BlockSpec((A_GROUPS, A_CHUNK, A_CHUNK), lambda i: (0, 0, 0)),
                  pl.BlockSpec((A_CHUNK, D_MODEL), lambda i: (0, 0))],
        out_specs=pl.BlockSpec((tm, D_MODEL), lambda i: (i, 0)),
        out_shape=jax.ShapeDtypeStruct((m, D_MODEL), BF16),
        compiler_params=pltpu.CompilerParams(
            dimension_semantics=("parallel",), vmem_limit_bytes=VMEM_LIMIT),
        name="gmlp",
    )(p2d, p2d, ln_g, ln_b, ws16, bs_cols)


def _merge_kernel(ua_ref, og_ref, ga_ref, gb_ref, x_ref, mod_ref, wa_ref, wb_ref, wo_ref, o_ref):
    y_a = _dot(ua_ref[...], wa_ref[...])
    y_b = _dot(og_ref[...], wb_ref[...])
    t = jax.nn.sigmoid(ga_ref[...]) * y_a + jax.nn.sigmoid(gb_ref[...]) * y_b
    y = _dot(t.astype(BF16), wo_ref[...])
    o_ref[...] = x_ref[...] + mod_ref[:, 2 * D_MODEL:3 * D_MODEL] * y


def _merge_call(ua, og, p2d, x2d, mod3, wa16, wb16, wo16, tm, tiles_per_batch):
    m = x2d.shape[0]
    full = pl.BlockSpec((D_MODEL, D_MODEL), lambda i: (0, 0))
    rows = pl.BlockSpec((tm, D_MODEL), lambda i: (i, 0))
    return pl.pallas_call(
        _merge_kernel,
        grid=(m // tm,),
        in_specs=[rows, rows,
                  pl.BlockSpec((tm, D_MODEL), lambda i: (i, 6)),
                  pl.BlockSpec((tm, D_MODEL), lambda i: (i, 7)),
                  rows,
                  pl.BlockSpec((None, 1, 6 * D_MODEL), lambda i: (i // tiles_per_batch, 0, 0)),
                  full, full, full],
        out_specs=rows,
        out_shape=jax.ShapeDtypeStruct((m, D_MODEL), F32),
        compiler_params=pltpu.CompilerParams(
            dimension_semantics=("parallel",), vmem_limit_bytes=VMEM_LIMIT),
        name="merge",
    )(ua, og, p2d, p2d, x2d, mod3, wa16, wb16, wo16)


def _ffn_kernel(x_ref, mod_ref, g2_ref, wa_ref, wb_ref, cw_ref, cb_ref, wd_ref, gf_ref, o_ref, h_ref, acc_ref):
    j = pl.program_id(1)
    tm = x_ref.shape[0]

    @pl.when(j == 0)
    def _():
        h = _norm_mod(x_ref[...], g2_ref[...], mod_ref[:, 3 * D_MODEL:4 * D_MODEL],
                      mod_ref[:, 4 * D_MODEL:5 * D_MODEL])
        h_ref[...] = h.astype(BF16)
        acc_ref[...] = jnp.zeros_like(acc_ref)

    h = h_ref[...]
    a = _dot(h, wa_ref[...])
    b = _dot(h, wb_ref[...])
    col_in_row = lax.broadcasted_iota(jnp.int32, a.shape, 0) & (GRID_W - 1)
    a_prev = jnp.where(col_in_row == 0, 0.0, pltpu.roll(a, 1, 0))
    a_next = jnp.where(col_in_row == GRID_W - 1, 0.0, pltpu.roll(a, tm - 1, 0))
    ac = a_prev * cw_ref[0:1, :] + a * cw_ref[1:2, :] + a_next * cw_ref[2:3, :] + cb_ref[...]
    acc_ref[...] += _dot((jax.nn.gelu(ac) * b).astype(BF16), wd_ref[...])

    @pl.when(j == pl.num_programs(1) - 1)
    def _():
        x2 = x_ref[...] + mod_ref[:, 5 * D_MODEL:6 * D_MODEL] * acc_ref[...]
        ms = jnp.mean(x2 * x2, axis=-1, keepdims=True)
        o_ref[...] = x2 * lax.rsqrt(ms + NORM_EPS) * gf_ref[...]


def _ffn_call(x1, mod3, norm2_g, w_up16, conv_w, conv_b, w_down16, final_g, tm, tf, tiles_per_batch):
    m = x1.shape[0]
    nf = D_FF // tf
    return pl.pallas_call(
        _ffn_kernel,
        grid=(m // tm, nf),
        in_specs=[pl.BlockSpec((tm, D_MODEL), lambda i, j: (i, 0)),
                  pl.BlockSpec((None, 1, 6 * D_MODEL), lambda i, j: (i // tiles_per_batch, 0, 0)),
                  pl.BlockSpec((1, D_MODEL), lambda i, j: (0, 0)),
                  pl.BlockSpec((D_MODEL, tf), lambda i, j: (0, j)),
                  pl.BlockSpec((D_MODEL, tf), lambda i, j: (0, nf + j)),
                  pl.BlockSpec((3, tf), lambda i, j: (0, j)),
                  pl.BlockSpec((1, tf), lambda i, j: (0, j)),
                  pl.BlockSpec((tf, D_MODEL), lambda i, j: (j, 0)),
                  pl.BlockSpec((1, D_MODEL), lambda i, j: (0, 0))],
        out_specs=pl.BlockSpec((tm, D_MODEL), lambda i, j: (i, 0)),
        out_shape=jax.ShapeDtypeStruct((m, D_MODEL), F32),
        scratch_shapes=[pltpu.VMEM((tm, D_MODEL), BF16), pltpu.VMEM((tm, D_MODEL), F32)],
        compiler_params=pltpu.CompilerParams(
            dimension_semantics=("parallel", "arbitrary"), vmem_limit_bytes=VMEM_LIMIT),
        name="ffn",
    )(x1, mod3, norm2_g, w_up16, w_up16, conv_w, conv_b, w_down16, final_g)


def kernel(x, c, ctx, c_ctx, w_mod, b_mod, norm1_g, w_in, conv_qkv, a_log, dt_bias, onorm_g, w_proj_b,
           a_ln_g, a_ln_b, a_ws, a_bs, w_proj_a, w_out, norm2_g, w_up, ffn_conv_w, ffn_conv_b, w_down,
           final_g):
    bsz, length, _ = x.shape
    ctx_len = ctx.shape[1]
    assert w_mod.shape[0] == 1 and bsz <= 7
    assert length % 1024 == 0 and ctx_len % CHUNK == 0

    w = w_in[0]
    ba_cols = jnp.pad(w[:, OFF_BA:OFF_Q], ((0, 0), (0, LANES - 4 * N_HEADS)))
    w_lat = jnp.concatenate([w[:, :OFF_BA], w[:, OFF_Q:], ba_cols], axis=1).astype(BF16)
    w_ctx = jnp.concatenate([w[:, :OFF_BA], ba_cols], axis=1).astype(BF16)
    lane_pad = (2 * N_HEADS, LANES - 4 * N_HEADS)
    a_lane = jnp.pad(jnp.exp(a_log[0].astype(F32)).reshape(-1), lane_pad).reshape(1, LANES)
    dt_lane = jnp.pad(dt_bias[0].astype(F32).reshape(-1), lane_pad).reshape(1, LANES)
    bs_cols = jnp.repeat(a_bs[0].T, LANES, axis=1)

    cond8 = jnp.concatenate([c, c_ctx[None, :], jnp.zeros((7 - bsz, D_MODEL), F32)], axis=0)
    mod3 = _mod_call(cond8, w_mod[0], b_mod[0]).reshape(8, 1, 6 * D_MODEL)

    x2d = x.reshape(bsz * length, D_MODEL)
    tm = 512
    tiles_per_batch = length // tm
    g1 = norm1_g[0].reshape(1, D_MODEL)
    p2d = _inproj_call(x2d, mod3, lambda i: i // tiles_per_batch, g1, w_lat, tm, P_COLS // 5, "in_proj")
    pc2d = _inproj_call(ctx.reshape(bsz * ctx_len, D_MODEL), mod3, lambda i: bsz, g1, w_ctx,
                        ctx_len, w_ctx.shape[1], "in_proj_ctx")
    p3 = p2d.reshape(bsz, length, P_COLS)
    pc3 = pc2d.reshape(bsz, ctx_len, w_ctx.shape[1])

    gcol, grow = _gates_call(p3, COLBLK_BA, a_lane, dt_lane, "gates")
    cgcol, cgrow = _gates_call(pc3, 2 * N_HEADS, a_lane, dt_lane, "gates_ctx")
    og = _delta_call(p3, gcol, grow, pc3, cgcol, cgrow, conv_qkv[0], onorm_g[0].reshape(1, HEAD_DIM))

    ua = _gmlp_call(p2d, a_ln_g[0].reshape(1, -1), a_ln_b[0].reshape(1, -1), a_ws[0].astype(BF16), bs_cols, tm)
    x1 = _merge_call(ua, og.reshape(bsz * length, B_WIDTH), p2d, x2d, mod3, w_proj_a[0].astype(BF16),
                     w_proj_b[0].astype(BF16), w_out[0].astype(BF16), tm, tiles_per_batch)
    out = _ffn_call(x1, mod3, norm2_g[0].reshape(1, -1), w_up[0].astype(BF16), ffn_conv_w[0],
                    ffn_conv_b[0].reshape(1, -1), w_down[0].astype(BF16), final_g.reshape(1, -1),
                    tm, 256, tiles_per_batch)
    return out.reshape(bsz, length, D_MODEL)
```

```python
import functools

import jax
import jax.numpy as jnp
from jax import lax
from jax.experimental import pallas as pl
from jax.experimental.pallas import tpu as pltpu

F32 = jnp.float32
BF16 = jnp.bfloat16

D_MODEL = 1024
GRID_W = 64
NORM_EPS = 1e-6
N_HEADS = 8
HEAD_DIM = 128
B_WIDTH = N_HEADS * HEAD_DIM
A_GROUPS = 8
A_CHUNK = 128
D_FF = 2816
OFF_BA = 2 * B_WIDTH
OFF_Q = OFF_BA + 4 * N_HEADS
LANES = 128
P_COLS = 8 * D_MODEL + LANES
COLBLK_BA = 8 * D_MODEL // LANES
CHUNK = 128

VMEM_LIMIT = 56 * 1024 * 1024


def _silu(x):
    return x * jax.nn.sigmoid(x)


def _dot(a, b):
    return jnp.dot(a, b, preferred_element_type=F32)


def _dot_nt(a, b):
    return lax.dot_general(a, b, (((1,), (1,)), ((), ())), preferred_element_type=F32)


def _mod_kernel(c_ref, w_ref, b_ref, o_ref):
    cond = _silu(c_ref[...])
    o_ref[...] = jnp.dot(cond, w_ref[...], preferred_element_type=F32,
                         precision=lax.Precision.HIGHEST) + b_ref[...]


def _mod_call(cond8, w_mod, b_mod):
    n = w_mod.shape[1]
    tn = 1536
    return pl.pallas_call(
        _mod_kernel,
        grid=(n // tn,),
        in_specs=[pl.BlockSpec((8, D_MODEL), lambda j: (0, 0)),
                  pl.BlockSpec((D_MODEL, tn), lambda j: (0, j)),
                  pl.BlockSpec((1, tn), lambda j: (0, j))],
        out_specs=pl.BlockSpec((8, tn), lambda j: (0, j)),
        out_shape=jax.ShapeDtypeStruct((8, n), F32),
        name="mod",
    )(cond8, w_mod, b_mod.reshape(1, n))


def _norm_mod(x, g, shift, scale):
    ms = jnp.mean(x * x, axis=-1, keepdims=True)
    y = x * lax.rsqrt(ms + NORM_EPS) * g
    return y * (1.0 + scale) + shift


def _inproj_kernel(x_ref, mod_ref, g_ref, w_ref, o_ref, h_ref):
    @pl.when(pl.program_id(1) == 0)
    def _():
        h = _norm_mod(x_ref[...], g_ref[...], mod_ref[:, 0:D_MODEL], mod_ref[:, D_MODEL:2 * D_MODEL])
        h_ref[...] = h.astype(BF16)

    o_ref[...] = _dot(h_ref[...], w_ref[...])


def _inproj_call(x2d, mod3, mod_row_of_tile, norm_g, w, tm, tn, name):
    m = x2d.shape[0]
    n = w.shape[1]
    return pl.pallas_call(
        _inproj_kernel,
        grid=(m // tm, n // tn),
        in_specs=[pl.BlockSpec((tm, D_MODEL), lambda i, j: (i, 0)),
                  pl.BlockSpec((None, 1, 6 * D_MODEL), lambda i, j: (mod_row_of_tile(i), 0, 0)),
                  pl.BlockSpec((1, D_MODEL), lambda i, j: (0, 0)),
                  pl.BlockSpec((D_MODEL, tn), lambda i, j: (0, j))],
        out_specs=pl.BlockSpec((tm, tn), lambda i, j: (i, j)),
        out_shape=jax.ShapeDtypeStruct((m, n), F32),
        scratch_shapes=[pltpu.VMEM((tm, D_MODEL), BF16)],
        compiler_params=pltpu.CompilerParams(
            dimension_semantics=("parallel", "arbitrary"), vmem_limit_bytes=VMEM_LIMIT),
        name=name,
    )(x2d, mod3, norm_g, w)


def _gates_kernel(ba_ref, a_ref, dt_ref, col_ref, row_ref, *, n_sub):
    lane = lax.broadcasted_iota(jnp.int32, (CHUNK, LANES), 1)
    ri = lax.broadcasted_iota(jnp.int32, (CHUNK, CHUNK), 0)
    ci = lax.broadcasted_iota(jnp.int32, (CHUNK, CHUNK), 1)
    tri_lo = (ri >= ci).astype(F32)
    tri_up = (ri <= ci).astype(F32)
    for s in range(n_sub):
        x = ba_ref[s * CHUNK:(s + 1) * CHUNK, :]
        beta = jax.nn.sigmoid(x)
        y = x + dt_ref[...]
        softplus = jnp.maximum(y, 0.0) + jnp.log1p(jnp.exp(-jnp.abs(y)))
        g = jnp.where((lane >= 2 * N_HEADS) & (lane < 4 * N_HEADS), -a_ref[...] * softplus, 0.0)
        pre = jnp.dot(tri_lo, g, preferred_element_type=F32, precision=lax.Precision.HIGHEST)
        suf = jnp.dot(tri_up, g, preferred_element_type=F32, precision=lax.Precision.HIGHEST)
        col = jnp.where(lane < 2 * N_HEADS, beta, jnp.where(lane < 3 * N_HEADS, pre, suf))
        col_ref[s * CHUNK:(s + 1) * CHUNK, :] = col
        row_ref[:, s * CHUNK:(s + 1) * CHUNK] = col.T


def _gates_call(p3, col_block, a_lane, dt_lane, name):
    bsz, length, _ = p3.shape
    tl = min(length, 1024)
    return pl.pallas_call(
        functools.partial(_gates_kernel, n_sub=tl // CHUNK),
        grid=(bsz, length // tl),
        in_specs=[pl.BlockSpec((None, tl, LANES), lambda b, i: (b, i, col_block)),
                  pl.BlockSpec((1, LANES), lambda b, i: (0, 0)),
                  pl.BlockSpec((1, LANES), lambda b, i: (0, 0))],
        out_specs=[pl.BlockSpec((None, tl, LANES), lambda b, i: (b, i, 0)),
                   pl.BlockSpec((None, LANES, tl), lambda b, i: (b, 0, i))],
        out_shape=[jax.ShapeDtypeStruct((bsz, length, LANES), F32),
                   jax.ShapeDtypeStruct((bsz, LANES, length), F32)],
        name=name,
    )(p3, a_lane, dt_lane)


def _tri_inverse_many(n_mats, ri, ci):
    def same_block(size):
        shift = size.bit_length() - 1
        return (ri >> shift) == (ci >> shift)

    eye = (ri == ci).astype(F32)
    blk16 = same_block(16)
    n16 = [jnp.where(blk16, n, 0.0) for n in n_mats]
    ts = [eye + n for n in n16]
    pbs = [n.astype(BF16) for n in n16]
    pbs = [_dot(p, p).astype(BF16) for p in pbs]
    for _ in range(2):
        ts = [t + _dot(t.astype(BF16), p) for t, p in zip(ts, pbs)]
        pbs = [_dot(p, p).astype(BF16) for p in pbs]
    ts = [t + _dot(t.astype(BF16), p) for t, p in zip(ts, pbs)]
    for size in (32, 64, 128):
        off_mask = same_block(size) & jnp.logical_not(same_block(size // 2))
        tbs = [t.astype(BF16) for t in ts]
        xs = [_dot(jnp.where(off_mask, n, 0.0).astype(BF16), tb).astype(BF16) for n, tb in zip(n_mats, tbs)]
        ts = [t + _dot(tb, x) for t, tb, x in zip(ts, tbs, xs)]
    return ts


def _chunk_start(c):
    return c * CHUNK if isinstance(c, int) else pl.multiple_of(c * CHUNK, CHUNK)


def _conv_act(p_ref, w_ref, c, n_chunks):
    length = n_chunks * CHUNK
    r0 = _chunk_start(c)
    cur = p_ref[pl.ds(r0, CHUNK), :]
    if isinstance(c, int):
        lo, hi = max(r0 - 8, 0), min(r0 + CHUNK, length - 8)
    else:
        lo = pl.multiple_of(jnp.maximum(r0 - 8, 0), 8)
        hi = pl.multiple_of(jnp.minimum(r0 + CHUNK, length - 8), 8)
    prev_row = jnp.where(c > 0, p_ref[pl.ds(lo, 8), :][7:8, :], 0.0)
    next_row = jnp.where(c < n_chunks - 1, p_ref[pl.ds(hi, 8), :][0:1, :], 0.0)
    row = lax.broadcasted_iota(jnp.int32, (CHUNK, LANES), 0)
    xp = jnp.where(row == 0, prev_row, pltpu.roll(cur, 1, 0))
    xn = jnp.where(row == CHUNK - 1, next_row, pltpu.roll(cur, CHUNK - 1, 0))
    y = xp * w_ref[0:1, :] + cur * w_ref[1:2, :] + xn * w_ref[2:3, :]
    return _silu(y)


def _l2norm(t):
    return t * lax.rsqrt(jnp.sum(t * t, axis=-1, keepdims=True) + NORM_EPS)


def _prep_chunks(chunks, n_chunks, h, pk_ref, pv_ref, pq_ref, gcol_ref, grow_ref, wk_ref, wv_ref, wq_ref):
    lane = lax.broadcasted_iota(jnp.int32, (CHUNK, LANES), 1)
    ri = lax.broadcasted_iota(jnp.int32, (CHUNK, CHUNK), 0)
    ci = lax.broadcasted_iota(jnp.int32, (CHUNK, CHUNK), 1)
    head_row = lax.broadcasted_iota(jnp.int32, (N_HEADS, CHUNK), 0)
    incl = (ri >= ci, ri <= ci)
    strict = (ri > ci, ri < ci)

    acts = []
    for c in chunks:
        k = _l2norm(_conv_act(pk_ref, wk_ref, c, n_chunks))
        v = _conv_act(pv_ref, wv_ref, c, n_chunks)
        q = None
        if pq_ref is not None:
            q = _l2norm(_conv_act(pq_ref, wq_ref, c, n_chunks)) * (HEAD_DIM ** -0.5)
        acts.append((k, v, q))
    prods = []
    for k, v, q in acts:
        kb16 = k.astype(BF16)
        if q is not None:
            aa = _dot_nt(jnp.concatenate([kb16, q.astype(BF16)], axis=0), kb16)
            prods.append((aa[:CHUNK], aa[CHUNK:]))
        else:
            prods.append((_dot_nt(kb16, kb16), None))

    chains = []
    for c, (k, v, q), (a_kk, a_qk) in zip(chunks, acts, prods):
        r0 = _chunk_start(c)
        gates = gcol_ref[pl.ds(r0, CHUNK), :]
        for d in range(2):
            beta = jnp.sum(jnp.where(lane == d * N_HEADS + h, gates, 0.0), axis=1, keepdims=True)
            gc = jnp.sum(jnp.where(lane == (2 + d) * N_HEADS + h, gates, 0.0), axis=1, keepdims=True)
            all_heads = grow_ref[(2 + d) * N_HEADS:(3 + d) * N_HEADS, pl.ds(r0, CHUNK)]
            gc_row = jnp.sum(jnp.where(head_row == h, all_heads, 0.0), axis=0, keepdims=True)
            gamma = jnp.where(incl[d], jnp.exp(jnp.where(incl[d], gc - gc_row, 0.0)), 0.0)
            neg_m = jnp.where(strict[d], -(a_kk * beta * gamma), 0.0)
            chains.append((k, v, q, a_qk, d, beta, gc, gamma, neg_m))

    t_invs = _tri_inverse_many([ch[-1] for ch in chains], ri, ci)
    rhss = [jnp.concatenate([k * (beta * jnp.exp(gc)), v * beta], axis=1).astype(BF16)
            for k, v, _, _, _, beta, gc, _, _ in chains]
    wus = [_dot(t.astype(BF16), rhs) for t, rhs in zip(t_invs, rhss)]

    out = []
    for (k, v, q, a_qk, d, beta, gc, gamma, _), wu in zip(chains, wus):
        w, u = wu[:, :HEAD_DIM], wu[:, HEAD_DIM:]
        g_end = gc[CHUNK - 1:CHUNK, :] if d == 0 else gc[0:1, :]
        kd = k * jnp.exp(g_end - gc)
        decay = jnp.broadcast_to(jnp.exp(g_end), (8, LANES))
        if q is not None:
            qg = q * jnp.exp(gc)
            qk = jnp.where(incl[d], a_qk * gamma, 0.0)
        else:
            qg = qk = None
        out.append((w, u, kd, decay, qg, qk))
    return [out[2 * i:2 * i + 2] for i in range(len(chunks))]


def _delta_kernel(pk_ref, pv_ref, pq_ref, pz_ref, gcol_ref, grow_ref, ck_ref, cv_ref, cgcol_ref, cgrow_ref,
                  wk_ref, wv_ref, wq_ref, onorm_ref, og_ref, wq_sc, kq_sc, u_sc, ge_sc, o_sc,
                  *, n_chunks, n_ctx_chunks, chunks_per_prep):
    h = pl.program_id(1)

    ctx = _prep_chunks(list(range(n_ctx_chunks)), n_ctx_chunks, h, ck_ref, cv_ref, None, cgcol_ref, cgrow_ref,
                       wk_ref, wv_ref, None)
    states = [jnp.zeros((HEAD_DIM, HEAD_DIM), F32)] * 2
    for c in range(n_ctx_chunks):
        steps = [ctx[c][0], ctx[n_ctx_chunks - 1 - c][1]]
        ws = [_dot(st[0].astype(BF16), s.astype(BF16)) for st, s in zip(steps, states)]
        v_new = [st[1] - x for st, x in zip(steps, ws)]
        kv = [_dot(st[2].T.astype(BF16), vn.astype(BF16)) for st, vn in zip(steps, v_new)]
        states = [s * st[3][0:1, :] + x for s, st, x in zip(states, steps, kv)]

    def prep_body(i, carry):
        chunks = [i * chunks_per_prep + g for g in range(chunks_per_prep)]
        prepped = _prep_chunks(chunks, n_chunks, h, pk_ref, pv_ref, pq_ref, gcol_ref, grow_ref,
                               wk_ref, wv_ref, wq_ref)
        for c, both in zip(chunks, prepped):
            r0 = _chunk_start(c)
            for d in range(2):
                w, u, kd, decay, qg, qk = both[d]
                wq_sc[d, c] = jnp.concatenate([w, qg], axis=0).astype(BF16)
                kq_sc[d, c] = jnp.concatenate([kd.T, qk], axis=0).astype(BF16)
                u_sc[d, pl.ds(r0, CHUNK), :] = u
                ge_sc[d, c] = decay
        return carry

    lax.fori_loop(0, n_chunks // chunks_per_prep, prep_body, 0)

    def seq_steps(states, chunks):
        ws = [_dot(wq_sc[d, c], s.astype(BF16)) for d, (c, s) in enumerate(zip(chunks, states))]
        v_new = [u_sc[d, pl.ds(_chunk_start(c), CHUNK), :] - x[:CHUNK] for d, (c, x) in enumerate(zip(chunks, ws))]
        kv = [_dot(kq_sc[d, c], vn.astype(BF16)) for d, (c, vn) in enumerate(zip(chunks, v_new))]
        nxt = [s * ge_sc[d, c][0:1, :] + x[:CHUNK] for d, (c, s, x) in enumerate(zip(chunks, states, kv))]
        return tuple(nxt), [x[CHUNK:] + y[CHUNK:] for x, y in zip(ws, kv)]

    def finish(c, o):
        r0 = _chunk_start(c)
        y = o * lax.rsqrt(jnp.mean(o * o, axis=-1, keepdims=True) + NORM_EPS) * onorm_ref[...]
        og_ref[pl.ds(r0, CHUNK), :] = (y * _silu(pz_ref[pl.ds(r0, CHUNK), :])).astype(og_ref.dtype)

    def first_half(i, carry):
        chunks = (i, n_chunks - 1 - i)
        carry, outs = seq_steps(carry, chunks)
        for c, o in zip(chunks, outs):
            o_sc[pl.ds(_chunk_start(c), CHUNK), :] = o
        return carry

    def second_half(i, carry):
        chunks = (i, n_chunks - 1 - i)
        carry, outs = seq_steps(carry, chunks)
        for c, o in zip(chunks, outs):
            finish(c, o_sc[pl.ds(_chunk_start(c), CHUNK), :] + o)
        return carry

    carry = lax.fori_loop(0, n_chunks // 2, first_half, tuple(states))
    lax.fori_loop(n_chunks // 2, n_chunks, second_half, carry)


def _delta_call(p3, gcol, grow, pc3, cgcol, cgrow, conv_w, onorm_g):
    bsz, length, _ = p3.shape
    ctx_len = pc3.shape[1]
    n_chunks = length // CHUNK
    n_ctx_chunks = ctx_len // CHUNK
    chunks_per_prep = 4
    assert n_chunks % 2 == 0 and n_chunks % chunks_per_prep == 0

    def col_spec(rows, first_block):
        return pl.BlockSpec((None, rows, LANES), lambda b, h: (b, 0, first_block + h))

    return pl.pallas_call(
        functools.partial(_delta_kernel, n_chunks=n_chunks, n_ctx_chunks=n_ctx_chunks,
                          chunks_per_prep=chunks_per_prep),
        grid=(bsz, N_HEADS),
        in_specs=[col_spec(length, 0), col_spec(length, N_HEADS), col_spec(length, 2 * N_HEADS),
                  col_spec(length, 3 * N_HEADS),
                  pl.BlockSpec((None, length, LANES), lambda b, h: (b, 0, 0)),
                  pl.BlockSpec((None, LANES, length), lambda b, h: (b, 0, 0)),
                  col_spec(ctx_len, 0), col_spec(ctx_len, N_HEADS),
                  pl.BlockSpec((None, ctx_len, LANES), lambda b, h: (b, 0, 0)),
                  pl.BlockSpec((None, LANES, ctx_len), lambda b, h: (b, 0, 0)),
                  pl.BlockSpec((3, LANES), lambda b, h: (0, h)),
                  pl.BlockSpec((3, LANES), lambda b, h: (0, N_HEADS + h)),
                  pl.BlockSpec((3, LANES), lambda b, h: (0, 2 * N_HEADS + h)),
                  pl.BlockSpec((1, LANES), lambda b, h: (0, 0))],
        out_specs=pl.BlockSpec((None, length, LANES), lambda b, h: (b, 0, h)),
        out_shape=jax.ShapeDtypeStruct((bsz, length, B_WIDTH), BF16),
        scratch_shapes=[pltpu.VMEM((2, n_chunks, 2 * CHUNK, LANES), BF16),
                        pltpu.VMEM((2, n_chunks, 2 * CHUNK, LANES), BF16),
                        pltpu.VMEM((2, length, LANES), F32),
                        pltpu.VMEM((2, n_chunks, 8, LANES), F32),
                        pltpu.VMEM((length, LANES), F32)],
        compiler_params=pltpu.CompilerParams(
            dimension_semantics=("parallel", "parallel"), vmem_limit_bytes=VMEM_LIMIT),
        name="delta",
    )(p3, p3, p3, p3, gcol, grow, pc3, pc3, cgcol, cgrow, conv_w, conv_w, conv_w, onorm_g)


def _gmlp_kernel(u_ref, v_ref, lng_ref, lnb_ref, ws_ref, bs_ref, o_ref, *, n_sub):
    v = jax.nn.gelu(v_ref[...])
    mu = jnp.mean(v, axis=-1, keepdims=True)
    vc = v - mu
    vn = vc * lax.rsqrt(jnp.mean(vc * vc, axis=-1, keepdims=True) + NORM_EPS) * lng_ref[...] + lnb_ref[...]
    vb = vn.astype(BF16)
    for n in range(n_sub):
        rows = slice(n * A_CHUNK, (n + 1) * A_CHUNK)
        for g in range(A_GROUPS):
            cols = slice(g * LANES, (g + 1) * LANES)
            s = _dot(ws_ref[g], vb[rows, cols]) + bs_ref[:, cols]
            o_ref[rows, cols] = (jax.nn.gelu(u_ref[rows, cols]) * s).astype(o_ref.dtype)


def _gmlp_call(p2d, ln_g, ln_b, ws16, bs_cols, tm):
    m = p2d.shape[0]
    return pl.pallas_call(
        functools.partial(_gmlp_kernel, n_sub=tm // A_CHUNK),
        grid=(m // tm,),
        in_specs=[pl.BlockSpec((tm, D_MODEL), lambda i: (i, 4)),
                  pl.BlockSpec((tm, D_MODEL), lambda i: (i, 5)),
                  pl.BlockSpec((1, D_MODEL), lambda i: (0, 0)),
                  pl.BlockSpec((1, D_MODEL), lambda i: (0, 0)),
                  pl.BlockSpec((A_GROUPS, A_CHUNK, A_CHUNK), lambda i: (0, 0, 0)),
                  pl.BlockSpec((A_CHUNK, D_MODEL), lambda i: (0, 0))],
        out_specs=pl.BlockSpec((tm, D_MODEL), lambda i: (i, 0)),
        out_shape=jax.ShapeDtypeStruct((m, D_MODEL), BF16),
        compiler_params=pltpu.CompilerParams(
            dimension_semantics=("parallel",), vmem_limit_bytes=VMEM_LIMIT),
        name="gmlp",
    )(p2d, p2d, ln_g, ln_b, ws16, bs_cols)


def _merge_kernel(ua_ref, og_ref, ga_ref, gb_ref, x_ref, mod_ref, wa_ref, wb_ref, wo_ref, o_ref):
    y_a = _dot(ua_ref[...], wa_ref[...])
    y_b = _dot(og_ref[...], wb_ref[...])
    t = jax.nn.sigmoid(ga_ref[...]) * y_a + jax.nn.sigmoid(gb_ref[...]) * y_b
    y = _dot(t.astype(BF16), wo_ref[...])
    o_ref[...] = x_ref[...] + mod_ref[:, 2 * D_MODEL:3 * D_MODEL] * y


def _merge_call(ua, og, p2d, x2d, mod3, wa16, wb16, wo16, tm, tiles_per_batch):
    m = x2d.shape[0]
    full = pl.BlockSpec((D_MODEL, D_MODEL), lambda i: (0, 0))
    rows = pl.BlockSpec((tm, D_MODEL), lambda i: (i, 0))
    return pl.pallas_call(
        _merge_kernel,
        grid=(m // tm,),
        in_specs=[rows, rows,
                  pl.BlockSpec((tm, D_MODEL), lambda i: (i, 6)),
                  pl.BlockSpec((tm, D_MODEL), lambda i: (i, 7)),
                  rows,
                  pl.BlockSpec((None, 1, 6 * D_MODEL), lambda i: (i // tiles_per_batch, 0, 0)),
                  full, full, full],
        out_specs=rows,
        out_shape=jax.ShapeDtypeStruct((m, D_MODEL), F32),
        compiler_params=pltpu.CompilerParams(
            dimension_semantics=("parallel",), vmem_limit_bytes=VMEM_LIMIT),
        name="merge",
    )(ua, og, p2d, p2d, x2d, mod3, wa16, wb16, wo16)


def _ffn_kernel(x_ref, mod_ref, g2_ref, wa_ref, wb_ref, cw_ref, cb_ref, wd_ref, gf_ref, o_ref, h_ref, acc_ref):
    j = pl.program_id(1)
    tm = x_ref.shape[0]

    @pl.when(j == 0)
    def _():
        h = _norm_mod(x_ref[...], g2_ref[...], mod_ref[:, 3 * D_MODEL:4 * D_MODEL],
                      mod_ref[:, 4 * D_MODEL:5 * D_MODEL])
        h_ref[...] = h.astype(BF16)
        acc_ref[...] = jnp.zeros_like(acc_ref)

    h = h_ref[...]
    a = _dot(h, wa_ref[...])
    b = _dot(h, wb_ref[...])
    col_in_row = lax.broadcasted_iota(jnp.int32, a.shape, 0) & (GRID_W - 1)
    a_prev = jnp.where(col_in_row == 0, 0.0, pltpu.roll(a, 1, 0))
    a_next = jnp.where(col_in_row == GRID_W - 1, 0.0, pltpu.roll(a, tm - 1, 0))
    ac = a_prev * cw_ref[0:1, :] + a * cw_ref[1:2, :] + a_next * cw_ref[2:3, :] + cb_ref[...]
    acc_ref[...] += _dot((jax.nn.gelu(ac) * b).astype(BF16), wd_ref[...])

    @pl.when(j == pl.num_programs(1) - 1)
    def _():
        x2 = x_ref[...] + mod_ref[:, 5 * D_MODEL:6 * D_MODEL] * acc_ref[...]
        ms = jnp.mean(x2 * x2, axis=-1, keepdims=True)
        o_ref[...] = x2 * lax.rsqrt(ms + NORM_EPS) * gf_ref[...]


def _ffn_call(x1, mod3, norm2_g, w_up16, conv_w, conv_b, w_down16, final_g, tm, tf, tiles_per_batch):
    m = x1.shape[0]
    nf = D_FF // tf
    return pl.pallas_call(
        _ffn_kernel,
        grid=(m // tm, nf),
        in_specs=[pl.BlockSpec((tm, D_MODEL), lambda i, j: (i, 0)),
                  pl.BlockSpec((None, 1, 6 * D_MODEL), lambda i, j: (i // tiles_per_batch, 0, 0)),
                  pl.BlockSpec((1, D_MODEL), lambda i, j: (0, 0)),
                  pl.BlockSpec((D_MODEL, tf), lambda i, j: (0, j)),
                  pl.BlockSpec((D_MODEL, tf), lambda i, j: (0, nf + j)),
                  pl.BlockSpec((3, tf), lambda i, j: (0, j)),
                  pl.BlockSpec((1, tf), lambda i, j: (0, j)),
                  pl.BlockSpec((tf, D_MODEL), lambda i, j: (j, 0)),
                  pl.BlockSpec((1, D_MODEL), lambda i, j: (0, 0))],
        out_specs=pl.BlockSpec((tm, D_MODEL), lambda i, j: (i, 0)),
        out_shape=jax.ShapeDtypeStruct((m, D_MODEL), F32),
        scratch_shapes=[pltpu.VMEM((tm, D_MODEL), BF16), pltpu.VMEM((tm, D_MODEL), F32)],
        compiler_params=pltpu.CompilerParams(
            dimension_semantics=("parallel", "arbitrary"), vmem_limit_bytes=VMEM_LIMIT),
        name="ffn",
    )(x1, mod3, norm2_g, w_up16, w_up16, conv_w, conv_b, w_down16, final_g)


def kernel(x, c, ctx, c_ctx, w_mod, b_mod, norm1_g, w_in, conv_qkv, a_log, dt_bias, onorm_g, w_proj_b,
           a_ln_g, a_ln_b, a_ws, a_bs, w_proj_a, w_out, norm2_g, w_up, ffn_conv_w, ffn_conv_b, w_down,
           final_g):
    bsz, length, _ = x.shape
    ctx_len = ctx.shape[1]
    assert w_mod.shape[0] == 1 and bsz <= 7
    assert length % 1024 == 0 and ctx_len % CHUNK == 0

    w = w_in[0]
    ba_cols = jnp.pad(w[:, OFF_BA:OFF_Q], ((0, 0), (0, LANES - 4 * N_HEADS)))
    w_lat = jnp.concatenate([w[:, :OFF_BA], w[:, OFF_Q:], ba_cols], axis=1).astype(BF16)
    w_ctx = jnp.concatenate([w[:, :OFF_BA], ba_cols], axis=1).astype(BF16)
    lane_pad = (2 * N_HEADS, LANES - 4 * N_HEADS)
    a_lane = jnp.pad(jnp.exp(a_log[0].astype(F32)).reshape(-1), lane_pad).reshape(1, LANES)
    dt_lane = jnp.pad(dt_bias[0].astype(F32).reshape(-1), lane_pad).reshape(1, LANES)
    bs_cols = jnp.repeat(a_bs[0].T, LANES, axis=1)

    cond8 = jnp.concatenate([c, c_ctx[None, :], jnp.zeros((7 - bsz, D_MODEL), F32)], axis=0)
    mod3 = _mod_call(cond8, w_mod[0], b_mod[0]).reshape(8, 1, 6 * D_MODEL)

    x2d = x.reshape(bsz * length, D_MODEL)
    tm = 512
    tiles_per_batch = length // tm
    g1 = norm1_g[0].reshape(1, D_MODEL)
    p2d = _inproj_call(x2d, mod3, lambda i: i // tiles_per_batch, g1, w_lat, tm, P_COLS // 5, "in_proj")
    pc2d = _inproj_call(ctx.reshape(bsz * ctx_len, D_MODEL), mod3, lambda i: bsz, g1, w_ctx,
                        ctx_len, w_ctx.shape[1], "in_proj_ctx")
    p3 = p2d.reshape(bsz, length, P_COLS)
    pc3 = pc2d.reshape(bsz, ctx_len, w_ctx.shape[1])

    gcol, grow = _gates_call(p3, COLBLK_BA, a_lane, dt_lane, "gates")
    cgcol, cgrow = _gates_call(pc3, 2 * N_HEADS, a_lane, dt_lane, "gates_ctx")
    og = _delta_call(p3, gcol, grow, pc3, cgcol, cgrow, conv_qkv[0], onorm_g[0].reshape(1, HEAD_DIM))

    ua = _gmlp_call(p2d, a_ln_g[0].reshape(1, -1), a_ln_b[0].reshape(1, -1), a_ws[0].astype(BF16), bs_cols, tm)
    x1 = _merge_call(ua, og.reshape(bsz * length, B_WIDTH), p2d, x2d, mod3, w_proj_a[0].astype(BF16),
                     w_proj_b[0].astype(BF16), w_out[0].astype(BF16), tm, tiles_per_batch)
    out = _ffn_call(x1, mod3, norm2_g[0].reshape(1, -1), w_up[0].astype(BF16), ffn_conv_w[0],
                    ffn_conv_b[0].reshape(1, -1), w_down[0].astype(BF16), final_g.reshape(1, -1),
                    tm, 256, tiles_per_batch)
    return out.reshape(bsz, length, D_MODEL)
```

```python
import functools

import jax
import jax.numpy as jnp
from jax import lax
from jax.experimental import pallas as pl
from jax.experimental.pallas import tpu as pltpu

F32 = jnp.float32
BF16 = jnp.bfloat16

D_MODEL = 1024
GRID_W = 64
NORM_EPS = 1e-6
N_HEADS = 8
HEAD_DIM = 128
B_WIDTH = N_HEADS * HEAD_DIM
A_GROUPS = 8
A_CHUNK = 128
D_FF = 2816
OFF_BA = 2 * B_WIDTH
OFF_Q = OFF_BA + 4 * N_HEADS
OFF_U = OFF_Q + 2 * B_WIDTH
OFF_GA = OFF_U + 2 * D_MODEL
LANES = 128
P_COLS = 4 * B_WIDTH + LANES
COLBLK_BA = 4 * B_WIDTH // LANES
CHUNK = 128

VMEM_LIMIT = 56 * 1024 * 1024


def _silu(x):
    return x * jax.nn.sigmoid(x)


def _dot(a, b):
    return jnp.dot(a, b, preferred_element_type=F32)


def _dot_nt(a, b):
    return lax.dot_general(a, b, (((1,), (1,)), ((), ())), preferred_element_type=F32)


def _mod_kernel(c_ref, w_ref, b_ref, o_ref):
    cond = _silu(c_ref[...])
    o_ref[...] = jnp.dot(cond, w_ref[...], preferred_element_type=F32,
                         precision=lax.Precision.HIGHEST) + b_ref[...]


def _mod_call(cond8, w_mod, b_mod):
    n = w_mod.shape[1]
    tn = 1536
    return pl.pallas_call(
        _mod_kernel,
        grid=(n // tn,),
        in_specs=[pl.BlockSpec((8, D_MODEL), lambda j: (0, 0)),
                  pl.BlockSpec((D_MODEL, tn), lambda j: (0, j)),
                  pl.BlockSpec((1, tn), lambda j: (0, j))],
        out_specs=pl.BlockSpec((8, tn), lambda j: (0, j)),
        out_shape=jax.ShapeDtypeStruct((8, n), F32),
        name="mod",
    )(cond8, w_mod, b_mod.reshape(1, n))


def _norm_mod(x, g, shift, scale):
    ms = jnp.mean(x * x, axis=-1, keepdims=True)
    y = x * lax.rsqrt(ms + NORM_EPS) * g
    return y * (1.0 + scale) + shift


def _inproj_kernel(x_ref, mod_ref, g_ref, w_ref, o_ref, h_ref):
    @pl.when(pl.program_id(1) == 0)
    def _():
        h = _norm_mod(x_ref[...], g_ref[...], mod_ref[:, 0:D_MODEL], mod_ref[:, D_MODEL:2 * D_MODEL])
        h_ref[...] = h.astype(BF16)

    o_ref[...] = _dot(h_ref[...], w_ref[...])


def _inproj_call(x2d, mod3, mod_row_of_tile, norm_g, w, tm, tn, name):
    m = x2d.shape[0]
    n = w.shape[1]
    return pl.pallas_call(
        _inproj_kernel,
        grid=(m // tm, n // tn),
        in_specs=[pl.BlockSpec((tm, D_MODEL), lambda i, j: (i, 0)),
                  pl.BlockSpec((None, 1, 6 * D_MODEL), lambda i, j: (mod_row_of_tile(i), 0, 0)),
                  pl.BlockSpec((1, D_MODEL), lambda i, j: (0, 0)),
                  pl.BlockSpec((D_MODEL, tn), lambda i, j: (0, j))],
        out_specs=pl.BlockSpec((tm, tn), lambda i, j: (i, j)),
        out_shape=jax.ShapeDtypeStruct((m, n), F32),
        scratch_shapes=[pltpu.VMEM((tm, D_MODEL), BF16)],
        compiler_params=pltpu.CompilerParams(
            dimension_semantics=("parallel", "arbitrary"), vmem_limit_bytes=VMEM_LIMIT),
        name=name,
    )(x2d, mod3, norm_g, w)


def _gates_kernel(ba_ref, a_ref, dt_ref, col_ref, row_ref, *, n_sub):
    lane = lax.broadcasted_iota(jnp.int32, (CHUNK, LANES), 1)
    ri = lax.broadcasted_iota(jnp.int32, (CHUNK, CHUNK), 0)
    ci = lax.broadcasted_iota(jnp.int32, (CHUNK, CHUNK), 1)
    tri_lo = (ri >= ci).astype(F32)
    tri_up = (ri <= ci).astype(F32)
    for s in range(n_sub):
        x = ba_ref[s * CHUNK:(s + 1) * CHUNK, :]
        beta = jax.nn.sigmoid(x)
        y = x + dt_ref[...]
        softplus = jnp.maximum(y, 0.0) + jnp.log1p(jnp.exp(-jnp.abs(y)))
        g = jnp.where((lane >= 2 * N_HEADS) & (lane < 4 * N_HEADS), -a_ref[...] * softplus, 0.0)
        pre = jnp.dot(tri_lo, g, preferred_element_type=F32, precision=lax.Precision.HIGHEST)
        suf = jnp.dot(tri_up, g, preferred_element_type=F32, precision=lax.Precision.HIGHEST)
        col = jnp.where(lane < 2 * N_HEADS, beta, jnp.where(lane < 3 * N_HEADS, pre, suf))
        col_ref[s * CHUNK:(s + 1) * CHUNK, :] = col
        row_ref[:, s * CHUNK:(s + 1) * CHUNK] = col.T


def _gates_call(p3, col_block, a_lane, dt_lane, name):
    bsz, length, _ = p3.shape
    tl = min(length, 1024)
    return pl.pallas_call(
        functools.partial(_gates_kernel, n_sub=tl // CHUNK),
        grid=(bsz, length // tl),
        in_specs=[pl.BlockSpec((None, tl, LANES), lambda b, i: (b, i, col_block)),
                  pl.BlockSpec((1, LANES), lambda b, i: (0, 0)),
                  pl.BlockSpec((1, LANES), lambda b, i: (0, 0))],
        out_specs=[pl.BlockSpec((None, tl, LANES), lambda b, i: (b, i, 0)),
                   pl.BlockSpec((None, LANES, tl), lambda b, i: (b, 0, i))],
        out_shape=[jax.ShapeDtypeStruct((bsz, length, LANES), F32),
                   jax.ShapeDtypeStruct((bsz, LANES, length), F32)],
        name=name,
    )(p3, a_lane, dt_lane)


def _tri_inverse_many(n_mats, ri, ci):
    def same_block(size):
        shift = size.bit_length() - 1
        return (ri >> shift) == (ci >> shift)

    eye = (ri == ci).astype(F32)
    blk16 = same_block(16)
    n16 = [jnp.where(blk16, n, 0.0) for n in n_mats]
    ts = [eye + n for n in n16]
    pbs = [n.astype(BF16) for n in n16]
    pbs = [_dot(p, p).astype(BF16) for p in pbs]
    for _ in range(2):
        ts = [t + _dot(t.astype(BF16), p) for t, p in zip(ts, pbs)]
        pbs = [_dot(p, p).astype(BF16) for p in pbs]
    ts = [t + _dot(t.astype(BF16), p) for t, p in zip(ts, pbs)]
    for size in (32, 64, 128):
        off_mask = same_block(size) & jnp.logical_not(same_block(size // 2))
        tbs = [t.astype(BF16) for t in ts]
        xs = [_dot(jnp.where(off_mask, n, 0.0).astype(BF16), tb).astype(BF16) for n, tb in zip(n_mats, tbs)]
        ts = [t + _dot(tb, x) for t, tb, x in zip(ts, tbs, xs)]
    return ts


def _chunk_start(c):
    return c * CHUNK if isinstance(c, int) else pl.multiple_of(c * CHUNK, CHUNK)


def _conv_act(p_ref, w_ref, c, n_chunks):
    length = n_chunks * CHUNK
    r0 = _chunk_start(c)
    cur = p_ref[pl.ds(r0, CHUNK), :]
    if isinstance(c, int):
        lo, hi = max(r0 - 8, 0), min(r0 + CHUNK, length - 8)
    else:
        lo = pl.multiple_of(jnp.maximum(r0 - 8, 0), 8)
        hi = pl.multiple_of(jnp.minimum(r0 + CHUNK, length - 8), 8)
    prev_row = jnp.where(c > 0, p_ref[pl.ds(lo, 8), :][7:8, :], 0.0)
    next_row = jnp.where(c < n_chunks - 1, p_ref[pl.ds(hi, 8), :][0:1, :], 0.0)
    row = lax.broadcasted_iota(jnp.int32, (CHUNK, LANES), 0)
    xp = jnp.where(row == 0, prev_row, pltpu.roll(cur, 1, 0))
    xn = jnp.where(row == CHUNK - 1, next_row, pltpu.roll(cur, CHUNK - 1, 0))
    y = xp * w_ref[0:1, :] + cur * w_ref[1:2, :] + xn * w_ref[2:3, :]
    return _silu(y)


def _l2norm(t):
    return t * lax.rsqrt(jnp.sum(t * t, axis=-1, keepdims=True) + NORM_EPS)


def _prep_chunks(chunks, n_chunks, h, pk_ref, pv_ref, pq_ref, gcol_ref, grow_ref, wk_ref, wv_ref, wq_ref):
    lane = lax.broadcasted_iota(jnp.int32, (CHUNK, LANES), 1)
    ri = lax.broadcasted_iota(jnp.int32, (CHUNK, CHUNK), 0)
    ci = lax.broadcasted_iota(jnp.int32, (CHUNK, CHUNK), 1)
    head_row = lax.broadcasted_iota(jnp.int32, (N_HEADS, CHUNK), 0)
    incl = (ri >= ci, ri <= ci)
    strict = (ri > ci, ri < ci)

    acts = []
    for c in chunks:
        k = _l2norm(_conv_act(pk_ref, wk_ref, c, n_chunks))
        v = _conv_act(pv_ref, wv_ref, c, n_chunks)
        q = None
        if pq_ref is not None:
            q = _l2norm(_conv_act(pq_ref, wq_ref, c, n_chunks)) * (HEAD_DIM ** -0.5)
        acts.append((k, v, q))
    prods = []
    for k, v, q in acts:
        kb16 = k.astype(BF16)
        if q is not None:
            aa = _dot_nt(jnp.concatenate([kb16, q.astype(BF16)], axis=0), kb16)
            prods.append((aa[:CHUNK], aa[CHUNK:]))
        else:
            prods.append((_dot_nt(kb16, kb16), None))

    chains = []
    for c, (k, v, q), (a_kk, a_qk) in zip(chunks, acts, prods):
        r0 = _chunk_start(c)
        gates = gcol_ref[pl.ds(r0, CHUNK), :]
        for d in range(2):
            beta = jnp.sum(jnp.where(lane == d * N_HEADS + h, gates, 0.0), axis=1, keepdims=True)
            gc = jnp.sum(jnp.where(lane == (2 + d) * N_HEADS + h, gates, 0.0), axis=1, keepdims=True)
            all_heads = grow_ref[(2 + d) * N_HEADS:(3 + d) * N_HEADS, pl.ds(r0, CHUNK)]
            gc_row = jnp.sum(jnp.where(head_row == h, all_heads, 0.0), axis=0, keepdims=True)
            gamma = jnp.where(incl[d], jnp.exp(jnp.where(incl[d], gc - gc_row, 0.0)), 0.0)
            neg_m = jnp.where(strict[d], -(a_kk * beta * gamma), 0.0)
            chains.append((k, v, q, a_qk, d, beta, gc, gamma, neg_m))

    t_invs = _tri_inverse_many([ch[-1] for ch in chains], ri, ci)
    rhss = [jnp.concatenate([k * (beta * jnp.exp(gc)), v * beta], axis=1).astype(BF16)
            for k, v, _, _, _, beta, gc, _, _ in chains]
    wus = [_dot(t.astype(BF16), rhs) for t, rhs in zip(t_invs, rhss)]

    out = []
    for (k, v, q, a_qk, d, beta, gc, gamma, _), wu in zip(chains, wus):
        w, u = wu[:, :HEAD_DIM], wu[:, HEAD_DIM:]
        g_end = gc[CHUNK - 1:CHUNK, :] if d == 0 else gc[0:1, :]
        kd = k * jnp.exp(g_end - gc)
        decay = jnp.broadcast_to(jnp.exp(g_end), (8, LANES))
        if q is not None:
            qg = q * jnp.exp(gc)
            qk = jnp.where(incl[d], a_qk * gamma, 0.0)
        else:
            qg = qk = None
        out.append((w, u, kd, decay, qg, qk))
    return [out[2 * i:2 * i + 2] for i in range(len(chunks))]


def _delta_kernel(pk_ref, pv_ref, pq_ref, pz_ref, gcol_ref, grow_ref, ck_ref, cv_ref, cgcol_ref, cgrow_ref,
                  wk_ref, wv_ref, wq_ref, onorm_ref, og_ref, wq_sc, kq_sc, u_sc, ge_sc, o_sc,
                  *, n_chunks, n_ctx_chunks, chunks_per_prep):
    h = pl.program_id(1)

    ctx = _prep_chunks(list(range(n_ctx_chunks)), n_ctx_chunks, h, ck_ref, cv_ref, None, cgcol_ref, cgrow_ref,
                       wk_ref, wv_ref, None)
    states = [jnp.zeros((HEAD_DIM, HEAD_DIM), F32)] * 2
    for c in range(n_ctx_chunks):
        steps = [ctx[c][0], ctx[n_ctx_chunks - 1 - c][1]]
        ws = [_dot(st[0].astype(BF16), s.astype(BF16)) for st, s in zip(steps, states)]
        v_new = [st[1] - x for st, x in zip(steps, ws)]
        kv = [_dot(st[2].T.astype(BF16), vn.astype(BF16)) for st, vn in zip(steps, v_new)]
        states = [s * st[3][0:1, :] + x for s, st, x in zip(states, steps, kv)]

    def prep_body(i, carry):
        chunks = [i * chunks_per_prep + g for g in range(chunks_per_prep)]
        prepped = _prep_chunks(chunks, n_chunks, h, pk_ref, pv_ref, pq_ref, gcol_ref, grow_ref,
                               wk_ref, wv_ref, wq_ref)
        for c, both in zip(chunks, prepped):
            r0 = _chunk_start(c)
            for d in range(2):
                w, u, kd, decay, qg, qk = both[d]
                wq_sc[d, c] = jnp.concatenate([w, qg], axis=0).astype(BF16)
                kq_sc[d, c] = jnp.concatenate([kd.T, qk], axis=0).astype(BF16)
                u_sc[d, pl.ds(r0, CHUNK), :] = u
                ge_sc[d, c] = decay
        return carry

    lax.fori_loop(0, n_chunks // chunks_per_prep, prep_body, 0)

    def seq_steps(states, chunks):
        ws = [_dot(wq_sc[d, c], s.astype(BF16)) for d, (c, s) in enumerate(zip(chunks, states))]
        v_new = [u_sc[d, pl.ds(_chunk_start(c), CHUNK), :] - x[:CHUNK] for d, (c, x) in enumerate(zip(chunks, ws))]
        kv = [_dot(kq_sc[d, c], vn.astype(BF16)) for d, (c, vn) in enumerate(zip(chunks, v_new))]
        nxt = [s * ge_sc[d, c][0:1, :] + x[:CHUNK] for d, (c, s, x) in enumerate(zip(chunks, states, kv))]
        return tuple(nxt), [x[CHUNK:] + y[CHUNK:] for x, y in zip(ws, kv)]

    def finish(c, o):
        r0 = _chunk_start(c)
        y = o * lax.rsqrt(jnp.mean(o * o, axis=-1, keepdims=True) + NORM_EPS) * onorm_ref[...]
        og_ref[pl.ds(r0, CHUNK), :] = (y * _silu(pz_ref[pl.ds(r0, CHUNK), :])).astype(og_ref.dtype)

    def first_half(i, carry):
        chunks = (i, n_chunks - 1 - i)
        carry, outs = seq_steps(carry, chunks)
        for c, o in zip(chunks, outs):
            o_sc[pl.ds(_chunk_start(c), CHUNK), :] = o
        return carry

    def second_half(i, carry):
        chunks = (i, n_chunks - 1 - i)
        carry, outs = seq_steps(carry, chunks)
        for c, o in zip(chunks, outs):
            finish(c, o_sc[pl.ds(_chunk_start(c), CHUNK), :] + o)
        return carry

    carry = lax.fori_loop(0, n_chunks // 2, first_half, tuple(states))
    lax.fori_loop(n_chunks // 2, n_chunks, second_half, carry)


def _delta_call(p3, gcol, grow, pc3, cgcol, cgrow, conv_w, onorm_g):
    bsz, length, _ = p3.shape
    ctx_len = pc3.shape[1]
    n_chunks = length // CHUNK
    n_ctx_chunks = ctx_len // CHUNK
    chunks_per_prep = 8
    assert n_chunks % 2 == 0 and n_chunks % chunks_per_prep == 0

    def col_spec(rows, first_block):
        return pl.BlockSpec((None, rows, LANES), lambda b, h: (b, 0, first_block + h))

    return pl.pallas_call(
        functools.partial(_delta_kernel, n_chunks=n_chunks, n_ctx_chunks=n_ctx_chunks,
                          chunks_per_prep=chunks_per_prep),
        grid=(bsz, N_HEADS),
        in_specs=[col_spec(length, 0), col_spec(length, N_HEADS), col_spec(length, 2 * N_HEADS),
                  col_spec(length, 3 * N_HEADS),
                  pl.BlockSpec((None, length, LANES), lambda b, h: (b, 0, 0)),
                  pl.BlockSpec((None, LANES, length), lambda b, h: (b, 0, 0)),
                  col_spec(ctx_len, 0), col_spec(ctx_len, N_HEADS),
                  pl.BlockSpec((None, ctx_len, LANES), lambda b, h: (b, 0, 0)),
                  pl.BlockSpec((None, LANES, ctx_len), lambda b, h: (b, 0, 0)),
                  pl.BlockSpec((3, LANES), lambda b, h: (0, h)),
                  pl.BlockSpec((3, LANES), lambda b, h: (0, N_HEADS + h)),
                  pl.BlockSpec((3, LANES), lambda b, h: (0, 2 * N_HEADS + h)),
                  pl.BlockSpec((1, LANES), lambda b, h: (0, 0))],
        out_specs=pl.BlockSpec((None, length, LANES), lambda b, h: (b, 0, h)),
        out_shape=jax.ShapeDtypeStruct((bsz, length, B_WIDTH), BF16),
        scratch_shapes=[pltpu.VMEM((2, n_chunks, 2 * CHUNK, LANES), BF16),
                        pltpu.VMEM((2, n_chunks, 2 * CHUNK, LANES), BF16),
                        pltpu.VMEM((2, length, LANES), F32),
                        pltpu.VMEM((2, n_chunks, 8, LANES), F32),
                        pltpu.VMEM((length, LANES), F32)],
        compiler_params=pltpu.CompilerParams(
            dimension_semantics=("parallel", "parallel"), vmem_limit_bytes=VMEM_LIMIT),
        name="delta",
    )(p3, p3, p3, p3, gcol, grow, pc3, pc3, cgcol, cgrow, conv_w, conv_w, conv_w, onorm_g)


def _mix_kernel(x_ref, og_ref, mod_ref, g1_ref, wuv_ref, wg_ref, lng_ref, lnb_ref, ws_ref, bs_ref,
                wa_ref, wb_ref, wo_ref, o_ref, ua_ref, *, n_sub):
    x = x_ref[...]
    h = _norm_mod(x, g1_ref[...], mod_ref[:, 0:D_MODEL], mod_ref[:, D_MODEL:2 * D_MODEL]).astype(BF16)
    uv = _dot(h, wuv_ref[...])
    v = jax.nn.gelu(uv[:, D_MODEL:])
    vc = v - jnp.mean(v, axis=-1, keepdims=True)
    vn = vc * lax.rsqrt(jnp.mean(vc * vc, axis=-1, keepdims=True) + NORM_EPS) * lng_ref[...] + lnb_ref[...]
    vb = vn.astype(BF16)
    for n in range(n_sub):
        rows = slice(n * A_CHUNK, (n + 1) * A_CHUNK)
        for g in range(A_GROUPS):
            cols = slice(g * LANES, (g + 1) * LANES)
            s = _dot(ws_ref[g], vb[rows, cols]) + bs_ref[:, cols]
            ua_ref[rows, cols] = (jax.nn.gelu(uv[rows, cols]) * s).astype(BF16)
    y_a = _dot(ua_ref[...], wa_ref[...])
    y_b = _dot(og_ref[...], wb_ref[...])
    gates = jax.nn.sigmoid(_dot(h, wg_ref[...]))
    t = gates[:, :D_MODEL] * y_a + gates[:, D_MODEL:] * y_b
    y = _dot(t.astype(BF16), wo_ref[...])
    o_ref[...] = x + mod_ref[:, 2 * D_MODEL:3 * D_MODEL] * y


def _resident(shape):
    return pl.BlockSpec(shape, lambda i: (0,) * len(shape), pipeline_mode=pl.Buffered(1))


def _mix_call(x2d, og, mod3, g1, wuv16, wg16, ln_g, ln_b, ws16, bs_cols, wa16, wb16, wo16, tm, tiles_per_batch):
    m = x2d.shape[0]
    rows = pl.BlockSpec((tm, D_MODEL), lambda i: (i, 0))
    return pl.pallas_call(
        functools.partial(_mix_kernel, n_sub=tm // A_CHUNK),
        grid=(m // tm,),
        in_specs=[rows, rows,
                  pl.BlockSpec((None, 1, 6 * D_MODEL), lambda i: (i // tiles_per_batch, 0, 0)),
                  _resident((1, D_MODEL)),
                  _resident((D_MODEL, 2 * D_MODEL)), _resident((D_MODEL, 2 * D_MODEL)),
                  _resident((1, D_MODEL)), _resident((1, D_MODEL)),
                  _resident((A_GROUPS, A_CHUNK, A_CHUNK)), _resident((A_CHUNK, D_MODEL)),
                  _resident((D_MODEL, D_MODEL)), _resident((D_MODEL, D_MODEL)), _resident((D_MODEL, D_MODEL))],
        out_specs=rows,
        out_shape=jax.ShapeDtypeStruct((m, D_MODEL), F32),
        scratch_shapes=[pltpu.VMEM((tm, D_MODEL), BF16)],
        compiler_params=pltpu.CompilerParams(
            dimension_semantics=("parallel",), vmem_limit_bytes=VMEM_LIMIT),
        name="mix",
    )(x2d, og, mod3, g1, wuv16, wg16, ln_g, ln_b, ws16, bs_cols, wa16, wb16, wo16)


def _ffn_kernel(x_ref, mod_ref, g2_ref, wa_ref, wb_ref, cw_ref, cb_ref, wd_ref, gf_ref, o_ref, h_ref, acc_ref):
    j = pl.program_id(1)
    tm = x_ref.shape[0]

    @pl.when(j == 0)
    def _():
        h = _norm_mod(x_ref[...], g2_ref[...], mod_ref[:, 3 * D_MODEL:4 * D_MODEL],
                      mod_ref[:, 4 * D_MODEL:5 * D_MODEL])
        h_ref[...] = h.astype(BF16)
        acc_ref[...] = jnp.zeros_like(acc_ref)

    h = h_ref[...]
    a = _dot(h, wa_ref[...])
    b = _dot(h, wb_ref[...])
    col_in_row = lax.broadcasted_iota(jnp.int32, a.shape, 0) & (GRID_W - 1)
    a_prev = jnp.where(col_in_row == 0, 0.0, pltpu.roll(a, 1, 0))
    a_next = jnp.where(col_in_row == GRID_W - 1, 0.0, pltpu.roll(a, tm - 1, 0))
    ac = a_prev * cw_ref[0:1, :] + a * cw_ref[1:2, :] + a_next * cw_ref[2:3, :] + cb_ref[...]
    acc_ref[...] += _dot((jax.nn.gelu(ac) * b).astype(BF16), wd_ref[...])

    @pl.when(j == pl.num_programs(1) - 1)
    def _():
        x2 = x_ref[...] + mod_ref[:, 5 * D_MODEL:6 * D_MODEL] * acc_ref[...]
        ms = jnp.mean(x2 * x2, axis=-1, keepdims=True)
        o_ref[...] = x2 * lax.rsqrt(ms + NORM_EPS) * gf_ref[...]


def _ffn_call(x1, mod3, norm2_g, w_up16, conv_w, conv_b, w_down16, final_g, tm, tf, tiles_per_batch):
    m = x1.shape[0]
    nf = D_FF // tf
    return pl.pallas_call(
        _ffn_kernel,
        grid=(m // tm, nf),
        in_specs=[pl.BlockSpec((tm, D_MODEL), lambda i, j: (i, 0)),
                  pl.BlockSpec((None, 1, 6 * D_MODEL), lambda i, j: (i // tiles_per_batch, 0, 0)),
                  pl.BlockSpec((1, D_MODEL), lambda i, j: (0, 0)),
                  pl.BlockSpec((D_MODEL, tf), lambda i, j: (0, j)),
                  pl.BlockSpec((D_MODEL, tf), lambda i, j: (0, nf + j)),
                  pl.BlockSpec((3, tf), lambda i, j: (0, j)),
                  pl.BlockSpec((1, tf), lambda i, j: (0, j)),
                  pl.BlockSpec((tf, D_MODEL), lambda i, j: (j, 0)),
                  pl.BlockSpec((1, D_MODEL), lambda i, j: (0, 0))],
        out_specs=pl.BlockSpec((tm, D_MODEL), lambda i, j: (i, 0)),
        out_shape=jax.ShapeDtypeStruct((m, D_MODEL), F32),
        scratch_shapes=[pltpu.VMEM((tm, D_MODEL), BF16), pltpu.VMEM((tm, D_MODEL), F32)],
        compiler_params=pltpu.CompilerParams(
            dimension_semantics=("parallel", "arbitrary"), vmem_limit_bytes=VMEM_LIMIT),
        name="ffn",
    )(x1, mod3, norm2_g, w_up16, w_up16, conv_w, conv_b, w_down16, final_g)


def kernel(x, c, ctx, c_ctx, w_mod, b_mod, norm1_g, w_in, conv_qkv, a_log, dt_bias, onorm_g, w_proj_b,
           a_ln_g, a_ln_b, a_ws, a_bs, w_proj_a, w_out, norm2_g, w_up, ffn_conv_w, ffn_conv_b, w_down,
           final_g):
    bsz, length, _ = x.shape
    ctx_len = ctx.shape[1]
    assert w_mod.shape[0] == 1 and bsz <= 7
    assert length % 1024 == 0 and ctx_len % CHUNK == 0

    w = w_in[0]
    ba_cols = jnp.pad(w[:, OFF_BA:OFF_Q], ((0, 0), (0, LANES - 4 * N_HEADS)))
    w_lat = jnp.concatenate([w[:, :OFF_BA], w[:, OFF_Q:OFF_U], ba_cols], axis=1).astype(BF16)
    w_ctx = jnp.concatenate([w[:, :OFF_BA], ba_cols], axis=1).astype(BF16)
    w_uv = w[:, OFF_U:OFF_GA].astype(BF16)
    w_gates = w[:, OFF_GA:].astype(BF16)
    lane_pad = (2 * N_HEADS, LANES - 4 * N_HEADS)
    a_lane = jnp.pad(jnp.exp(a_log[0].astype(F32)).reshape(-1), lane_pad).reshape(1, LANES)
    dt_lane = jnp.pad(dt_bias[0].astype(F32).reshape(-1), lane_pad).reshape(1, LANES)
    bs_cols = jnp.repeat(a_bs[0].T, LANES, axis=1)

    cond8 = jnp.concatenate([c, c_ctx[None, :], jnp.zeros((7 - bsz, D_MODEL), F32)], axis=0)
    mod3 = _mod_call(cond8, w_mod[0], b_mod[0]).reshape(8, 1, 6 * D_MODEL)

    x2d = x.reshape(bsz * length, D_MODEL)
    tm = 512
    tiles_per_batch = length // tm
    g1 = norm1_g[0].reshape(1, D_MODEL)
    p2d = _inproj_call(x2d, mod3, lambda i: i // tiles_per_batch, g1, w_lat, tm, P_COLS // 3, "in_proj")
    pc2d = _inproj_call(ctx.reshape(bsz * ctx_len, D_MODEL), mod3, lambda i: bsz, g1, w_ctx,
                        ctx_len, w_ctx.shape[1], "in_proj_ctx")
    p3 = p2d.reshape(bsz, length, P_COLS)
    pc3 = pc2d.reshape(bsz, ctx_len, w_ctx.shape[1])

    gcol, grow = _gates_call(p3, COLBLK_BA, a_lane, dt_lane, "gates")
    cgcol, cgrow = _gates_call(pc3, 2 * N_HEADS, a_lane, dt_lane, "gates_ctx")
    og = _delta_call(p3, gcol, grow, pc3, cgcol, cgrow, conv_qkv[0], onorm_g[0].reshape(1, HEAD_DIM))

    x1 = _mix_call(x2d, og.reshape(bsz * length, B_WIDTH), mod3, g1, w_uv, w_gates,
                   a_ln_g[0].reshape(1, -1), a_ln_b[0].reshape(1, -1), a_ws[0].astype(BF16), bs_cols,
                   w_proj_a[0].astype(BF16), w_proj_b[0].astype(BF16), w_out[0].astype(BF16), tm, tiles_per_batch)
    out = _ffn_call(x1, mod3, norm2_g[0].reshape(1, -1), w_up[0].astype(BF16), ffn_conv_w[0],
                    ffn_conv_b[0].reshape(1, -1), w_down[0].astype(BF16), final_g.reshape(1, -1),
                    tm, D_FF // 2, tiles_per_batch)
    return out.reshape(bsz, length, D_MODEL)
```

```python
import functools

import jax
import jax.numpy as jnp
from jax import lax
from jax.experimental import pallas as pl
from jax.experimental.pallas import tpu as pltpu

F32 = jnp.float32
BF16 = jnp.bfloat16

D_MODEL = 1024
GRID_W = 64
NORM_EPS = 1e-6
N_HEADS = 8
HEAD_DIM = 128
B_WIDTH = N_HEADS * HEAD_DIM
A_GROUPS = 8
A_CHUNK = 128
D_FF = 2816
OFF_BA = 2 * B_WIDTH
OFF_Q = OFF_BA + 4 * N_HEADS
OFF_U = OFF_Q + 2 * B_WIDTH
OFF_GA = OFF_U + 2 * D_MODEL
LANES = 128
MXU_COLS = 256
CHUNK = 128
GROUP = 8

VMEM_LIMIT = 56 * 1024 * 1024


def _silu(x):
    return x * jax.nn.sigmoid(x)


def _dot(a, b):
    return jnp.dot(a, b, preferred_element_type=F32)


def _dot_nt(a, b):
    return lax.dot_general(a, b, (((1,), (1,)), ((), ())), preferred_element_type=F32)


def _mod_kernel(c_ref, w_ref, b_ref, o_ref):
    cond = _silu(c_ref[...])
    o_ref[...] = jnp.dot(cond, w_ref[...], preferred_element_type=F32,
                         precision=lax.Precision.HIGHEST) + b_ref[...]


def _mod_call(cond8, w_mod, b_mod):
    n = w_mod.shape[1]
    tn = 1536
    return pl.pallas_call(
        _mod_kernel,
        grid=(n // tn,),
        in_specs=[pl.BlockSpec((8, D_MODEL), lambda j: (0, 0)),
                  pl.BlockSpec((D_MODEL, tn), lambda j: (0, j)),
                  pl.BlockSpec((1, tn), lambda j: (0, j))],
        out_specs=pl.BlockSpec((8, tn), lambda j: (0, j)),
        out_shape=jax.ShapeDtypeStruct((8, n), F32),
        name="mod",
    )(cond8, w_mod, b_mod.reshape(1, n))


def _norm_mod(x, g, shift, scale):
    ms = jnp.mean(x * x, axis=-1, keepdims=True)
    y = x * lax.rsqrt(ms + NORM_EPS) * g
    return y * (1.0 + scale) + shift


def _l2norm(t):
    return t * lax.rsqrt(jnp.sum(t * t, axis=-1, keepdims=True) + NORM_EPS)


def _inproj_kernel(x_ref, xb_ref, xa_ref, mod_ref, g_ref, w_ref, wba_ref, cw_ref, o_ref, ba_ref, h_ref, halo_ref,
                   *, tiles_per_seq, kinds):
    i, j = pl.program_id(0), pl.program_id(1)
    tm, tn = o_ref.shape
    shift, scale = mod_ref[:, 0:D_MODEL], mod_ref[:, D_MODEL:2 * D_MODEL]

    @pl.when(j == 0)
    def _():
        h_ref[...] = _norm_mod(x_ref[...], g_ref[...], shift, scale).astype(BF16)
        halo = jnp.concatenate([xb_ref[...], xa_ref[...]], axis=0)
        halo_ref[...] = _norm_mod(halo, g_ref[...], shift, scale).astype(BF16)
        ba_ref[...] = _dot(h_ref[...], wba_ref[...])

    tile_in_seq = i % tiles_per_seq
    slab = MXU_COLS
    row = lax.broadcasted_iota(jnp.int32, (tm, slab), 0)
    tiled = (tm // 8, 8, slab)

    def epilogue(kind):
        for s in range(tn // slab):
            cols = slice(s * slab, (s + 1) * slab)
            p = _dot(h_ref[...], w_ref[:, cols])
            if kind != 'z':
                edge = _dot(halo_ref[...], w_ref[:, cols])
                prev_row = jnp.where(tile_in_seq > 0, edge[7:8, :], 0.0)
                next_row = jnp.where(tile_in_seq < tiles_per_seq - 1, edge[8:9, :], 0.0)
                xp = jnp.where(row == 0, prev_row, pltpu.roll(p, 1, 0))
                xn = jnp.where(row == tm - 1, next_row, pltpu.roll(p, tm - 1, 0))
                taps = [jnp.broadcast_to(cw_ref[r:r + 1, cols], (8, slab)) for r in range(3)]
                p = (xp.reshape(tiled) * taps[0] + p.reshape(tiled) * taps[1]
                     + xn.reshape(tiled) * taps[2]).reshape(tm, slab)
            y = _silu(p)
            for hd in range(slab // HEAD_DIM):
                head = y[:, hd * HEAD_DIM:(hd + 1) * HEAD_DIM]
                if kind in ('k', 'q'):
                    head = _l2norm(head)
                if kind == 'q':
                    head = head * (HEAD_DIM ** -0.5)
                o_ref[:, s * slab + hd * HEAD_DIM:s * slab + (hd + 1) * HEAD_DIM] = head.astype(o_ref.dtype)

    for jj, kind in enumerate(kinds):
        pl.when(j == jj)(functools.partial(epilogue, kind))


def _inproj_call(x2d, mod3, mod_row_of_tile, norm_g, w, w_ba, conv_w, kinds, tm, tiles_per_seq, name):
    m = x2d.shape[0]
    n = w.shape[1]
    tn = B_WIDTH
    assert n == tn * len(kinds) and tm % 8 == 0
    conv_groups = conv_w.shape[1] // tn
    last_halo = m // 8 - 1
    return pl.pallas_call(
        functools.partial(_inproj_kernel, tiles_per_seq=tiles_per_seq, kinds=kinds),
        grid=(m // tm, n // tn),
        in_specs=[pl.BlockSpec((tm, D_MODEL), lambda i, j: (i, 0)),
                  pl.BlockSpec((8, D_MODEL), lambda i, j: (jnp.maximum(i * (tm // 8) - 1, 0), 0)),
                  pl.BlockSpec((8, D_MODEL), lambda i, j: (jnp.minimum((i + 1) * (tm // 8), last_halo), 0)),
                  pl.BlockSpec((None, 1, 6 * D_MODEL), lambda i, j: (mod_row_of_tile(i), 0, 0)),
                  pl.BlockSpec((1, D_MODEL), lambda i, j: (0, 0)),
                  pl.BlockSpec((D_MODEL, tn), lambda i, j: (0, j)),
                  pl.BlockSpec((D_MODEL, LANES), lambda i, j: (0, 0)),
                  pl.BlockSpec((3, tn), lambda i, j: (0, jnp.minimum(j, conv_groups - 1)))],
        out_specs=[pl.BlockSpec((tm, tn), lambda i, j: (i, j)),
                   pl.BlockSpec((tm, LANES), lambda i, j: (i, 0))],
        out_shape=[jax.ShapeDtypeStruct((m, n), BF16), jax.ShapeDtypeStruct((m, LANES), F32)],
        scratch_shapes=[pltpu.VMEM((tm, D_MODEL), BF16), pltpu.VMEM((16, D_MODEL), BF16)],
        compiler_params=pltpu.CompilerParams(
            dimension_semantics=("parallel", "arbitrary"), vmem_limit_bytes=VMEM_LIMIT),
        name=name,
    )(x2d, x2d, x2d, mod3, norm_g, w, w_ba, conv_w)


def _gates_kernel(ba_ref, a_ref, dt_ref, col_ref, row_ref, *, n_sub):
    lane = lax.broadcasted_iota(jnp.int32, (CHUNK, LANES), 1)
    ri = lax.broadcasted_iota(jnp.int32, (CHUNK, CHUNK), 0)
    ci = lax.broadcasted_iota(jnp.int32, (CHUNK, CHUNK), 1)
    tri_lo = (ri >= ci).astype(F32)
    tri_up = (ri <= ci).astype(F32)
    for s in range(n_sub):
        x = ba_ref[s * CHUNK:(s + 1) * CHUNK, :]
        beta = jax.nn.sigmoid(x)
        y = x + dt_ref[...]
        softplus = jnp.maximum(y, 0.0) + jnp.log1p(jnp.exp(-jnp.abs(y)))
        g = jnp.where((lane >= 2 * N_HEADS) & (lane < 4 * N_HEADS), -a_ref[...] * softplus, 0.0)
        pre = jnp.dot(tri_lo, g, preferred_element_type=F32, precision=lax.Precision.HIGHEST)
        suf = jnp.dot(tri_up, g, preferred_element_type=F32, precision=lax.Precision.HIGHEST)
        col = jnp.where(lane < 2 * N_HEADS, beta, jnp.where(lane < 3 * N_HEADS, pre, suf))
        col_ref[s * CHUNK:(s + 1) * CHUNK, :] = col
        row_ref[:, s * CHUNK:(s + 1) * CHUNK] = col.T


def _gates_call(ba3, a_lane, dt_lane, name):
    bsz, length, _ = ba3.shape
    tl = min(length, 1024)
    return pl.pallas_call(
        functools.partial(_gates_kernel, n_sub=tl // CHUNK),
        grid=(bsz, length // tl),
        in_specs=[pl.BlockSpec((None, tl, LANES), lambda b, i: (b, i, 0)),
                  pl.BlockSpec((1, LANES), lambda b, i: (0, 0)),
                  pl.BlockSpec((1, LANES), lambda b, i: (0, 0))],
        out_specs=[pl.BlockSpec((None, tl, LANES), lambda b, i: (b, i, 0)),
                   pl.BlockSpec((None, LANES, tl), lambda b, i: (b, 0, i))],
        out_shape=[jax.ShapeDtypeStruct((bsz, length, LANES), F32),
                   jax.ShapeDtypeStruct((bsz, LANES, length), F32)],
        name=name,
    )(ba3, a_lane, dt_lane)


def _run_interleaved(*gens):
    results = [None] * len(gens)
    live = list(range(len(gens)))
    while live:
        for idx in list(live):
            try:
                next(gens[idx])
            except StopIteration as stop:
                results[idx] = stop.value
                live.remove(idx)
    return results


def _tri_inverse_stages(n_mats, ri, ci):
    def same_block(size):
        shift = size.bit_length() - 1
        return (ri >> shift) == (ci >> shift)

    eye = (ri == ci).astype(F32)
    blk16 = same_block(16)
    n16 = [jnp.where(blk16, n, 0.0) for n in n_mats]
    ts = [eye + n for n in n16]
    pbs = [n.astype(BF16) for n in n16]
    pbs = [_dot(p, p).astype(BF16) for p in pbs]
    yield
    for _ in range(2):
        ts = [t + _dot(t.astype(BF16), p) for t, p in zip(ts, pbs)]
        pbs = [_dot(p, p).astype(BF16) for p in pbs]
        yield
    ts = [t + _dot(t.astype(BF16), p) for t, p in zip(ts, pbs)]
    yield
    for size in (32, 64, 128):
        off_mask = same_block(size) & jnp.logical_not(same_block(size // 2))
        tbs = [t.astype(BF16) for t in ts]
        xs = [_dot(jnp.where(off_mask, n, 0.0).astype(BF16), tb).astype(BF16) for n, tb in zip(n_mats, tbs)]
        yield
        ts = [t + _dot(tb, x) for t, tb, x in zip(ts, tbs, xs)]
        yield
    return ts


def _chunk_start(c):
    return c * CHUNK if isinstance(c, int) else pl.multiple_of(c * CHUNK, CHUNK)


def _prep_stages(chunks, h, k_ref, v_ref, q_ref, gcol_ref, grow_ref):
    lane = lax.broadcasted_iota(jnp.int32, (CHUNK, LANES), 1)
    ri = lax.broadcasted_iota(jnp.int32, (CHUNK, CHUNK), 0)
    ci = lax.broadcasted_iota(jnp.int32, (CHUNK, CHUNK), 1)
    head_row = lax.broadcasted_iota(jnp.int32, (N_HEADS, CHUNK), 0)
    incl = (ri >= ci, ri <= ci)
    strict = (ri > ci, ri < ci)

    acts, prods = [], []
    for c in chunks:
        rows = pl.ds(_chunk_start(c), CHUNK)
        kb16 = k_ref[rows, :]
        k = kb16.astype(F32)
        kt16 = k.T.astype(BF16)
        v = v_ref[rows, :].astype(F32)
        if q_ref is not None:
            qb16 = q_ref[rows, :]
            aa = _dot(jnp.concatenate([kb16, qb16], axis=0), kt16)
            acts.append((k, v, qb16.astype(F32)))
            prods.append((aa[:CHUNK], aa[CHUNK:]))
        else:
            acts.append((k, v, None))
            prods.append((_dot(kb16, kt16), None))
    yield

    chains = []
    for c, (k, v, q), (a_kk, a_qk) in zip(chunks, acts, prods):
        r0 = _chunk_start(c)
        gates = gcol_ref[pl.ds(r0, CHUNK), :]
        for d in range(2):
            beta = jnp.sum(jnp.where(lane == d * N_HEADS + h, gates, 0.0), axis=1, keepdims=True)
            gc = jnp.sum(jnp.where(lane == (2 + d) * N_HEADS + h, gates, 0.0), axis=1, keepdims=True)
            all_heads = grow_ref[(2 + d) * N_HEADS:(3 + d) * N_HEADS, pl.ds(r0, CHUNK)]
            gc_row = jnp.sum(jnp.where(head_row == h, all_heads, 0.0), axis=0, keepdims=True)
            gamma = jnp.where(incl[d], jnp.exp(jnp.where(incl[d], gc - gc_row, 0.0)), 0.0)
            neg_m = jnp.where(strict[d], -(a_kk * beta * gamma), 0.0)
            chains.append((k, v, q, a_qk, d, beta, gc, gamma, neg_m))

    t_invs = yield from _tri_inverse_stages([ch[-1] for ch in chains], ri, ci)
    rhss = [jnp.concatenate([k * (beta * jnp.exp(gc)), v * beta], axis=1).astype(BF16)
            for k, v, _, _, _, beta, gc, _, _ in chains]
    wus = [_dot(t.astype(BF16), rhs) for t, rhs in zip(t_invs, rhss)]
    yield

    out = []
    for (k, v, q, a_qk, d, beta, gc, gamma, _), wu in zip(chains, wus):
        w, u = wu[:, :HEAD_DIM], wu[:, HEAD_DIM:]
        g_end = gc[CHUNK - 1:CHUNK, :] if d == 0 else gc[0:1, :]
        kd = k * jnp.exp(g_end - gc)
        decay = jnp.broadcast_to(jnp.exp(g_end), (8, LANES))
        if q is not None:
            qg = q * jnp.exp(gc)
            qk = jnp.where(incl[d], a_qk * gamma, 0.0)
        else:
            qg = qk = None
        out.append((w, u, kd, decay, qg, qk))
    return [out[2 * i:2 * i + 2] for i in range(len(chunks))]


def _delta_kernel(pk_ref, pv_ref, pq_ref, gcol_ref, grow_ref, ck_ref, cv_ref, cgcol_ref, cgrow_ref,
                  sz_ref, onorm_ref, og_ref,
                  wq_sc, kq_sc, u_sc, ge_sc, s0_sc, o_sc, *, n_chunks, n_ctx_chunks, n_heads_total):
    t = pl.program_id(0)
    h = jnp.minimum(t, n_heads_total - 1) % N_HEADS
    slot_p = t % 2
    slot_s = 1 - slot_p

    @pl.when(t == 0)
    def _():
        wq_sc[1] = jnp.zeros(wq_sc.shape[1:], wq_sc.dtype)
        kq_sc[1] = jnp.zeros(kq_sc.shape[1:], kq_sc.dtype)
        u_sc[1] = jnp.zeros(u_sc.shape[1:], u_sc.dtype)
        ge_sc[1] = jnp.zeros(ge_sc.shape[1:], ge_sc.dtype)
        s0_sc[1] = jnp.zeros(s0_sc.shape[1:], s0_sc.dtype)

    ctx, = _run_interleaved(_prep_stages(list(range(n_ctx_chunks)), h, ck_ref, cv_ref, None,
                                         cgcol_ref, cgrow_ref))
    states = [jnp.zeros((HEAD_DIM, HEAD_DIM), F32)] * 2
    for c in range(n_ctx_chunks):
        steps = [ctx[c][0], ctx[n_ctx_chunks - 1 - c][1]]
        ws = [_dot(st[0].astype(BF16), s.astype(BF16)) for st, s in zip(steps, states)]
        v_new = [st[1] - x for st, x in zip(steps, ws)]
        kv = [_dot(st[2].T.astype(BF16), vn.astype(BF16)) for st, vn in zip(steps, v_new)]
        states = [s * st[3][0:1, :] + x for s, st, x in zip(states, steps, kv)]
    for d in range(2):
        s0_sc[slot_p, d] = states[d]

    def prepare(i):
        chunks = [i * GROUP + g for g in range(GROUP)]
        prepped = yield from _prep_stages(chunks, h, pk_ref, pv_ref, pq_ref, gcol_ref, grow_ref)
        for c, both in zip(chunks, prepped):
            r0 = _chunk_start(c)
            for d in range(2):
                w, u, kd, decay, qg, qk = both[d]
                wq_sc[slot_p, d, c] = jnp.concatenate([w, qg], axis=0).astype(BF16)
                kq_sc[slot_p, d, c] = jnp.concatenate([kd.T, qk], axis=0).astype(BF16)
                u_sc[slot_p, d, pl.ds(r0, CHUNK), :] = u
                ge_sc[slot_p, d, c] = decay

    def finish(c, o):
        r0 = _chunk_start(c)
        y = o * lax.rsqrt(jnp.mean(o * o, axis=-1, keepdims=True) + NORM_EPS) * onorm_ref[...]
        og_ref[pl.ds(r0, CHUNK), :] = (y * sz_ref[pl.ds(r0, CHUNK), :].astype(F32)).astype(og_ref.dtype)

    def scan(i, states, second_half):
        for g in range(GROUP):
            j = i * GROUP + g
            chunks = (j, n_chunks - 1 - j)
            ws = [_dot(wq_sc[slot_s, d, c], s.astype(BF16)) for d, (c, s) in enumerate(zip(chunks, states))]
            yield
            v_new = [u_sc[slot_s, d, pl.ds(_chunk_start(c), CHUNK), :] - x[:CHUNK]
                     for d, (c, x) in enumerate(zip(chunks, ws))]
            kv = [_dot(kq_sc[slot_s, d, c], vn.astype(BF16)) for d, (c, vn) in enumerate(zip(chunks, v_new))]
            yield
            states = [s * ge_sc[slot_s, d, c][0:1, :] + x[:CHUNK]
                      for d, (c, s, x) in enumerate(zip(chunks, states, kv))]
            for c, x, y in zip(chunks, ws, kv):
                o = x[CHUNK:] + y[CHUNK:]
                rows = pl.ds(_chunk_start(c), CHUNK)
                if second_half:
                    finish(c, o_sc[rows, :] + o)
                else:
                    o_sc[rows, :] = o
        return tuple(states)

    def body(second_half, i, states):
        _, states = _run_interleaved(prepare(i), scan(i, states, second_half))
        return states

    n_iter = n_chunks // GROUP
    states = (s0_sc[slot_s, 0], s0_sc[slot_s, 1])
    states = lax.fori_loop(0, n_iter // 2, functools.partial(body, False), states)
    lax.fori_loop(n_iter // 2, n_iter, functools.partial(body, True), states)


def _delta_call(p3, gcol, grow, pc3, cgcol, cgrow, onorm_g):
    bsz, length, _ = p3.shape
    ctx_len = pc3.shape[1]
    n_chunks = length // CHUNK
    n_ctx_chunks = ctx_len // CHUNK
    n_heads_total = bsz * N_HEADS
    assert n_chunks % (2 * GROUP) == 0

    def prepared(t):
        t = jnp.minimum(t, n_heads_total - 1)
        return t // N_HEADS, t % N_HEADS

    def scanned(t):
        t = jnp.maximum(t - 1, 0)
        return t // N_HEADS, t % N_HEADS

    def head_cols(rows, first_block, which):
        def index(t):
            b, h = which(t)
            return b, 0, first_block + h
        return pl.BlockSpec((None, rows, LANES), index)

    def per_batch(shape):
        return pl.BlockSpec((None,) + shape, lambda t: (prepared(t)[0], 0, 0))

    return pl.pallas_call(
        functools.partial(_delta_kernel, n_chunks=n_chunks, n_ctx_chunks=n_ctx_chunks,
                          n_heads_total=n_heads_total),
        grid=(n_heads_total + 1,),
        in_specs=[head_cols(length, 0, prepared), head_cols(length, N_HEADS, prepared),
                  head_cols(length, 2 * N_HEADS, prepared),
                  per_batch((length, LANES)), per_batch((LANES, length)),
                  head_cols(ctx_len, 0, prepared), head_cols(ctx_len, N_HEADS, prepared),
                  per_batch((ctx_len, LANES)), per_batch((LANES, ctx_len)),
                  head_cols(length, 3 * N_HEADS, scanned),
                  pl.BlockSpec((1, LANES), lambda t: (0, 0))],
        out_specs=head_cols(length, 0, scanned),
        out_shape=jax.ShapeDtypeStruct((bsz, length, B_WIDTH), BF16),
        scratch_shapes=[pltpu.VMEM((2, 2, n_chunks, 2 * CHUNK, LANES), BF16),
                        pltpu.VMEM((2, 2, n_chunks, 2 * CHUNK, LANES), BF16),
                        pltpu.VMEM((2, 2, length, LANES), F32),
                        pltpu.VMEM((2, 2, n_chunks, 8, LANES), F32),
                        pltpu.VMEM((2, 2, HEAD_DIM, HEAD_DIM), F32),
                        pltpu.VMEM((length, LANES), F32)],
        compiler_params=pltpu.CompilerParams(
            dimension_semantics=("arbitrary",), vmem_limit_bytes=VMEM_LIMIT),
        name="delta",
    )(p3, p3, p3, gcol, grow, pc3, pc3, cgcol, cgrow, p3, onorm_g)


def _mix_kernel(x_ref, og_ref, mod_ref, g1_ref, wuv_ref, wg_ref, lng_ref, lnb_ref, ws_ref, bs_ref,
                wa_ref, wb_ref, wo_ref, o_ref, ua_ref, *, n_sub):
    x = x_ref[...]
    h = _norm_mod(x, g1_ref[...], mod_ref[:, 0:D_MODEL], mod_ref[:, D_MODEL:2 * D_MODEL]).astype(BF16)
    uv = _dot(h, wuv_ref[...])
    v = jax.nn.gelu(uv[:, D_MODEL:])
    vc = v - jnp.mean(v, axis=-1, keepdims=True)
    vn = vc * lax.rsqrt(jnp.mean(vc * vc, axis=-1, keepdims=True) + NORM_EPS) * lng_ref[...] + lnb_ref[...]
    vb = vn.astype(BF16)
    for n in range(n_sub):
        rows = slice(n * A_CHUNK, (n + 1) * A_CHUNK)
        for g in range(A_GROUPS):
            cols = slice(g * LANES, (g + 1) * LANES)
            s = _dot(ws_ref[g], vb[rows, cols]) + bs_ref[:, cols]
            ua_ref[rows, cols] = (jax.nn.gelu(uv[rows, cols]) * s).astype(BF16)
    y_a = _dot(ua_ref[...], wa_ref[...])
    y_b = _dot(og_ref[...], wb_ref[...])
    gates = jax.nn.sigmoid(_dot(h, wg_ref[...]))
    t = gates[:, :D_MODEL] * y_a + gates[:, D_MODEL:] * y_b
    y = _dot(t.astype(BF16), wo_ref[...])
    o_ref[...] = x + mod_ref[:, 2 * D_MODEL:3 * D_MODEL] * y


def _resident(shape):
    return pl.BlockSpec(shape, lambda i: (0,) * len(shape), pipeline_mode=pl.Buffered(1))


def _mix_call(x2d, og, mod3, g1, wuv16, wg16, ln_g, ln_b, ws16, bs_cols, wa16, wb16, wo16, tm, tiles_per_batch):
    m = x2d.shape[0]
    rows = pl.BlockSpec((tm, D_MODEL), lambda i: (i, 0))
    return pl.pallas_call(
        functools.partial(_mix_kernel, n_sub=tm // A_CHUNK),
        grid=(m // tm,),
        in_specs=[rows, rows,
                  pl.BlockSpec((None, 1, 6 * D_MODEL), lambda i: (i // tiles_per_batch, 0, 0)),
                  _resident((1, D_MODEL)),
                  _resident((D_MODEL, 2 * D_MODEL)), _resident((D_MODEL, 2 * D_MODEL)),
                  _resident((1, D_MODEL)), _resident((1, D_MODEL)),
                  _resident((A_GROUPS, A_CHUNK, A_CHUNK)), _resident((A_CHUNK, D_MODEL)),
                  _resident((D_MODEL, D_MODEL)), _resident((D_MODEL, D_MODEL)), _resident((D_MODEL, D_MODEL))],
        out_specs=rows,
        out_shape=jax.ShapeDtypeStruct((m, D_MODEL), F32),
        scratch_shapes=[pltpu.VMEM((tm, D_MODEL), BF16)],
        compiler_params=pltpu.CompilerParams(
            dimension_semantics=("parallel",), vmem_limit_bytes=VMEM_LIMIT),
        name="mix",
    )(x2d, og, mod3, g1, wuv16, wg16, ln_g, ln_b, ws16, bs_cols, wa16, wb16, wo16)


def _ffn_kernel(x_ref, mod_ref, g2_ref, wa_ref, wb_ref, cw_ref, cb_ref, wd_ref, gf_ref, o_ref, h_ref, acc_ref):
    j = pl.program_id(1)
    tm = x_ref.shape[0]

    @pl.when(j == 0)
    def _():
        h = _norm_mod(x_ref[...], g2_ref[...], mod_ref[:, 3 * D_MODEL:4 * D_MODEL],
                      mod_ref[:, 4 * D_MODEL:5 * D_MODEL])
        h_ref[...] = h.astype(BF16)
        acc_ref[...] = jnp.zeros_like(acc_ref)

    h = h_ref[...]
    a = _dot(h, wa_ref[...])
    b = _dot(h, wb_ref[...])
    col_in_row = lax.broadcasted_iota(jnp.int32, a.shape, 0) & (GRID_W - 1)
    a_prev = jnp.where(col_in_row == 0, 0.0, pltpu.roll(a, 1, 0))
    a_next = jnp.where(col_in_row == GRID_W - 1, 0.0, pltpu.roll(a, tm - 1, 0))
    ac = a_prev * cw_ref[0:1, :] + a * cw_ref[1:2, :] + a_next * cw_ref[2:3, :] + cb_ref[...]
    acc_ref[...] += _dot((jax.nn.gelu(ac) * b).astype(BF16), wd_ref[...])

    @pl.when(j == pl.num_programs(1) - 1)
    def _():
        x2 = x_ref[...] + mod_ref[:, 5 * D_MODEL:6 * D_MODEL] * acc_ref[...]
        ms = jnp.mean(x2 * x2, axis=-1, keepdims=True)
        o_ref[...] = x2 * lax.rsqrt(ms + NORM_EPS) * gf_ref[...]


def _ffn_call(x1, mod3, norm2_g, w_up16, conv_w, conv_b, w_down16, final_g, tm, tf, tiles_per_batch):
    m = x1.shape[0]
    nf = D_FF // tf
    return pl.pallas_call(
        _ffn_kernel,
        grid=(m // tm, nf),
        in_specs=[pl.BlockSpec((tm, D_MODEL), lambda i, j: (i, 0)),
                  pl.BlockSpec((None, 1, 6 * D_MODEL), lambda i, j: (i // tiles_per_batch, 0, 0)),
                  pl.BlockSpec((1, D_MODEL), lambda i, j: (0, 0)),
                  pl.BlockSpec((D_MODEL, tf), lambda i, j: (0, j)),
                  pl.BlockSpec((D_MODEL, tf), lambda i, j: (0, nf + j)),
                  pl.BlockSpec((3, tf), lambda i, j: (0, j)),
                  pl.BlockSpec((1, tf), lambda i, j: (0, j)),
                  pl.BlockSpec((tf, D_MODEL), lambda i, j: (j, 0)),
                  pl.BlockSpec((1, D_MODEL), lambda i, j: (0, 0))],
        out_specs=pl.BlockSpec((tm, D_MODEL), lambda i, j: (i, 0)),
        out_shape=jax.ShapeDtypeStruct((m, D_MODEL), F32),
        scratch_shapes=[pltpu.VMEM((tm, D_MODEL), BF16), pltpu.VMEM((tm, D_MODEL), F32)],
        compiler_params=pltpu.CompilerParams(
            dimension_semantics=("parallel", "arbitrary"), vmem_limit_bytes=VMEM_LIMIT),
        name="ffn",
    )(x1, mod3, norm2_g, w_up16, w_up16, conv_w, conv_b, w_down16, final_g)


def kernel(x, c, ctx, c_ctx, w_mod, b_mod, norm1_g, w_in, conv_qkv, a_log, dt_bias, onorm_g, w_proj_b,
           a_ln_g, a_ln_b, a_ws, a_bs, w_proj_a, w_out, norm2_g, w_up, ffn_conv_w, ffn_conv_b, w_down,
           final_g):
    bsz, length, _ = x.shape
    ctx_len = ctx.shape[1]
    assert w_mod.shape[0] == 1 and bsz <= 7
    assert length % 1024 == 0 and ctx_len % CHUNK == 0

    w = w_in[0]
    w_kvqz = jnp.concatenate([w[:, :OFF_BA], w[:, OFF_Q:OFF_U]], axis=1).astype(BF16)
    w_kv = w[:, :OFF_BA].astype(BF16)
    w_ba = jnp.pad(w[:, OFF_BA:OFF_Q], ((0, 0), (0, LANES - 4 * N_HEADS))).astype(BF16)
    w_uv = w[:, OFF_U:OFF_GA].astype(BF16)
    w_gates = w[:, OFF_GA:].astype(BF16)
    lane_pad = (2 * N_HEADS, LANES - 4 * N_HEADS)
    a_lane = jnp.pad(jnp.exp(a_log[0].astype(F32)).reshape(-1), lane_pad).reshape(1, LANES)
    dt_lane = jnp.pad(dt_bias[0].astype(F32).reshape(-1), lane_pad).reshape(1, LANES)
    bs_cols = jnp.repeat(a_bs[0].T, LANES, axis=1)

    cond8 = jnp.concatenate([c, c_ctx[None, :], jnp.zeros((7 - bsz, D_MODEL), F32)], axis=0)
    mod3 = _mod_call(cond8, w_mod[0], b_mod[0]).reshape(8, 1, 6 * D_MODEL)

    x2d = x.reshape(bsz * length, D_MODEL)
    g1 = norm1_g[0].reshape(1, D_MODEL)
    tm_in = 1024
    tiles_per_seq = length // tm_in
    p2d, ba2d = _inproj_call(x2d, mod3, lambda i: i // tiles_per_seq, g1, w_kvqz, w_ba, conv_qkv[0],
                             ('k', 'v', 'q', 'z'), tm_in, tiles_per_seq, "in_proj")
    pc2d, cba2d = _inproj_call(ctx.reshape(bsz * ctx_len, D_MODEL), mod3, lambda i: bsz, g1, w_kv, w_ba,
                               conv_qkv[0], ('k', 'v'), ctx_len, 1, "in_proj_ctx")
    p3 = p2d.reshape(bsz, length, 4 * B_WIDTH)
    pc3 = pc2d.reshape(bsz, ctx_len, 2 * B_WIDTH)

    gcol, grow = _gates_call(ba2d.reshape(bsz, length, LANES), a_lane, dt_lane, "gates")
    cgcol, cgrow = _gates_call(cba2d.reshape(bsz, ctx_len, LANES), a_lane, dt_lane, "gates_ctx")
    og = _delta_call(p3, gcol, grow, pc3, cgcol, cgrow, onorm_g[0].reshape(1, HEAD_DIM))

    tm = 512
    tiles_per_batch = length // tm
    x1 = _mix_call(x2d, og.reshape(bsz * length, B_WIDTH), mod3, g1, w_uv, w_gates,
                   a_ln_g[0].reshape(1, -1), a_ln_b[0].reshape(1, -1), a_ws[0].astype(BF16), bs_cols,
                   w_proj_a[0].astype(BF16), w_proj_b[0].astype(BF16), w_out[0].astype(BF16), tm, tiles_per_batch)
    out = _ffn_call(x1, mod3, norm2_g[0].reshape(1, -1), w_up[0].astype(BF16), ffn_conv_w[0],
                    ffn_conv_b[0].reshape(1, -1), w_down[0].astype(BF16), final_g.reshape(1, -1),
                    tm, D_FF // 2, tiles_per_batch)
    return out.reshape(bsz, length, D_MODEL)
```

```python
import functools

import jax
import jax.numpy as jnp
from jax import lax
from jax.experimental import pallas as pl
from jax.experimental.pallas import tpu as pltpu

F32 = jnp.float32
BF16 = jnp.bfloat16

D_MODEL = 1024
GRID_W = 64
NORM_EPS = 1e-6
N_HEADS = 8
HEAD_DIM = 128
B_WIDTH = N_HEADS * HEAD_DIM
A_GROUPS = 8
A_CHUNK = 128
D_FF = 2816
OFF_BA = 2 * B_WIDTH
OFF_Q = OFF_BA + 4 * N_HEADS
OFF_U = OFF_Q + 2 * B_WIDTH
OFF_GA = OFF_U + 2 * D_MODEL
LANES = 128
MXU_COLS = 256
CHUNK = 128
GROUP = 8

VMEM_LIMIT = 56 * 1024 * 1024


def _silu(x):
    return x * jax.nn.sigmoid(x)


def _dot(a, b):
    return jnp.dot(a, b, preferred_element_type=F32)


def _dot_nt(a, b):
    return lax.dot_general(a, b, (((1,), (1,)), ((), ())), preferred_element_type=F32)


def _mod_kernel(c_ref, w_ref, b_ref, o_ref):
    cond = _silu(c_ref[...])
    o_ref[...] = jnp.dot(cond, w_ref[...], preferred_element_type=F32,
                         precision=lax.Precision.HIGHEST) + b_ref[...]


def _mod_call(cond8, w_mod, b_mod):
    n = w_mod.shape[1]
    tn = 1536
    return pl.pallas_call(
        _mod_kernel,
        grid=(n // tn,),
        in_specs=[pl.BlockSpec((8, D_MODEL), lambda j: (0, 0)),
                  pl.BlockSpec((D_MODEL, tn), lambda j: (0, j)),
                  pl.BlockSpec((1, tn), lambda j: (0, j))],
        out_specs=pl.BlockSpec((8, tn), lambda j: (0, j)),
        out_shape=jax.ShapeDtypeStruct((8, n), F32),
        name="mod",
    )(cond8, w_mod, b_mod.reshape(1, n))


def _norm_mod(x, g, shift, scale):
    ms = jnp.mean(x * x, axis=-1, keepdims=True)
    y = x * lax.rsqrt(ms + NORM_EPS) * g
    return y * (1.0 + scale) + shift


def _l2norm(t):
    return t * lax.rsqrt(jnp.sum(t * t, axis=-1, keepdims=True) + NORM_EPS)


def _gate_columns(ba, a_lane, dt_lane):
    n_sub = ba.shape[0] // CHUNK
    lane = lax.broadcasted_iota(jnp.int32, ba.shape, 1)
    ri = lax.broadcasted_iota(jnp.int32, (CHUNK, CHUNK), 0)
    ci = lax.broadcasted_iota(jnp.int32, (CHUNK, CHUNK), 1)
    tri_lo, tri_up = (ri >= ci).astype(F32), (ri <= ci).astype(F32)
    beta = jax.nn.sigmoid(ba)
    y = ba + dt_lane
    softplus = jnp.maximum(y, 0.0) + jnp.log1p(jnp.exp(-jnp.abs(y)))
    g = jnp.where((lane >= 2 * N_HEADS) & (lane < 4 * N_HEADS), -a_lane * softplus, 0.0)
    part = 4 * N_HEADS
    hi = g.astype(BF16).astype(F32)
    mid = (g - hi).astype(BF16).astype(F32)
    lo = (g - hi - mid).astype(BF16).astype(F32)
    packed = (hi + pltpu.roll(mid, part, 1) + pltpu.roll(lo, 2 * part, 1)).astype(BF16)

    def unpack(r):
        return r + pltpu.roll(r, LANES - part, 1) + pltpu.roll(r, LANES - 2 * part, 1)

    tri_lo, tri_up = tri_lo.astype(BF16), tri_up.astype(BF16)
    parts = [packed[s * CHUNK:(s + 1) * CHUNK, :] for s in range(n_sub)]
    pre = [unpack(_dot(tri_lo, x)) for x in parts]
    suf = [unpack(_dot(tri_up, x)) for x in parts]
    lane = lax.broadcasted_iota(jnp.int32, (CHUNK, LANES), 1)
    return [jnp.where(lane < 2 * N_HEADS, beta[s * CHUNK:(s + 1) * CHUNK, :],
                      jnp.where(lane < 3 * N_HEADS, pre[s], suf[s])) for s in range(n_sub)]


def _inproj_kernel(x_ref, xb_ref, xa_ref, mod_ref, g_ref, w_ref, wba_ref, cw_ref, a_ref, dt_ref,
                   o_ref, gcol_ref, grow_ref, h_ref, halo_ref, *, tiles_per_seq, kinds):
    i, j = pl.program_id(0), pl.program_id(1)
    tm, tn = o_ref.shape
    shift, scale = mod_ref[:, 0:D_MODEL], mod_ref[:, D_MODEL:2 * D_MODEL]

    @pl.when(j == 0)
    def _():
        h_ref[...] = _norm_mod(x_ref[...], g_ref[...], shift, scale).astype(BF16)
        halo = jnp.concatenate([xb_ref[...], xa_ref[...]], axis=0)
        halo_ref[...] = _norm_mod(halo, g_ref[...], shift, scale).astype(BF16)
        ba = _dot(h_ref[...], wba_ref[...])
        for s, col in enumerate(_gate_columns(ba, a_ref[...], dt_ref[...])):
            rows = slice(s * CHUNK, (s + 1) * CHUNK)
            gcol_ref[rows, :] = col
            grow_ref[:, rows] = col.T

    tile_in_seq = i % tiles_per_seq
    slab = MXU_COLS
    row = lax.broadcasted_iota(jnp.int32, (tm, slab), 0)
    tiled = (tm // 8, 8, slab)

    def epilogue(kind):
        for s in range(tn // slab):
            cols = slice(s * slab, (s + 1) * slab)
            p = _dot(h_ref[...], w_ref[:, cols])
            if kind != 'z':
                edge = _dot(halo_ref[...], w_ref[:, cols])
                prev_row = jnp.where(tile_in_seq > 0, edge[7:8, :], 0.0)
                next_row = jnp.where(tile_in_seq < tiles_per_seq - 1, edge[8:9, :], 0.0)
                xp = jnp.where(row == 0, prev_row, pltpu.roll(p, 1, 0))
                xn = jnp.where(row == tm - 1, next_row, pltpu.roll(p, tm - 1, 0))
                taps = [jnp.broadcast_to(cw_ref[r:r + 1, cols], (8, slab)) for r in range(3)]
                p = (xp.reshape(tiled) * taps[0] + p.reshape(tiled) * taps[1]
                     + xn.reshape(tiled) * taps[2]).reshape(tm, slab)
            y = _silu(p)
            for hd in range(slab // HEAD_DIM):
                head = y[:, hd * HEAD_DIM:(hd + 1) * HEAD_DIM]
                if kind in ('k', 'q'):
                    head = _l2norm(head)
                if kind == 'q':
                    head = head * (HEAD_DIM ** -0.5)
                o_ref[:, s * slab + hd * HEAD_DIM:s * slab + (hd + 1) * HEAD_DIM] = head.astype(o_ref.dtype)

    for jj, kind in enumerate(kinds):
        pl.when(j == jj)(functools.partial(epilogue, kind))


def _inproj_call(x2d, mod3, mod_row_of_tile, norm_g, w, w_ba, conv_w, a_lane, dt_lane, kinds, tm, tiles_per_seq,
                 name):
    m = x2d.shape[0]
    n = w.shape[1]
    tn = B_WIDTH
    assert n == tn * len(kinds) and tm % CHUNK == 0
    conv_groups = conv_w.shape[1] // tn
    last_halo = m // 8 - 1
    return pl.pallas_call(
        functools.partial(_inproj_kernel, tiles_per_seq=tiles_per_seq, kinds=kinds),
        grid=(m // tm, n // tn),
        in_specs=[pl.BlockSpec((tm, D_MODEL), lambda i, j: (i, 0)),
                  pl.BlockSpec((8, D_MODEL), lambda i, j: (jnp.maximum(i * (tm // 8) - 1, 0), 0)),
                  pl.BlockSpec((8, D_MODEL), lambda i, j: (jnp.minimum((i + 1) * (tm // 8), last_halo), 0)),
                  pl.BlockSpec((None, 1, 6 * D_MODEL), lambda i, j: (mod_row_of_tile(i), 0, 0)),
                  pl.BlockSpec((1, D_MODEL), lambda i, j: (0, 0)),
                  pl.BlockSpec((D_MODEL, tn), lambda i, j: (0, j)),
                  pl.BlockSpec((D_MODEL, LANES), lambda i, j: (0, 0)),
                  pl.BlockSpec((3, tn), lambda i, j: (0, jnp.minimum(j, conv_groups - 1))),
                  pl.BlockSpec((1, LANES), lambda i, j: (0, 0)),
                  pl.BlockSpec((1, LANES), lambda i, j: (0, 0))],
        out_specs=[pl.BlockSpec((tm, tn), lambda i, j: (i, j)),
                   pl.BlockSpec((tm, LANES), lambda i, j: (i, 0)),
                   pl.BlockSpec((LANES, tm), lambda i, j: (0, i))],
        out_shape=[jax.ShapeDtypeStruct((m, n), BF16), jax.ShapeDtypeStruct((m, LANES), F32),
                   jax.ShapeDtypeStruct((LANES, m), F32)],
        scratch_shapes=[pltpu.VMEM((tm, D_MODEL), BF16), pltpu.VMEM((16, D_MODEL), BF16)],
        compiler_params=pltpu.CompilerParams(
            dimension_semantics=("parallel", "arbitrary"), vmem_limit_bytes=VMEM_LIMIT),
        name=name,
    )(x2d, x2d, x2d, mod3, norm_g, w, w_ba, conv_w, a_lane, dt_lane)


def _run_interleaved(*gens):
    results = [None] * len(gens)
    live = list(range(len(gens)))
    while live:
        for idx in list(live):
            try:
                next(gens[idx])
            except StopIteration as stop:
                results[idx] = stop.value
                live.remove(idx)
    return results


def _tri_inverse_stages(n_mats, ri, ci):
    def same_block(size):
        shift = size.bit_length() - 1
        return (ri >> shift) == (ci >> shift)

    eye = (ri == ci).astype(F32)
    blk16 = same_block(16)
    n16 = [jnp.where(blk16, n, 0.0) for n in n_mats]
    ts = [eye + n for n in n16]
    pbs = [n.astype(BF16) for n in n16]
    pbs = [_dot(p, p).astype(BF16) for p in pbs]
    yield
    for _ in range(2):
        ts = [t + _dot(t.astype(BF16), p) for t, p in zip(ts, pbs)]
        pbs = [_dot(p, p).astype(BF16) for p in pbs]
        yield
    ts = [t + _dot(t.astype(BF16), p) for t, p in zip(ts, pbs)]
    yield
    tbs = [t.astype(BF16) for t in ts]
    for size in (32, 64, 128):
        off_mask = same_block(size) & jnp.logical_not(same_block(size // 2))
        xs = [_dot(jnp.where(off_mask, n, 0.0).astype(BF16), tb).astype(BF16) for n, tb in zip(n_mats, tbs)]
        yield
        tbs = [tb + _dot(tb, x).astype(BF16) for tb, x in zip(tbs, xs)]
        yield
    return tbs


def _chunk_start(c):
    return c * CHUNK if isinstance(c, int) else pl.multiple_of(c * CHUNK, CHUNK)


def _prep_stages(chunks, h, k_ref, v_ref, q_ref, gcol_ref, grow_ref):
    lane = lax.broadcasted_iota(jnp.int32, (CHUNK, LANES), 1)
    ri = lax.broadcasted_iota(jnp.int32, (CHUNK, CHUNK), 0)
    ci = lax.broadcasted_iota(jnp.int32, (CHUNK, CHUNK), 1)
    head_row = lax.broadcasted_iota(jnp.int32, (N_HEADS, CHUNK), 0)
    incl = (ri >= ci, ri <= ci)
    strict = (ri > ci, ri < ci)

    acts, prods = [], []
    for c in chunks:
        rows = pl.ds(_chunk_start(c), CHUNK)
        kb16 = k_ref[rows, :]
        k = kb16.astype(F32)
        kt16 = k.T.astype(BF16)
        v = v_ref[rows, :].astype(F32)
        if q_ref is not None:
            qb16 = q_ref[rows, :]
            aa = _dot(jnp.concatenate([kb16, qb16], axis=0), kt16)
            acts.append((k, v, qb16.astype(F32)))
            prods.append((aa[:CHUNK], aa[CHUNK:]))
        else:
            acts.append((k, v, None))
            prods.append((_dot(kb16, kt16), None))
    yield

    chains = []
    for c, (k, v, q), (a_kk, a_qk) in zip(chunks, acts, prods):
        r0 = _chunk_start(c)
        gates = gcol_ref[pl.ds(r0, CHUNK), :]
        for d in range(2):
            beta = jnp.sum(jnp.where(lane == d * N_HEADS + h, gates, 0.0), axis=1, keepdims=True)
            gc = jnp.sum(jnp.where(lane == (2 + d) * N_HEADS + h, gates, 0.0), axis=1, keepdims=True)
            all_heads = grow_ref[(2 + d) * N_HEADS:(3 + d) * N_HEADS, pl.ds(r0, CHUNK)]
            gc_row = jnp.sum(jnp.where(head_row == h, all_heads, 0.0), axis=0, keepdims=True)
            gamma = jnp.where(incl[d], jnp.exp(jnp.where(incl[d], gc - gc_row, 0.0)), 0.0)
            neg_m = jnp.where(strict[d], -(a_kk * beta * gamma), 0.0)
            chains.append((k, v, q, a_qk, d, beta, gc, gamma, neg_m))

    t_invs = yield from _tri_inverse_stages([ch[-1] for ch in chains], ri, ci)
    rhss = [jnp.concatenate([k * (beta * jnp.exp(gc)), v * beta], axis=1).astype(BF16)
            for k, v, _, _, _, beta, gc, _, _ in chains]
    wus = [_dot(t.astype(BF16), rhs) for t, rhs in zip(t_invs, rhss)]
    yield

    out = []
    for (k, v, q, a_qk, d, beta, gc, gamma, _), wu in zip(chains, wus):
        w, u = wu[:, :HEAD_DIM], wu[:, HEAD_DIM:]
        g_end = gc[CHUNK - 1:CHUNK, :] if d == 0 else gc[0:1, :]
        kd = k * jnp.exp(g_end - gc)
        decay = jnp.broadcast_to(jnp.exp(g_end), (8, LANES))
        if q is not None:
            qg = q * jnp.exp(gc)
            qk = jnp.where(incl[d], a_qk * gamma, 0.0)
        else:
            qg = qk = None
        out.append((w, u, kd, decay, qg, qk))
    return [out[2 * i:2 * i + 2] for i in range(len(chunks))]


def _delta_kernel(pk_ref, pv_ref, pq_ref, gcol_ref, grow_ref, ck_ref, cv_ref, cgcol_ref, cgrow_ref,
                  sz_ref, onorm_ref, og_ref,
                  wq_sc, kq_sc, u_sc, ge_sc, s0_sc, o_sc, *, n_chunks, n_ctx_chunks, n_heads_total):
    t = pl.program_id(0)
    h = jnp.minimum(t, n_heads_total - 1) % N_HEADS
    slot_p = t % 2
    slot_s = 1 - slot_p

    @pl.when(t == 0)
    def _():
        wq_sc[1] = jnp.zeros(wq_sc.shape[1:], wq_sc.dtype)
        kq_sc[1] = jnp.zeros(kq_sc.shape[1:], kq_sc.dtype)
        u_sc[1] = jnp.zeros(u_sc.shape[1:], u_sc.dtype)
        ge_sc[1] = jnp.zeros(ge_sc.shape[1:], ge_sc.dtype)
        s0_sc[1] = jnp.zeros(s0_sc.shape[1:], s0_sc.dtype)

    def context_states():
        ctx = yield from _prep_stages(list(range(n_ctx_chunks)), h, ck_ref, cv_ref, None, cgcol_ref, cgrow_ref)
        states = [jnp.zeros((HEAD_DIM, HEAD_DIM), F32)] * 2
        for c in range(n_ctx_chunks):
            steps = [ctx[c][0], ctx[n_ctx_chunks - 1 - c][1]]
            ws = [_dot(st[0].astype(BF16), s.astype(BF16)) for st, s in zip(steps, states)]
            yield
            v_new = [st[1] - x for st, x in zip(steps, ws)]
            kv = [_dot(st[2].T.astype(BF16), vn.astype(BF16)) for st, vn in zip(steps, v_new)]
            yield
            states = [s * st[3][0:1, :] + x for s, st, x in zip(states, steps, kv)]
        for d in range(2):
            s0_sc[slot_p, d] = states[d]

    def prepare(i):
        chunks = [i * GROUP + g for g in range(GROUP)]
        prepped = yield from _prep_stages(chunks, h, pk_ref, pv_ref, pq_ref, gcol_ref, grow_ref)
        for c, both in zip(chunks, prepped):
            r0 = _chunk_start(c)
            for d in range(2):
                w, u, kd, decay, qg, qk = both[d]
                wq_sc[slot_p, d, c] = jnp.concatenate([w, qg], axis=0).astype(BF16)
                kq_sc[slot_p, d, c] = jnp.concatenate([kd.T, qk], axis=0).astype(BF16)
                u_sc[slot_p, d, pl.ds(r0, CHUNK), :] = u
                ge_sc[slot_p, d, c] = decay

    def finish(c, o):
        r0 = _chunk_start(c)
        y = o * lax.rsqrt(jnp.mean(o * o, axis=-1, keepdims=True) + NORM_EPS) * onorm_ref[...]
        og_ref[pl.ds(r0, CHUNK), :] = (y * sz_ref[pl.ds(r0, CHUNK), :].astype(F32)).astype(og_ref.dtype)

    def scan(i, states, second_half):
        for g in range(GROUP):
            j = i * GROUP + g
            chunks = (j, n_chunks - 1 - j)
            ws = [_dot(wq_sc[slot_s, d, c], s.astype(BF16)) for d, (c, s) in enumerate(zip(chunks, states))]
            yield
            v_new = [u_sc[slot_s, d, pl.ds(_chunk_start(c), CHUNK), :] - x[:CHUNK]
                     for d, (c, x) in enumerate(zip(chunks, ws))]
            kv = [_dot(kq_sc[slot_s, d, c], vn.astype(BF16)) for d, (c, vn) in enumerate(zip(chunks, v_new))]
            yield
            states = [s * ge_sc[slot_s, d, c][0:1, :] + x[:CHUNK]
                      for d, (c, s, x) in enumerate(zip(chunks, states, kv))]
            for c, x, y in zip(chunks, ws, kv):
                o = x[CHUNK:] + y[CHUNK:]
                rows = pl.ds(_chunk_start(c), CHUNK)
                if second_half:
                    finish(c, o_sc[rows, :] + o)
                else:
                    o_sc[rows, :] = o
        return tuple(states)

    def body(second_half, i, states, *also):
        return _run_interleaved(scan(i, states, second_half), prepare(i), *also)[0]

    n_iter = n_chunks // GROUP
    states = (s0_sc[slot_s, 0], s0_sc[slot_s, 1])
    states = body(False, 0, states, context_states())
    states = lax.fori_loop(1, n_iter // 2, functools.partial(body, False), states)
    lax.fori_loop(n_iter // 2, n_iter, functools.partial(body, True), states)


def _delta_call(p3, gcol, grow, pc3, cgcol, cgrow, onorm_g):
    bsz, length, _ = p3.shape
    ctx_len = pc3.shape[1]
    n_chunks = length // CHUNK
    n_ctx_chunks = ctx_len // CHUNK
    n_heads_total = bsz * N_HEADS
    assert n_chunks % (2 * GROUP) == 0

    def prepared(t):
        t = jnp.minimum(t, n_heads_total - 1)
        return t // N_HEADS, t % N_HEADS

    def scanned(t):
        t = jnp.maximum(t - 1, 0)
        return t // N_HEADS, t % N_HEADS

    def head_cols(rows, first_block, which):
        def index(t):
            b, h = which(t)
            return b, 0, first_block + h
        return pl.BlockSpec((None, rows, LANES), index)

    def gates_col(rows):
        return pl.BlockSpec((None, rows, LANES), lambda t: (prepared(t)[0], 0, 0), pipeline_mode=pl.Buffered(1))

    def gates_row(rows):
        return pl.BlockSpec((LANES, rows), lambda t: (0, prepared(t)[0]), pipeline_mode=pl.Buffered(1))

    return pl.pallas_call(
        functools.partial(_delta_kernel, n_chunks=n_chunks, n_ctx_chunks=n_ctx_chunks,
                          n_heads_total=n_heads_total),
        grid=(n_heads_total + 1,),
        in_specs=[head_cols(length, 0, prepared), head_cols(length, N_HEADS, prepared),
                  head_cols(length, 2 * N_HEADS, prepared),
                  gates_col(length), gates_row(length),
                  head_cols(ctx_len, 0, prepared), head_cols(ctx_len, N_HEADS, prepared),
                  gates_col(ctx_len), gates_row(ctx_len),
                  head_cols(length, 3 * N_HEADS, scanned),
                  pl.BlockSpec((1, LANES), lambda t: (0, 0))],
        out_specs=head_cols(length, 0, scanned),
        out_shape=jax.ShapeDtypeStruct((bsz, length, B_WIDTH), BF16),
        scratch_shapes=[pltpu.VMEM((2, 2, n_chunks, 2 * CHUNK, LANES), BF16),
                        pltpu.VMEM((2, 2, n_chunks, 2 * CHUNK, LANES), BF16),
                        pltpu.VMEM((2, 2, length, LANES), F32),
                        pltpu.VMEM((2, 2, n_chunks, 8, LANES), F32),
                        pltpu.VMEM((2, 2, HEAD_DIM, HEAD_DIM), F32),
                        pltpu.VMEM((length, LANES), F32)],
        compiler_params=pltpu.CompilerParams(
            dimension_semantics=("arbitrary",), vmem_limit_bytes=62 * 1024 * 1024),
        name="delta",
    )(p3, p3, p3, gcol, grow, pc3, pc3, cgcol, cgrow, p3, onorm_g)


def _mix_kernel(x_ref, og_ref, mod_ref, g1_ref, wuv_ref, wg_ref, lng_ref, lnb_ref, ws_ref, bs_ref,
                wa_ref, wb_ref, wo_ref, o_ref, ua_ref, *, n_sub):
    x = x_ref[...]
    h = _norm_mod(x, g1_ref[...], mod_ref[:, 0:D_MODEL], mod_ref[:, D_MODEL:2 * D_MODEL]).astype(BF16)
    uv = _dot(h, wuv_ref[...])
    v = jax.nn.gelu(uv[:, D_MODEL:])
    vc = v - jnp.mean(v, axis=-1, keepdims=True)
    vn = vc * lax.rsqrt(jnp.mean(vc * vc, axis=-1, keepdims=True) + NORM_EPS) * lng_ref[...] + lnb_ref[...]
    vb = vn.astype(BF16)
    for n in range(n_sub):
        rows = slice(n * A_CHUNK, (n + 1) * A_CHUNK)
        for g in range(A_GROUPS):
            cols = slice(g * LANES, (g + 1) * LANES)
            s = _dot(ws_ref[g], vb[rows, cols]) + bs_ref[:, cols]
            ua_ref[rows, cols] = (jax.nn.gelu(uv[rows, cols]) * s).astype(BF16)
    y_a = _dot(ua_ref[...], wa_ref[...])
    y_b = _dot(og_ref[...], wb_ref[...])
    gates = jax.nn.sigmoid(_dot(h, wg_ref[...]))
    t = gates[:, :D_MODEL] * y_a + gates[:, D_MODEL:] * y_b
    y = _dot(t.astype(BF16), wo_ref[...])
    o_ref[...] = x + mod_ref[:, 2 * D_MODEL:3 * D_MODEL] * y


def _resident(shape):
    return pl.BlockSpec(shape, lambda i: (0,) * len(shape), pipeline_mode=pl.Buffered(1))


def _mix_call(x2d, og, mod3, g1, wuv16, wg16, ln_g, ln_b, ws16, bs_cols, wa16, wb16, wo16, tm, tiles_per_batch):
    m = x2d.shape[0]
    rows = pl.BlockSpec((tm, D_MODEL), lambda i: (i, 0))
    return pl.pallas_call(
        functools.partial(_mix_kernel, n_sub=tm // A_CHUNK),
        grid=(m // tm,),
        in_specs=[rows, rows,
                  pl.BlockSpec((None, 1, 6 * D_MODEL), lambda i: (i // tiles_per_batch, 0, 0)),
                  _resident((1, D_MODEL)),
                  _resident((D_MODEL, 2 * D_MODEL)), _resident((D_MODEL, 2 * D_MODEL)),
                  _resident((1, D_MODEL)), _resident((1, D_MODEL)),
                  _resident((A_GROUPS, A_CHUNK, A_CHUNK)), _resident((A_CHUNK, D_MODEL)),
                  _resident((D_MODEL, D_MODEL)), _resident((D_MODEL, D_MODEL)), _resident((D_MODEL, D_MODEL))],
        out_specs=rows,
        out_shape=jax.ShapeDtypeStruct((m, D_MODEL), F32),
        scratch_shapes=[pltpu.VMEM((tm, D_MODEL), BF16)],
        compiler_params=pltpu.CompilerParams(
            dimension_semantics=("parallel",), vmem_limit_bytes=VMEM_LIMIT),
        name="mix",
    )(x2d, og, mod3, g1, wuv16, wg16, ln_g, ln_b, ws16, bs_cols, wa16, wb16, wo16)


def _ffn_kernel(x_ref, mod_ref, g2_ref, wa_ref, wb_ref, cw_ref, cb_ref, wd_ref, gf_ref, o_ref, h_ref, acc_ref):
    j = pl.program_id(1)
    tm = x_ref.shape[0]

    @pl.when(j == 0)
    def _():
        h = _norm_mod(x_ref[...], g2_ref[...], mod_ref[:, 3 * D_MODEL:4 * D_MODEL],
                      mod_ref[:, 4 * D_MODEL:5 * D_MODEL])
        h_ref[...] = h.astype(BF16)
        acc_ref[...] = jnp.zeros_like(acc_ref)

    h = h_ref[...]
    a = _dot(h, wa_ref[...])
    b = _dot(h, wb_ref[...])
    col_in_row = lax.broadcasted_iota(jnp.int32, a.shape, 0) & (GRID_W - 1)
    a_prev = jnp.where(col_in_row == 0, 0.0, pltpu.roll(a, 1, 0))
    a_next = jnp.where(col_in_row == GRID_W - 1, 0.0, pltpu.roll(a, tm - 1, 0))
    ac = a_prev * cw_ref[0:1, :] + a * cw_ref[1:2, :] + a_next * cw_ref[2:3, :] + cb_ref[...]
    acc_ref[...] += _dot((jax.nn.gelu(ac) * b).astype(BF16), wd_ref[...])

    @pl.when(j == pl.num_programs(1) - 1)
    def _():
        x2 = x_ref[...] + mod_ref[:, 5 * D_MODEL:6 * D_MODEL] * acc_ref[...]
        ms = jnp.mean(x2 * x2, axis=-1, keepdims=True)
        o_ref[...] = x2 * lax.rsqrt(ms + NORM_EPS) * gf_ref[...]


def _ffn_call(x1, mod3, norm2_g, w_up16, conv_w, conv_b, w_down16, final_g, tm, tf, tiles_per_batch):
    m = x1.shape[0]
    nf = D_FF // tf
    return pl.pallas_call(
        _ffn_kernel,
        grid=(m // tm, nf),
        in_specs=[pl.BlockSpec((tm, D_MODEL), lambda i, j: (i, 0)),
                  pl.BlockSpec((None, 1, 6 * D_MODEL), lambda i, j: (i // tiles_per_batch, 0, 0)),
                  pl.BlockSpec((1, D_MODEL), lambda i, j: (0, 0)),
                  pl.BlockSpec((D_MODEL, tf), lambda i, j: (0, j)),
                  pl.BlockSpec((D_MODEL, tf), lambda i, j: (0, nf + j)),
                  pl.BlockSpec((3, tf), lambda i, j: (0, j)),
                  pl.BlockSpec((1, tf), lambda i, j: (0, j)),
                  pl.BlockSpec((tf, D_MODEL), lambda i, j: (j, 0)),
                  pl.BlockSpec((1, D_MODEL), lambda i, j: (0, 0))],
        out_specs=pl.BlockSpec((tm, D_MODEL), lambda i, j: (i, 0)),
        out_shape=jax.ShapeDtypeStruct((m, D_MODEL), F32),
        scratch_shapes=[pltpu.VMEM((tm, D_MODEL), BF16), pltpu.VMEM((tm, D_MODEL), F32)],
        compiler_params=pltpu.CompilerParams(
            dimension_semantics=("parallel", "arbitrary"), vmem_limit_bytes=VMEM_LIMIT),
        name="ffn",
    )(x1, mod3, norm2_g, w_up16, w_up16, conv_w, conv_b, w_down16, final_g)


def kernel(x, c, ctx, c_ctx, w_mod, b_mod, norm1_g, w_in, conv_qkv, a_log, dt_bias, onorm_g, w_proj_b,
           a_ln_g, a_ln_b, a_ws, a_bs, w_proj_a, w_out, norm2_g, w_up, ffn_conv_w, ffn_conv_b, w_down,
           final_g):
    bsz, length, _ = x.shape
    ctx_len = ctx.shape[1]
    assert w_mod.shape[0] == 1 and bsz <= 7
    assert length % 1024 == 0 and ctx_len % CHUNK == 0

    w = w_in[0]
    w_kvqz = jnp.concatenate([w[:, :OFF_BA], w[:, OFF_Q:OFF_U]], axis=1).astype(BF16)
    w_kv = w[:, :OFF_BA].astype(BF16)
    w_ba = jnp.pad(w[:, OFF_BA:OFF_Q], ((0, 0), (0, LANES - 4 * N_HEADS))).astype(BF16)
    w_uv = w[:, OFF_U:OFF_GA].astype(BF16)
    w_gates = w[:, OFF_GA:].astype(BF16)
    lane_pad = (2 * N_HEADS, LANES - 4 * N_HEADS)
    a_lane = jnp.pad(jnp.exp(a_log[0].astype(F32)).reshape(-1), lane_pad).reshape(1, LANES)
    dt_lane = jnp.pad(dt_bias[0].astype(F32).reshape(-1), lane_pad).reshape(1, LANES)
    bs_cols = jnp.repeat(a_bs[0].T, LANES, axis=1)

    cond8 = jnp.concatenate([c, c_ctx[None, :], jnp.zeros((7 - bsz, D_MODEL), F32)], axis=0)
    mod3 = _mod_call(cond8, w_mod[0], b_mod[0]).reshape(8, 1, 6 * D_MODEL)

    x2d = x.reshape(bsz * length, D_MODEL)
    g1 = norm1_g[0].reshape(1, D_MODEL)
    tm_in = 1024
    tiles_per_seq = length // tm_in
    p2d, gcol, grow = _inproj_call(x2d, mod3, lambda i: i // tiles_per_seq, g1, w_kvqz, w_ba, conv_qkv[0],
                                   a_lane, dt_lane, ('k', 'v', 'q', 'z'), tm_in, tiles_per_seq, "in_proj")
    pc2d, cgcol, cgrow = _inproj_call(ctx.reshape(bsz * ctx_len, D_MODEL), mod3, lambda i: bsz, g1, w_kv, w_ba,
                                      conv_qkv[0], a_lane, dt_lane, ('k', 'v'), ctx_len, 1, "in_proj_ctx")
    p3 = p2d.reshape(bsz, length, 4 * B_WIDTH)
    pc3 = pc2d.reshape(bsz, ctx_len, 2 * B_WIDTH)
    og = _delta_call(p3, gcol.reshape(bsz, length, LANES), grow, pc3, cgcol.reshape(bsz, ctx_len, LANES), cgrow,
                     onorm_g[0].reshape(1, HEAD_DIM))

    tm = 512
    tiles_per_batch = length // tm
    x1 = _mix_call(x2d, og.reshape(bsz * length, B_WIDTH), mod3, g1, w_uv, w_gates,
                   a_ln_g[0].reshape(1, -1), a_ln_b[0].reshape(1, -1), a_ws[0].astype(BF16), bs_cols,
                   w_proj_a[0].astype(BF16), w_proj_b[0].astype(BF16), w_out[0].astype(BF16), tm, tiles_per_batch)
    out = _ffn_call(x1, mod3, norm2_g[0].reshape(1, -1), w_up[0].astype(BF16), ffn_conv_w[0],
                    ffn_conv_b[0].reshape(1, -1), w_down[0].astype(BF16), final_g.reshape(1, -1),
                    tm, D_FF // 2, tiles_per_batch)
    return out.reshape(bsz, length, D_MODEL)
```

```python
import functools

import jax
import jax.numpy as jnp
from jax import lax
from jax.experimental import pallas as pl
from jax.experimental.pallas import tpu as pltpu

F32 = jnp.float32
BF16 = jnp.bfloat16

D_MODEL = 1024
GRID_W = 64
NORM_EPS = 1e-6
N_HEADS = 8
HEAD_DIM = 128
B_WIDTH = N_HEADS * HEAD_DIM
A_GROUPS = 8
A_CHUNK = 128
D_FF = 2816
OFF_BA = 2 * B_WIDTH
OFF_Q = OFF_BA + 4 * N_HEADS
OFF_U = OFF_Q + 2 * B_WIDTH
OFF_GA = OFF_U + 2 * D_MODEL
LANES = 128
MXU_COLS = 256
CHUNK = 128
GROUP = 8

VMEM_LIMIT = 56 * 1024 * 1024


def _silu(x):
    return x * jax.nn.sigmoid(x)


def _dot(a, b):
    return jnp.dot(a, b, preferred_element_type=F32)


def _dot_nt(a, b):
    return lax.dot_general(a, b, (((1,), (1,)), ((), ())), preferred_element_type=F32)


def _mod_kernel(c_ref, w_ref, b_ref, o_ref):
    cond = _silu(c_ref[...])
    o_ref[...] = jnp.dot(cond, w_ref[...], preferred_element_type=F32,
                         precision=lax.Precision.HIGHEST) + b_ref[...]


def _mod_call(cond8, w_mod, b_mod):
    n = w_mod.shape[1]
    tn = 1536
    return pl.pallas_call(
        _mod_kernel,
        grid=(n // tn,),
        in_specs=[pl.BlockSpec((8, D_MODEL), lambda j: (0, 0)),
                  pl.BlockSpec((D_MODEL, tn), lambda j: (0, j)),
                  pl.BlockSpec((1, tn), lambda j: (0, j))],
        out_specs=pl.BlockSpec((8, tn), lambda j: (0, j)),
        out_shape=jax.ShapeDtypeStruct((8, n), F32),
        name="mod",
    )(cond8, w_mod, b_mod.reshape(1, n))


def _norm_mod(x, g, shift, scale):
    ms = jnp.mean(x * x, axis=-1, keepdims=True)
    y = x * lax.rsqrt(ms + NORM_EPS) * g
    return y * (1.0 + scale) + shift


def _l2norm(t):
    return t * lax.rsqrt(jnp.sum(t * t, axis=-1, keepdims=True) + NORM_EPS)


def _gate_columns(ba, a_lane, dt_lane):
    n_sub = ba.shape[0] // CHUNK
    lane = lax.broadcasted_iota(jnp.int32, ba.shape, 1)
    ri = lax.broadcasted_iota(jnp.int32, (CHUNK, CHUNK), 0)
    ci = lax.broadcasted_iota(jnp.int32, (CHUNK, CHUNK), 1)
    tri_lo, tri_up = (ri >= ci).astype(F32), (ri <= ci).astype(F32)
    beta = jax.nn.sigmoid(ba)
    y = ba + dt_lane
    softplus = jnp.maximum(y, 0.0) + jnp.log1p(jnp.exp(-jnp.abs(y)))
    g = jnp.where((lane >= 2 * N_HEADS) & (lane < 4 * N_HEADS), -a_lane * softplus, 0.0)
    part = 4 * N_HEADS
    hi = g.astype(BF16).astype(F32)
    mid = (g - hi).astype(BF16).astype(F32)
    lo = (g - hi - mid).astype(BF16).astype(F32)
    packed = (hi + pltpu.roll(mid, part, 1) + pltpu.roll(lo, 2 * part, 1)).astype(BF16)

    def unpack(r):
        return r + pltpu.roll(r, LANES - part, 1) + pltpu.roll(r, LANES - 2 * part, 1)

    tri_lo, tri_up = tri_lo.astype(BF16), tri_up.astype(BF16)
    parts = [packed[s * CHUNK:(s + 1) * CHUNK, :] for s in range(n_sub)]
    pre = [unpack(_dot(tri_lo, x)) for x in parts]
    suf = [unpack(_dot(tri_up, x)) for x in parts]
    lane = lax.broadcasted_iota(jnp.int32, (CHUNK, LANES), 1)
    return [jnp.where(lane < 2 * N_HEADS, beta[s * CHUNK:(s + 1) * CHUNK, :],
                      jnp.where(lane < 3 * N_HEADS, pre[s], suf[s])) for s in range(n_sub)]


def _inproj_kernel(x_ref, xb_ref, xa_ref, mod_ref, g_ref, w_ref, wba_ref, cw_ref, a_ref, dt_ref,
                   o_ref, gcol_ref, grow_ref, h_ref, halo_ref, *, tiles_per_seq, kinds):
    i, j = pl.program_id(0), pl.program_id(1)
    tm, tn = o_ref.shape
    shift, scale = mod_ref[:, 0:D_MODEL], mod_ref[:, D_MODEL:2 * D_MODEL]

    @pl.when(j == 0)
    def _():
        h_ref[...] = _norm_mod(x_ref[...], g_ref[...], shift, scale).astype(BF16)
        halo = jnp.concatenate([xb_ref[...], xa_ref[...]], axis=0)
        halo_ref[...] = _norm_mod(halo, g_ref[...], shift, scale).astype(BF16)
        ba = _dot(h_ref[...], wba_ref[...])
        for s, col in enumerate(_gate_columns(ba, a_ref[...], dt_ref[...])):
            rows = slice(s * CHUNK, (s + 1) * CHUNK)
            gcol_ref[rows, :] = col
            grow_ref[:, rows] = col.T

    tile_in_seq = i % tiles_per_seq
    slab = MXU_COLS
    row = lax.broadcasted_iota(jnp.int32, (tm, slab), 0)
    tiled = (tm // 8, 8, slab)

    def epilogue(kind):
        for s in range(tn // slab):
            cols = slice(s * slab, (s + 1) * slab)
            p = _dot(h_ref[...], w_ref[:, cols])
            if kind != 'z':
                edge = _dot(halo_ref[...], w_ref[:, cols])
                prev_row = jnp.where(tile_in_seq > 0, edge[7:8, :], 0.0)
                next_row = jnp.where(tile_in_seq < tiles_per_seq - 1, edge[8:9, :], 0.0)
                xp = jnp.where(row == 0, prev_row, pltpu.roll(p, 1, 0))
                xn = jnp.where(row == tm - 1, next_row, pltpu.roll(p, tm - 1, 0))
                taps = [jnp.broadcast_to(cw_ref[r:r + 1, cols], (8, slab)) for r in range(3)]
                p = (xp.reshape(tiled) * taps[0] + p.reshape(tiled) * taps[1]
                     + xn.reshape(tiled) * taps[2]).reshape(tm, slab)
            y = _silu(p)
            for hd in range(slab // HEAD_DIM):
                head = y[:, hd * HEAD_DIM:(hd + 1) * HEAD_DIM]
                if kind in ('k', 'q'):
                    head = _l2norm(head)
                if kind == 'q':
                    head = head * (HEAD_DIM ** -0.5)
                o_ref[:, s * slab + hd * HEAD_DIM:s * slab + (hd + 1) * HEAD_DIM] = head.astype(o_ref.dtype)

    for jj, kind in enumerate(kinds):
        pl.when(j == jj)(functools.partial(epilogue, kind))


def _inproj_call(x2d, mod3, mod_row_of_tile, norm_g, w, w_ba, conv_w, a_lane, dt_lane, kinds, tm, tiles_per_seq,
                 name):
    m = x2d.shape[0]
    n = w.shape[1]
    tn = B_WIDTH
    assert n == tn * len(kinds) and tm % CHUNK == 0
    conv_groups = conv_w.shape[1] // tn
    last_halo = m // 8 - 1
    return pl.pallas_call(
        functools.partial(_inproj_kernel, tiles_per_seq=tiles_per_seq, kinds=kinds),
        grid=(m // tm, n // tn),
        in_specs=[pl.BlockSpec((tm, D_MODEL), lambda i, j: (i, 0)),
                  pl.BlockSpec((8, D_MODEL), lambda i, j: (jnp.maximum(i * (tm // 8) - 1, 0), 0)),
                  pl.BlockSpec((8, D_MODEL), lambda i, j: (jnp.minimum((i + 1) * (tm // 8), last_halo), 0)),
                  pl.BlockSpec((None, 1, 6 * D_MODEL), lambda i, j: (mod_row_of_tile(i), 0, 0)),
                  pl.BlockSpec((1, D_MODEL), lambda i, j: (0, 0)),
                  pl.BlockSpec((D_MODEL, tn), lambda i, j: (0, j)),
                  pl.BlockSpec((D_MODEL, LANES), lambda i, j: (0, 0)),
                  pl.BlockSpec((3, tn), lambda i, j: (0, jnp.minimum(j, conv_groups - 1))),
                  pl.BlockSpec((1, LANES), lambda i, j: (0, 0)),
                  pl.BlockSpec((1, LANES), lambda i, j: (0, 0))],
        out_specs=[pl.BlockSpec((tm, tn), lambda i, j: (i, j)),
                   pl.BlockSpec((tm, LANES), lambda i, j: (i, 0)),
                   pl.BlockSpec((LANES, tm), lambda i, j: (0, i))],
        out_shape=[jax.ShapeDtypeStruct((m, n), BF16), jax.ShapeDtypeStruct((m, LANES), F32),
                   jax.ShapeDtypeStruct((LANES, m), F32)],
        scratch_shapes=[pltpu.VMEM((tm, D_MODEL), BF16), pltpu.VMEM((16, D_MODEL), BF16)],
        compiler_params=pltpu.CompilerParams(
            dimension_semantics=("parallel", "arbitrary"), vmem_limit_bytes=VMEM_LIMIT),
        name=name,
    )(x2d, x2d, x2d, mod3, norm_g, w, w_ba, conv_w, a_lane, dt_lane)


def _run_interleaved(*gens):
    results = [None] * len(gens)
    live = list(range(len(gens)))
    while live:
        for idx in list(live):
            try:
                next(gens[idx])
            except StopIteration as stop:
                results[idx] = stop.value
                live.remove(idx)
    return results


TRI_INCL, TRI_STRICT, TRI_BLK16, TRI_EYE = 0, 2, 4, 5
MERGE_SIZES = (32, 64, 128)


def _mask_constants():
    ri = lax.broadcasted_iota(jnp.int32, (CHUNK, CHUNK), 0)
    ci = lax.broadcasted_iota(jnp.int32, (CHUNK, CHUNK), 1)

    def same_block(size):
        return (ri // size) == (ci // size)

    tri = jnp.stack([ri >= ci, ri <= ci, ri > ci, ri < ci, same_block(16), ri == ci]).astype(F32)
    off = jnp.stack([same_block(s) & ~same_block(s // 2) for s in MERGE_SIZES]).astype(BF16)
    return tri, off


def _tri_inverse_stages(n_mats, tri_ref, off_ref):
    n16 = [n * tri_ref[TRI_BLK16] for n in n_mats]
    ts = [tri_ref[TRI_EYE] + n for n in n16]
    pbs = [n.astype(BF16) for n in n16]
    pbs = [_dot(p, p).astype(BF16) for p in pbs]
    nbs = [n.astype(BF16) for n in n_mats]
    yield
    for _ in range(2):
        ts = [t + _dot(t.astype(BF16), p) for t, p in zip(ts, pbs)]
        pbs = [_dot(p, p).astype(BF16) for p in pbs]
        yield
    ts = [t + _dot(t.astype(BF16), p) for t, p in zip(ts, pbs)]
    yield
    tbs = [t.astype(BF16) for t in ts]
    for level in range(len(MERGE_SIZES)):
        xs = [_dot(nb * off_ref[level], tb).astype(BF16) for nb, tb in zip(nbs, tbs)]
        yield
        tbs = [tb + _dot(tb, x).astype(BF16) for tb, x in zip(tbs, xs)]
        yield
    return tbs


def _chunk_start(c):
    return c * CHUNK if isinstance(c, int) else pl.multiple_of(c * CHUNK, CHUNK)


def _prep_stages(chunks, h, k_ref, v_ref, q_ref, gcol_ref, grow_ref, tri_ref, off_ref, early, late):
    lane = lax.broadcasted_iota(jnp.int32, (CHUNK, LANES), 1)
    head_row = lax.broadcasted_iota(jnp.int32, (N_HEADS, CHUNK), 0)

    acts, prods = [], []
    for c in chunks:
        rows = pl.ds(_chunk_start(c), CHUNK)
        kb16 = k_ref[rows, :]
        k = kb16.astype(F32)
        kt16 = k.T.astype(BF16)
        v = v_ref[rows, :].astype(F32)
        if q_ref is not None:
            qb16 = q_ref[rows, :]
            aa = _dot(jnp.concatenate([kb16, qb16], axis=0), kt16)
            acts.append((k, v, qb16.astype(F32)))
            prods.append((aa[:CHUNK], aa[CHUNK:]))
        else:
            acts.append((k, v, None))
            prods.append((_dot(kb16, kt16), None))
    yield

    neg_ms, rhss = [], []
    for i, (c, (k, v, q), (a_kk, a_qk)) in enumerate(zip(chunks, acts, prods)):
        r0 = _chunk_start(c)
        gates = gcol_ref[pl.ds(r0, CHUNK), :]
        for d in range(2):
            beta = jnp.sum(jnp.where(lane == d * N_HEADS + h, gates, 0.0), axis=1, keepdims=True)
            gc = jnp.sum(jnp.where(lane == (2 + d) * N_HEADS + h, gates, 0.0), axis=1, keepdims=True)
            all_heads = grow_ref[(2 + d) * N_HEADS:(3 + d) * N_HEADS, pl.ds(r0, CHUNK)]
            gc_row = jnp.sum(jnp.where(head_row == h, all_heads, 0.0), axis=0, keepdims=True)
            decay_ij = jnp.exp(jnp.minimum(gc - gc_row, 0.0))
            neg_ms.append(a_kk * (-beta) * (decay_ij * tri_ref[TRI_STRICT + d]))
            eg = jnp.exp(gc)
            rhss.append(jnp.concatenate([k * (beta * eg), v * beta], axis=1).astype(BF16))
            g_end = gc[CHUNK - 1:CHUNK, :] if d == 0 else gc[0:1, :]
            kd = k * jnp.exp(g_end - gc)
            decay = jnp.broadcast_to(jnp.exp(g_end), (8, LANES))
            if q is not None:
                early(i, d, kd, decay, q * eg, a_qk * (decay_ij * tri_ref[TRI_INCL + d]))
            else:
                early(i, d, kd, decay, None, None)

    t_invs = yield from _tri_inverse_stages(neg_ms, tri_ref, off_ref)
    wus = [_dot(t, rhs) for t, rhs in zip(t_invs, rhss)]
    yield
    for n, wu in enumerate(wus):
        late(n // 2, n % 2, wu[:, :HEAD_DIM], wu[:, HEAD_DIM:])


def _delta_kernel(pk_ref, pv_ref, pq_ref, gcol_ref, grow_ref, ck_ref, cv_ref, cgcol_ref, cgrow_ref,
                  sz_ref, onorm_ref, tri_ref, off_ref, og_ref,
                  wq_sc, kq_sc, u_sc, ge_sc, s0_sc, o_sc, *, n_chunks, n_ctx_chunks, n_heads_total):
    t = pl.program_id(0)
    h = jnp.minimum(t, n_heads_total - 1) % N_HEADS
    slot_p = t % 2
    slot_s = 1 - slot_p

    @pl.when(t == 0)
    def _():
        wq_sc[1] = jnp.zeros(wq_sc.shape[1:], wq_sc.dtype)
        kq_sc[1] = jnp.zeros(kq_sc.shape[1:], kq_sc.dtype)
        u_sc[1] = jnp.zeros(u_sc.shape[1:], u_sc.dtype)
        ge_sc[1] = jnp.zeros(ge_sc.shape[1:], ge_sc.dtype)
        s0_sc[1] = jnp.zeros(s0_sc.shape[1:], s0_sc.dtype)

    def context_states():
        kept = {}

        def early(c, d, kd, decay, qg, qk):
            kept[c, d] = [kd.T.astype(BF16), decay]

        def late(c, d, w, u):
            kept[c, d] += [w.astype(BF16), u]

        yield from _prep_stages(list(range(n_ctx_chunks)), h, ck_ref, cv_ref, None, cgcol_ref, cgrow_ref,
                                tri_ref, off_ref, early, late)
        states = [jnp.zeros((HEAD_DIM, HEAD_DIM), F32)] * 2
        for c in range(n_ctx_chunks):
            steps = [kept[c, 0], kept[n_ctx_chunks - 1 - c, 1]]
            ws = [_dot(w, s.astype(BF16)) for (_, _, w, _), s in zip(steps, states)]
            yield
            kv = [_dot(kd_t, (u - x).astype(BF16)) for (kd_t, _, _, u), x in zip(steps, ws)]
            yield
            states = [s * decay[0:1, :] + x for s, (_, decay, _, _), x in zip(states, steps, kv)]
        for d in range(2):
            s0_sc[slot_p, d] = states[d]

    def prepare(i):
        chunks = [i * GROUP + g for g in range(GROUP)]

        def early(n, d, kd, decay, qg, qk):
            c = chunks[n]
            kq_sc[slot_p, d, c, 0:CHUNK, :] = kd.T.astype(BF16)
            kq_sc[slot_p, d, c, CHUNK:2 * CHUNK, :] = qk.astype(BF16)
            wq_sc[slot_p, d, c, CHUNK:2 * CHUNK, :] = qg.astype(BF16)
            ge_sc[slot_p, d, c] = decay

        def late(n, d, w, u):
            c = chunks[n]
            wq_sc[slot_p, d, c, 0:CHUNK, :] = w.astype(BF16)
            u_sc[slot_p, d, pl.ds(_chunk_start(c), CHUNK), :] = u

        yield from _prep_stages(chunks, h, pk_ref, pv_ref, pq_ref, gcol_ref, grow_ref, tri_ref, off_ref,
                                early, late)

    def finish(c, o):
        r0 = _chunk_start(c)
        y = o * lax.rsqrt(jnp.mean(o * o, axis=-1, keepdims=True) + NORM_EPS) * onorm_ref[...]
        og_ref[pl.ds(r0, CHUNK), :] = (y * sz_ref[pl.ds(r0, CHUNK), :].astype(F32)).astype(og_ref.dtype)

    def scan(i, states, second_half):
        for g in range(GROUP):
            j = i * GROUP + g
            chunks = (j, n_chunks - 1 - j)
            ws = [_dot(wq_sc[slot_s, d, c], s.astype(BF16)) for d, (c, s) in enumerate(zip(chunks, states))]
            yield
            v_new = [u_sc[slot_s, d, pl.ds(_chunk_start(c), CHUNK), :] - x[:CHUNK]
                     for d, (c, x) in enumerate(zip(chunks, ws))]
            kv = [_dot(kq_sc[slot_s, d, c], vn.astype(BF16)) for d, (c, vn) in enumerate(zip(chunks, v_new))]
            yield
            states = [s * ge_sc[slot_s, d, c][0:1, :] + x[:CHUNK]
                      for d, (c, s, x) in enumerate(zip(chunks, states, kv))]
            for c, x, y in zip(chunks, ws, kv):
                o = x[CHUNK:] + y[CHUNK:]
                rows = pl.ds(_chunk_start(c), CHUNK)
                if second_half:
                    finish(c, o_sc[rows, :] + o)
                else:
                    o_sc[rows, :] = o
        return tuple(states)

    def body(second_half, i, states, *also):
        return _run_interleaved(scan(i, states, second_half), prepare(i), *also)[0]

    n_iter = n_chunks // GROUP
    states = (s0_sc[slot_s, 0], s0_sc[slot_s, 1])
    states = body(False, 0, states, context_states())
    states = lax.fori_loop(1, n_iter // 2, functools.partial(body, False), states)
    lax.fori_loop(n_iter // 2, n_iter, functools.partial(body, True), states)


def _delta_call(p3, gcol, grow, pc3, cgcol, cgrow, onorm_g):
    bsz, length, _ = p3.shape
    ctx_len = pc3.shape[1]
    n_chunks = length // CHUNK
    n_ctx_chunks = ctx_len // CHUNK
    n_heads_total = bsz * N_HEADS
    assert n_chunks % (2 * GROUP) == 0
    tri_masks, off_masks = _mask_constants()

    def prepared(t):
        t = jnp.minimum(t, n_heads_total - 1)
        return t // N_HEADS, t % N_HEADS

    def scanned(t):
        t = jnp.maximum(t - 1, 0)
        return t // N_HEADS, t % N_HEADS

    def head_cols(rows, first_block, which):
        def index(t):
            b, h = which(t)
            return b, 0, first_block + h
        return pl.BlockSpec((None, rows, LANES), index)

    def gates_col(rows):
        return pl.BlockSpec((None, rows, LANES), lambda t: (prepared(t)[0], 0, 0), pipeline_mode=pl.Buffered(1))

    def gates_row(rows):
        return pl.BlockSpec((LANES, rows), lambda t: (0, prepared(t)[0]), pipeline_mode=pl.Buffered(1))

    return pl.pallas_call(
        functools.partial(_delta_kernel, n_chunks=n_chunks, n_ctx_chunks=n_ctx_chunks,
                          n_heads_total=n_heads_total),
        grid=(n_heads_total + 1,),
        in_specs=[head_cols(length, 0, prepared), head_cols(length, N_HEADS, prepared),
                  head_cols(length, 2 * N_HEADS, prepared),
                  gates_col(length), gates_row(length),
                  head_cols(ctx_len, 0, prepared), head_cols(ctx_len, N_HEADS, prepared),
                  gates_col(ctx_len), gates_row(ctx_len),
                  head_cols(length, 3 * N_HEADS, scanned),
                  pl.BlockSpec((1, LANES), lambda t: (0, 0)),
                  pl.BlockSpec(tri_masks.shape, lambda t: (0, 0, 0), pipeline_mode=pl.Buffered(1)),
                  pl.BlockSpec(off_masks.shape, lambda t: (0, 0, 0), pipeline_mode=pl.Buffered(1))],
        out_specs=head_cols(length, 0, scanned),
        out_shape=jax.ShapeDtypeStruct((bsz, length, B_WIDTH), BF16),
        scratch_shapes=[pltpu.VMEM((2, 2, n_chunks, 2 * CHUNK, LANES), BF16),
                        pltpu.VMEM((2, 2, n_chunks, 2 * CHUNK, LANES), BF16),
                        pltpu.VMEM((2, 2, length, LANES), F32),
                        pltpu.VMEM((2, 2, n_chunks, 8, LANES), F32),
                        pltpu.VMEM((2, 2, HEAD_DIM, HEAD_DIM), F32),
                        pltpu.VMEM((length, LANES), F32)],
        compiler_params=pltpu.CompilerParams(
            dimension_semantics=("arbitrary",), vmem_limit_bytes=62 * 1024 * 1024),
        name="delta",
    )(p3, p3, p3, gcol, grow, pc3, pc3, cgcol, cgrow, p3, onorm_g, tri_masks, off_masks)


def _mix_kernel(x_ref, og_ref, mod_ref, g1_ref, wuv_ref, wg_ref, lng_ref, lnb_ref, ws_ref, bs_ref,
                wa_ref, wb_ref, wo_ref, o_ref, ua_ref, *, n_sub):
    x = x_ref[...]
    h = _norm_mod(x, g1_ref[...], mod_ref[:, 0:D_MODEL], mod_ref[:, D_MODEL:2 * D_MODEL]).astype(BF16)
    uv = _dot(h, wuv_ref[...])
    v = jax.nn.gelu(uv[:, D_MODEL:])
    vc = v - jnp.mean(v, axis=-1, keepdims=True)
    vn = vc * lax.rsqrt(jnp.mean(vc * vc, axis=-1, keepdims=True) + NORM_EPS) * lng_ref[...] + lnb_ref[...]
    vb = vn.astype(BF16)
    for n in range(n_sub):
        rows = slice(n * A_CHUNK, (n + 1) * A_CHUNK)
        for g in range(A_GROUPS):
            cols = slice(g * LANES, (g + 1) * LANES)
            s = _dot(ws_ref[g], vb[rows, cols]) + bs_ref[:, cols]
            ua_ref[rows, cols] = (jax.nn.gelu(uv[rows, cols]) * s).astype(BF16)
    y_a = _dot(ua_ref[...], wa_ref[...])
    y_b = _dot(og_ref[...], wb_ref[...])
    gates = jax.nn.sigmoid(_dot(h, wg_ref[...]))
    t = gates[:, :D_MODEL] * y_a + gates[:, D_MODEL:] * y_b
    y = _dot(t.astype(BF16), wo_ref[...])
    o_ref[...] = x + mod_ref[:, 2 * D_MODEL:3 * D_MODEL] * y


def _resident(shape):
    return pl.BlockSpec(shape, lambda i: (0,) * len(shape), pipeline_mode=pl.Buffered(1))


def _mix_call(x2d, og, mod3, g1, wuv16, wg16, ln_g, ln_b, ws16, bs_cols, wa16, wb16, wo16, tm, tiles_per_batch):
    m = x2d.shape[0]
    rows = pl.BlockSpec((tm, D_MODEL), lambda i: (i, 0))
    return pl.pallas_call(
        functools.partial(_mix_kernel, n_sub=tm // A_CHUNK),
        grid=(m // tm,),
        in_specs=[rows, rows,
                  pl.BlockSpec((None, 1, 6 * D_MODEL), lambda i: (i // tiles_per_batch, 0, 0)),
                  _resident((1, D_MODEL)),
                  _resident((D_MODEL, 2 * D_MODEL)), _resident((D_MODEL, 2 * D_MODEL)),
                  _resident((1, D_MODEL)), _resident((1, D_MODEL)),
                  _resident((A_GROUPS, A_CHUNK, A_CHUNK)), _resident((A_CHUNK, D_MODEL)),
                  _resident((D_MODEL, D_MODEL)), _resident((D_MODEL, D_MODEL)), _resident((D_MODEL, D_MODEL))],
        out_specs=rows,
        out_shape=jax.ShapeDtypeStruct((m, D_MODEL), F32),
        scratch_shapes=[pltpu.VMEM((tm, D_MODEL), BF16)],
        compiler_params=pltpu.CompilerParams(
            dimension_semantics=("parallel",), vmem_limit_bytes=VMEM_LIMIT),
        name="mix",
    )(x2d, og, mod3, g1, wuv16, wg16, ln_g, ln_b, ws16, bs_cols, wa16, wb16, wo16)


def _ffn_kernel(x_ref, mod_ref, g2_ref, wup_ref, cw_ref, cb_ref, wd_ref, gf_ref, o_ref, *, n_split):
    tm = x_ref.shape[0]
    tf = D_FF // n_split
    x = x_ref[...]
    h = _norm_mod(x, g2_ref[...], mod_ref[:, 3 * D_MODEL:4 * D_MODEL],
                  mod_ref[:, 4 * D_MODEL:5 * D_MODEL]).astype(BF16)
    col_in_row = lax.broadcasted_iota(jnp.int32, (tm, tf), 0) & (GRID_W - 1)
    acc = None
    for j in range(n_split):
        cols = slice(j * tf, (j + 1) * tf)
        a = _dot(h, wup_ref[:, cols])
        b = _dot(h, wup_ref[:, D_FF + j * tf:D_FF + (j + 1) * tf])
        a_prev = jnp.where(col_in_row == 0, 0.0, pltpu.roll(a, 1, 0))
        a_next = jnp.where(col_in_row == GRID_W - 1, 0.0, pltpu.roll(a, tm - 1, 0))
        ac = a_prev * cw_ref[0:1, cols] + a * cw_ref[1:2, cols] + a_next * cw_ref[2:3, cols] + cb_ref[:, cols]
        part = _dot((jax.nn.gelu(ac) * b).astype(BF16), wd_ref[cols, :])
        acc = part if acc is None else acc + part
    x2 = x + mod_ref[:, 5 * D_MODEL:6 * D_MODEL] * acc
    ms = jnp.mean(x2 * x2, axis=-1, keepdims=True)
    o_ref[...] = x2 * lax.rsqrt(ms + NORM_EPS) * gf_ref[...]


def _ffn_call(x1, mod3, norm2_g, w_up16, conv_w, conv_b, w_down16, final_g, tm, n_split, tiles_per_batch):
    m = x1.shape[0]
    assert D_FF % (n_split * LANES) == 0 and tm % GRID_W == 0
    return pl.pallas_call(
        functools.partial(_ffn_kernel, n_split=n_split),
        grid=(m // tm,),
        in_specs=[pl.BlockSpec((tm, D_MODEL), lambda i: (i, 0)),
                  pl.BlockSpec((None, 1, 6 * D_MODEL), lambda i: (i // tiles_per_batch, 0, 0)),
                  _resident((1, D_MODEL)),
                  _resident((D_MODEL, 2 * D_FF)),
                  _resident((3, D_FF)), _resident((1, D_FF)),
                  _resident((D_FF, D_MODEL)),
                  _resident((1, D_MODEL))],
        out_specs=pl.BlockSpec((tm, D_MODEL), lambda i: (i, 0)),
        out_shape=jax.ShapeDtypeStruct((m, D_MODEL), F32),
        compiler_params=pltpu.CompilerParams(
            dimension_semantics=("parallel",), vmem_limit_bytes=VMEM_LIMIT),
        name="ffn",
    )(x1, mod3, norm2_g, w_up16, conv_w, conv_b, w_down16, final_g)


def kernel(x, c, ctx, c_ctx, w_mod, b_mod, norm1_g, w_in, conv_qkv, a_log, dt_bias, onorm_g, w_proj_b,
           a_ln_g, a_ln_b, a_ws, a_bs, w_proj_a, w_out, norm2_g, w_up, ffn_conv_w, ffn_conv_b, w_down,
           final_g):
    bsz, length, _ = x.shape
    ctx_len = ctx.shape[1]
    assert w_mod.shape[0] == 1 and bsz <= 7
    assert length % 1024 == 0 and ctx_len % CHUNK == 0

    w = w_in[0]
    w_kvqz = jnp.concatenate([w[:, :OFF_BA], w[:, OFF_Q:OFF_U]], axis=1).astype(BF16)
    w_kv = w[:, :OFF_BA].astype(BF16)
    w_ba = jnp.pad(w[:, OFF_BA:OFF_Q], ((0, 0), (0, LANES - 4 * N_HEADS))).astype(BF16)
    w_uv = w[:, OFF_U:OFF_GA].astype(BF16)
    w_gates = w[:, OFF_GA:].astype(BF16)
    lane_pad = (2 * N_HEADS, LANES - 4 * N_HEADS)
    a_lane = jnp.pad(jnp.exp(a_log[0].astype(F32)).reshape(-1), lane_pad).reshape(1, LANES)
    dt_lane = jnp.pad(dt_bias[0].astype(F32).reshape(-1), lane_pad).reshape(1, LANES)
    bs_cols = jnp.repeat(a_bs[0].T, LANES, axis=1)

    cond8 = jnp.concatenate([c, c_ctx[None, :], jnp.zeros((7 - bsz, D_MODEL), F32)], axis=0)
    mod3 = _mod_call(cond8, w_mod[0], b_mod[0]).reshape(8, 1, 6 * D_MODEL)

    x2d = x.reshape(bsz * length, D_MODEL)
    g1 = norm1_g[0].reshape(1, D_MODEL)
    tm_in = 1024
    tiles_per_seq = length // tm_in
    p2d, gcol, grow = _inproj_call(x2d, mod3, lambda i: i // tiles_per_seq, g1, w_kvqz, w_ba, conv_qkv[0],
                                   a_lane, dt_lane, ('k', 'v', 'q', 'z'), tm_in, tiles_per_seq, "in_proj")
    pc2d, cgcol, cgrow = _inproj_call(ctx.reshape(bsz * ctx_len, D_MODEL), mod3, lambda i: bsz, g1, w_kv, w_ba,
                                      conv_qkv[0], a_lane, dt_lane, ('k', 'v'), ctx_len, 1, "in_proj_ctx")
    p3 = p2d.reshape(bsz, length, 4 * B_WIDTH)
    pc3 = pc2d.reshape(bsz, ctx_len, 2 * B_WIDTH)
    og = _delta_call(p3, gcol.reshape(bsz, length, LANES), grow, pc3, cgcol.reshape(bsz, ctx_len, LANES), cgrow,
                     onorm_g[0].reshape(1, HEAD_DIM))

    tm = 512
    tiles_per_batch = length // tm
    x1 = _mix_call(x2d, og.reshape(bsz * length, B_WIDTH), mod3, g1, w_uv, w_gates,
                   a_ln_g[0].reshape(1, -1), a_ln_b[0].reshape(1, -1), a_ws[0].astype(BF16), bs_cols,
                   w_proj_a[0].astype(BF16), w_proj_b[0].astype(BF16), w_out[0].astype(BF16), tm, tiles_per_batch)
    out = _ffn_call(x1, mod3, norm2_g[0].reshape(1, -1), w_up[0].astype(BF16), ffn_conv_w[0],
                    ffn_conv_b[0].reshape(1, -1), w_down[0].astype(BF16), final_g.reshape(1, -1),
                    tm, 1, tiles_per_batch)
    return out.reshape(bsz, length, D_MODEL)
```

```python
import functools

import jax
import jax.numpy as jnp
from jax import lax
from jax.experimental import pallas as pl
from jax.experimental.pallas import tpu as pltpu

F32 = jnp.float32
BF16 = jnp.bfloat16

D_MODEL = 1024
GRID_W = 64
NORM_EPS = 1e-6
N_HEADS = 8
HEAD_DIM = 128
B_WIDTH = N_HEADS * HEAD_DIM
A_GROUPS = 8
A_CHUNK = 128
D_FF = 2816
OFF_BA = 2 * B_WIDTH
OFF_Q = OFF_BA + 4 * N_HEADS
OFF_U = OFF_Q + 2 * B_WIDTH
OFF_GA = OFF_U + 2 * D_MODEL
LANES = 128
MXU_COLS = 256
CHUNK = 128
GROUP = 8

VMEM_LIMIT = 56 * 1024 * 1024


def _silu(x):
    return x * jax.nn.sigmoid(x)


def _dot(a, b):
    return jnp.dot(a, b, preferred_element_type=F32)


def _dot_nt(a, b):
    return lax.dot_general(a, b, (((1,), (1,)), ((), ())), preferred_element_type=F32)


def _mod_kernel(c_ref, w_ref, b_ref, o_ref):
    cond = _silu(c_ref[...])
    o_ref[...] = jnp.dot(cond, w_ref[...], preferred_element_type=F32,
                         precision=lax.Precision.HIGHEST) + b_ref[...]


def _mod_call(cond8, w_mod, b_mod):
    n = w_mod.shape[1]
    tn = 1536
    return pl.pallas_call(
        _mod_kernel,
        grid=(n // tn,),
        in_specs=[pl.BlockSpec((8, D_MODEL), lambda j: (0, 0)),
                  pl.BlockSpec((D_MODEL, tn), lambda j: (0, j)),
                  pl.BlockSpec((1, tn), lambda j: (0, j))],
        out_specs=pl.BlockSpec((8, tn), lambda j: (0, j)),
        out_shape=jax.ShapeDtypeStruct((8, n), F32),
        name="mod",
    )(cond8, w_mod, b_mod.reshape(1, n))


def _norm_mod(x, g, shift, scale):
    ms = jnp.mean(x * x, axis=-1, keepdims=True)
    y = x * lax.rsqrt(ms + NORM_EPS) * g
    return y * (1.0 + scale) + shift


def _l2norm(t):
    return t * lax.rsqrt(jnp.sum(t * t, axis=-1, keepdims=True) + NORM_EPS)


def _gate_columns(ba, a_lane, dt_lane):
    n_sub = ba.shape[0] // CHUNK
    lane = lax.broadcasted_iota(jnp.int32, ba.shape, 1)
    ri = lax.broadcasted_iota(jnp.int32, (CHUNK, CHUNK), 0)
    ci = lax.broadcasted_iota(jnp.int32, (CHUNK, CHUNK), 1)
    tri_lo, tri_up = (ri >= ci).astype(F32), (ri <= ci).astype(F32)
    beta = jax.nn.sigmoid(ba)
    y = ba + dt_lane
    softplus = jnp.maximum(y, 0.0) + jnp.log1p(jnp.exp(-jnp.abs(y)))
    g = jnp.where((lane >= 2 * N_HEADS) & (lane < 4 * N_HEADS), -a_lane * softplus, 0.0)
    part = 4 * N_HEADS
    hi = g.astype(BF16).astype(F32)
    mid = (g - hi).astype(BF16).astype(F32)
    lo = (g - hi - mid).astype(BF16).astype(F32)
    packed = (hi + pltpu.roll(mid, part, 1) + pltpu.roll(lo, 2 * part, 1)).astype(BF16)

    def unpack(r):
        return r + pltpu.roll(r, LANES - part, 1) + pltpu.roll(r, LANES - 2 * part, 1)

    tri_lo, tri_up = tri_lo.astype(BF16), tri_up.astype(BF16)
    parts = [packed[s * CHUNK:(s + 1) * CHUNK, :] for s in range(n_sub)]
    pre = [unpack(_dot(tri_lo, x)) for x in parts]
    suf = [unpack(_dot(tri_up, x)) for x in parts]
    lane = lax.broadcasted_iota(jnp.int32, (CHUNK, LANES), 1)
    return [jnp.where(lane < 2 * N_HEADS, beta[s * CHUNK:(s + 1) * CHUNK, :],
                      jnp.where(lane < 3 * N_HEADS, pre[s], suf[s])) for s in range(n_sub)]


def _resident(shape):
    return pl.BlockSpec(shape, lambda i: (0,) * len(shape), pipeline_mode=pl.Buffered(1))


def _slab_order(kinds, slabs_per_group):
    heavy = [(g, s) for g, kind in enumerate(kinds) if kind != 'z' for s in range(slabs_per_group)]
    light = [(g, s) for g, kind in enumerate(kinds) if kind == 'z' for s in range(slabs_per_group)]
    every = max(len(heavy) // max(len(light), 1), 1)
    order = []
    for n, item in enumerate(heavy):
        order.append(item)
        if light and (n + 1) % every == 0:
            order.append(light.pop(0))
    return order + light


def _inproj_kernel(x_ref, xb_ref, xa_ref, mod_ref, g_ref, w_ref, wba_ref, cw_ref, a_ref, dt_ref,
                   o_ref, gcol_ref, grow_ref, h_ref, halo_ref, *, tiles_per_seq, kinds):
    i = pl.program_id(0)
    tm = o_ref.shape[0]
    shift, scale = mod_ref[:, 0:D_MODEL], mod_ref[:, D_MODEL:2 * D_MODEL]
    h_ref[...] = _norm_mod(x_ref[...], g_ref[...], shift, scale).astype(BF16)
    halo = jnp.concatenate([xb_ref[...], xa_ref[...]], axis=0)
    halo_ref[...] = _norm_mod(halo, g_ref[...], shift, scale).astype(BF16)
    ba = _dot(h_ref[...], wba_ref[...])
    for s, col in enumerate(_gate_columns(ba, a_ref[...], dt_ref[...])):
        rows = slice(s * CHUNK, (s + 1) * CHUNK)
        gcol_ref[rows, :] = col
        grow_ref[:, rows] = col.T

    tile_in_seq = i % tiles_per_seq
    slab = MXU_COLS
    row8 = lax.broadcasted_iota(jnp.int32, (8, slab), 0)
    tiled = (tm // 8, 8, slab)
    for g, s in _slab_order(kinds, B_WIDTH // slab):
        kind = kinds[g]
        col0 = g * B_WIDTH + s * slab
        cols = slice(col0, col0 + slab)
        p = _dot(h_ref[...], w_ref[:, cols])
        if kind != 'z':
            edge = _dot(halo_ref[...], w_ref[:, cols])
            prev_row = jnp.where(tile_in_seq > 0, edge[7:8, :], 0.0)
            next_row = jnp.where(tile_in_seq < tiles_per_seq - 1, edge[8:9, :], 0.0)
            xp = pltpu.roll(p, 1, 0)
            xp = jnp.concatenate([jnp.where(row8 == 0, prev_row, xp[0:8]), xp[8:]], axis=0)
            xn = pltpu.roll(p, tm - 1, 0)
            xn = jnp.concatenate([xn[:tm - 8], jnp.where(row8 == 7, next_row, xn[tm - 8:])], axis=0)
            taps = [jnp.broadcast_to(cw_ref[r:r + 1, cols], (8, slab)) for r in range(3)]
            p = (xp.reshape(tiled) * taps[0] + p.reshape(tiled) * taps[1]
                 + xn.reshape(tiled) * taps[2]).reshape(tm, slab)
        y = _silu(p)
        for hd in range(slab // HEAD_DIM):
            head = y[:, hd * HEAD_DIM:(hd + 1) * HEAD_DIM]
            if kind in ('k', 'q'):
                head = _l2norm(head)
            if kind == 'q':
                head = head * (HEAD_DIM ** -0.5)
            o_ref[:, col0 + hd * HEAD_DIM:col0 + (hd + 1) * HEAD_DIM] = head.astype(o_ref.dtype)


def _inproj_call(x2d, mod3, mod_row_of_tile, norm_g, w, w_ba, conv_w, a_lane, dt_lane, kinds, tm, tiles_per_seq,
                 name):
    m = x2d.shape[0]
    n = w.shape[1]
    assert n == B_WIDTH * len(kinds) and tm % CHUNK == 0
    assert all(kind != 'z' for kind in kinds[:conv_w.shape[1] // B_WIDTH]) and 'z' not in kinds[:-1]
    last_halo = m // 8 - 1
    return pl.pallas_call(
        functools.partial(_inproj_kernel, tiles_per_seq=tiles_per_seq, kinds=kinds),
        grid=(m // tm,),
        in_specs=[pl.BlockSpec((tm, D_MODEL), lambda i: (i, 0)),
                  pl.BlockSpec((8, D_MODEL), lambda i: (jnp.maximum(i * (tm // 8) - 1, 0), 0)),
                  pl.BlockSpec((8, D_MODEL), lambda i: (jnp.minimum((i + 1) * (tm // 8), last_halo), 0)),
                  pl.BlockSpec((None, 1, 6 * D_MODEL), lambda i: (mod_row_of_tile(i), 0, 0)),
                  _resident((1, D_MODEL)),
                  _resident((D_MODEL, n)),
                  _resident((D_MODEL, LANES)),
                  _resident(conv_w.shape),
                  _resident((1, LANES)), _resident((1, LANES))],
        out_specs=[pl.BlockSpec((tm, n), lambda i: (i, 0)),
                   pl.BlockSpec((tm, LANES), lambda i: (i, 0)),
                   pl.BlockSpec((LANES, tm), lambda i: (0, i))],
        out_shape=[jax.ShapeDtypeStruct((m, n), BF16), jax.ShapeDtypeStruct((m, LANES), F32),
                   jax.ShapeDtypeStruct((LANES, m), F32)],
        scratch_shapes=[pltpu.VMEM((tm, D_MODEL), BF16), pltpu.VMEM((16, D_MODEL), BF16)],
        compiler_params=pltpu.CompilerParams(
            dimension_semantics=("parallel",), vmem_limit_bytes=VMEM_LIMIT),
        name=name,
    )(x2d, x2d, x2d, mod3, norm_g, w, w_ba, conv_w, a_lane, dt_lane)


def _run_interleaved(*gens):
    results = [None] * len(gens)
    live = list(range(len(gens)))
    while live:
        for idx in list(live):
            try:
                next(gens[idx])
            except StopIteration as stop:
                results[idx] = stop.value
                live.remove(idx)
    return results


TRI_INCL, TRI_STRICT, TRI_BLK16, TRI_EYE = 0, 2, 4, 5
MERGE_SIZES = (32, 64, 128)


def _mask_constants():
    ri = lax.broadcasted_iota(jnp.int32, (CHUNK, CHUNK), 0)
    ci = lax.broadcasted_iota(jnp.int32, (CHUNK, CHUNK), 1)

    def same_block(size):
        return (ri // size) == (ci // size)

    tri = jnp.stack([ri >= ci, ri <= ci, ri > ci, ri < ci, same_block(16), ri == ci]).astype(F32)
    off = jnp.stack([same_block(s) & ~same_block(s // 2) for s in MERGE_SIZES]).astype(BF16)
    return tri, off


def _tri_inverse_stages(n_mats, tri_ref, off_ref):
    n16 = [n * tri_ref[TRI_BLK16] for n in n_mats]
    ts = [tri_ref[TRI_EYE] + n for n in n16]
    pbs = [n.astype(BF16) for n in n16]
    pbs = [_dot(p, p).astype(BF16) for p in pbs]
    nbs = [n.astype(BF16) for n in n_mats]
    yield
    for _ in range(2):
        ts = [t + _dot(t.astype(BF16), p) for t, p in zip(ts, pbs)]
        pbs = [_dot(p, p).astype(BF16) for p in pbs]
        yield
    ts = [t + _dot(t.astype(BF16), p) for t, p in zip(ts, pbs)]
    yield
    tbs = [t.astype(BF16) for t in ts]
    for level in range(len(MERGE_SIZES)):
        xs = [_dot(nb * off_ref[level], tb).astype(BF16) for nb, tb in zip(nbs, tbs)]
        yield
        tbs = [tb + _dot(tb, x).astype(BF16) for tb, x in zip(tbs, xs)]
        yield
    return tbs


def _chunk_start(c):
    return c * CHUNK if isinstance(c, int) else pl.multiple_of(c * CHUNK, CHUNK)


def _prep_stages(chunks, h, k_ref, v_ref, q_ref, gcol_ref, grow_ref, tri_ref, off_ref, early, late):
    lane = lax.broadcasted_iota(jnp.int32, (CHUNK, LANES), 1)
    head_row = lax.broadcasted_iota(jnp.int32, (N_HEADS, CHUNK), 0)

    acts, prods = [], []
    for c in chunks:
        rows = pl.ds(_chunk_start(c), CHUNK)
        kb16 = k_ref[rows, :]
        k = kb16.astype(F32)
        kt16 = k.T.astype(BF16)
        v = v_ref[rows, :].astype(F32)
        if q_ref is not None:
            qb16 = q_ref[rows, :]
            aa = _dot(jnp.concatenate([kb16, qb16], axis=0), kt16)
            acts.append((k, v, qb16.astype(F32)))
            prods.append((aa[:CHUNK], aa[CHUNK:]))
        else:
            acts.append((k, v, None))
            prods.append((_dot(kb16, kt16), None))
    yield

    neg_ms, rhss = [], []
    for i, (c, (k, v, q), (a_kk, a_qk)) in enumerate(zip(chunks, acts, prods)):
        r0 = _chunk_start(c)
        gates = gcol_ref[pl.ds(r0, CHUNK), :]
        for d in range(2):
            beta = jnp.sum(jnp.where(lane == d * N_HEADS + h, gates, 0.0), axis=1, keepdims=True)
            gc = jnp.sum(jnp.where(lane == (2 + d) * N_HEADS + h, gates, 0.0), axis=1, keepdims=True)
            all_heads = grow_ref[(2 + d) * N_HEADS:(3 + d) * N_HEADS, pl.ds(r0, CHUNK)]
            gc_row = jnp.sum(jnp.where(head_row == h, all_heads, 0.0), axis=0, keepdims=True)
            decay_ij = jnp.exp(jnp.minimum(gc - gc_row, 0.0))
            neg_ms.append(a_kk * (-beta) * (decay_ij * tri_ref[TRI_STRICT + d]))
            eg = jnp.exp(gc)
            rhss.append(jnp.concatenate([k * (beta * eg), v * beta], axis=1).astype(BF16))
            g_end = gc[CHUNK - 1:CHUNK, :] if d == 0 else gc[0:1, :]
            kd = k * jnp.exp(g_end - gc)
            decay = jnp.broadcast_to(jnp.exp(g_end), (8, LANES))
            if q is not None:
                early(i, d, kd, decay, q * eg, a_qk * (decay_ij * tri_ref[TRI_INCL + d]))
            else:
                early(i, d, kd, decay, None, None)

    t_invs = yield from _tri_inverse_stages(neg_ms, tri_ref, off_ref)
    wus = [_dot(t, rhs) for t, rhs in zip(t_invs, rhss)]
    yield
    for n, wu in enumerate(wus):
        late(n // 2, n % 2, wu[:, :HEAD_DIM], wu[:, HEAD_DIM:])


def _delta_kernel(pk_ref, pv_ref, pq_ref, gcol_ref, grow_ref, ck_ref, cv_ref, cgcol_ref, cgrow_ref,
                  sz_ref, onorm_ref, tri_ref, off_ref, og_ref,
                  wq_sc, kq_sc, u_sc, ge_sc, s0_sc, o_sc, *, n_chunks, n_ctx_chunks, n_heads_total):
    t = pl.program_id(0)
    h = jnp.minimum(t, n_heads_total - 1) % N_HEADS
    slot_p = t % 2
    slot_s = 1 - slot_p

    @pl.when(t == 0)
    def _():
        wq_sc[1] = jnp.zeros(wq_sc.shape[1:], wq_sc.dtype)
        kq_sc[1] = jnp.zeros(kq_sc.shape[1:], kq_sc.dtype)
        u_sc[1] = jnp.zeros(u_sc.shape[1:], u_sc.dtype)
        ge_sc[1] = jnp.zeros(ge_sc.shape[1:], ge_sc.dtype)
        s0_sc[1] = jnp.zeros(s0_sc.shape[1:], s0_sc.dtype)

    def context_states():
        kept = {}

        def early(c, d, kd, decay, qg, qk):
            kept[c, d] = [kd.T.astype(BF16), decay]

        def late(c, d, w, u):
            kept[c, d] += [w.astype(BF16), u]

        yield from _prep_stages(list(range(n_ctx_chunks)), h, ck_ref, cv_ref, None, cgcol_ref, cgrow_ref,
                                tri_ref, off_ref, early, late)
        states = [jnp.zeros((HEAD_DIM, HEAD_DIM), F32)] * 2
        for c in range(n_ctx_chunks):
            steps = [kept[c, 0], kept[n_ctx_chunks - 1 - c, 1]]
            ws = [_dot(w, s.astype(BF16)) for (_, _, w, _), s in zip(steps, states)]
            yield
            kv = [_dot(kd_t, (u - x).astype(BF16)) for (kd_t, _, _, u), x in zip(steps, ws)]
            yield
            states = [s * decay[0:1, :] + x for s, (_, decay, _, _), x in zip(states, steps, kv)]
        for d in range(2):
            s0_sc[slot_p, d] = states[d]

    def prepare(i):
        chunks = [i * GROUP + g for g in range(GROUP)]

        def early(n, d, kd, decay, qg, qk):
            c = chunks[n]
            kq_sc[slot_p, d, c, 0:CHUNK, :] = kd.T.astype(BF16)
            kq_sc[slot_p, d, c, CHUNK:2 * CHUNK, :] = qk.astype(BF16)
            wq_sc[slot_p, d, c, CHUNK:2 * CHUNK, :] = qg.astype(BF16)
            ge_sc[slot_p, d, c] = decay

        def late(n, d, w, u):
            c = chunks[n]
            wq_sc[slot_p, d, c, 0:CHUNK, :] = w.astype(BF16)
            u_sc[slot_p, d, pl.ds(_chunk_start(c), CHUNK), :] = u

        yield from _prep_stages(chunks, h, pk_ref, pv_ref, pq_ref, gcol_ref, grow_ref, tri_ref, off_ref,
                                early, late)

    def finish(c, o):
        r0 = _chunk_start(c)
        y = o * lax.rsqrt(jnp.mean(o * o, axis=-1, keepdims=True) + NORM_EPS) * onorm_ref[...]
        og_ref[pl.ds(r0, CHUNK), :] = (y * sz_ref[pl.ds(r0, CHUNK), :].astype(F32)).astype(og_ref.dtype)

    def scan(i, states, second_half):
        for g in range(GROUP):
            j = i * GROUP + g
            chunks = (j, n_chunks - 1 - j)
            ws = [_dot(wq_sc[slot_s, d, c], s.astype(BF16)) for d, (c, s) in enumerate(zip(chunks, states))]
            yield
            v_new = [u_sc[slot_s, d, pl.ds(_chunk_start(c), CHUNK), :] - x[:CHUNK]
                     for d, (c, x) in enumerate(zip(chunks, ws))]
            kv = [_dot(kq_sc[slot_s, d, c], vn.astype(BF16)) for d, (c, vn) in enumerate(zip(chunks, v_new))]
            yield
            states = [s * ge_sc[slot_s, d, c][0:1, :] + x[:CHUNK]
                      for d, (c, s, x) in enumerate(zip(chunks, states, kv))]
            for c, x, y in zip(chunks, ws, kv):
                o = x[CHUNK:] + y[CHUNK:]
                rows = pl.ds(_chunk_start(c), CHUNK)
                if second_half:
                    finish(c, o_sc[rows, :] + o)
                else:
                    o_sc[rows, :] = o
        return tuple(states)

    def body(second_half, i, states, *also):
        return _run_interleaved(scan(i, states, second_half), prepare(i), *also)[0]

    n_iter = n_chunks // GROUP
    states = (s0_sc[slot_s, 0], s0_sc[slot_s, 1])
    states = body(False, 0, states, context_states())
    states = lax.fori_loop(1, n_iter // 2, functools.partial(body, False), states)
    lax.fori_loop(n_iter // 2, n_iter, functools.partial(body, True), states)


def _delta_call(p3, gcol, grow, pc3, cgcol, cgrow, onorm_g):
    bsz, length, _ = p3.shape
    ctx_len = pc3.shape[1]
    n_chunks = length // CHUNK
    n_ctx_chunks = ctx_len // CHUNK
    n_heads_total = bsz * N_HEADS
    assert n_chunks % (2 * GROUP) == 0
    tri_masks, off_masks = _mask_constants()

    def prepared(t):
        t = jnp.minimum(t, n_heads_total - 1)
        return t // N_HEADS, t % N_HEADS

    def scanned(t):
        t = jnp.maximum(t - 1, 0)
        return t // N_HEADS, t % N_HEADS

    def head_cols(rows, first_block, which):
        def index(t):
            b, h = which(t)
            return b, 0, first_block + h
        return pl.BlockSpec((None, rows, LANES), index)

    def gates_col(rows):
        return pl.BlockSpec((None, rows, LANES), lambda t: (prepared(t)[0], 0, 0), pipeline_mode=pl.Buffered(1))

    def gates_row(rows):
        return pl.BlockSpec((LANES, rows), lambda t: (0, prepared(t)[0]), pipeline_mode=pl.Buffered(1))

    return pl.pallas_call(
        functools.partial(_delta_kernel, n_chunks=n_chunks, n_ctx_chunks=n_ctx_chunks,
                          n_heads_total=n_heads_total),
        grid=(n_heads_total + 1,),
        in_specs=[head_cols(length, 0, prepared), head_cols(length, N_HEADS, prepared),
                  head_cols(length, 2 * N_HEADS, prepared),
                  gates_col(length), gates_row(length),
                  head_cols(ctx_len, 0, prepared), head_cols(ctx_len, N_HEADS, prepared),
                  gates_col(ctx_len), gates_row(ctx_len),
                  head_cols(length, 3 * N_HEADS, scanned),
                  pl.BlockSpec((1, LANES), lambda t: (0, 0)),
                  pl.BlockSpec(tri_masks.shape, lambda t: (0, 0, 0), pipeline_mode=pl.Buffered(1)),
                  pl.BlockSpec(off_masks.shape, lambda t: (0, 0, 0), pipeline_mode=pl.Buffered(1))],
        out_specs=head_cols(length, 0, scanned),
        out_shape=jax.ShapeDtypeStruct((bsz, length, B_WIDTH), BF16),
        scratch_shapes=[pltpu.VMEM((2, 2, n_chunks, 2 * CHUNK, LANES), BF16),
                        pltpu.VMEM((2, 2, n_chunks, 2 * CHUNK, LANES), BF16),
                        pltpu.VMEM((2, 2, length, LANES), F32),
                        pltpu.VMEM((2, 2, n_chunks, 8, LANES), F32),
                        pltpu.VMEM((2, 2, HEAD_DIM, HEAD_DIM), F32),
                        pltpu.VMEM((length, LANES), F32)],
        compiler_params=pltpu.CompilerParams(
            dimension_semantics=("arbitrary",), vmem_limit_bytes=62 * 1024 * 1024),
        name="delta",
    )(p3, p3, p3, gcol, grow, pc3, pc3, cgcol, cgrow, p3, onorm_g, tri_masks, off_masks)


def _mix_kernel(x_ref, og_ref, mod_ref, g1_ref, wuv_ref, wg_ref, lng_ref, lnb_ref, ws_ref, bs_ref,
                wa_ref, wb_ref, wo_ref, o_ref, ua_ref, *, n_sub):
    x = x_ref[...]
    h = _norm_mod(x, g1_ref[...], mod_ref[:, 0:D_MODEL], mod_ref[:, D_MODEL:2 * D_MODEL]).astype(BF16)
    uv = _dot(h, wuv_ref[...])
    gates_pre = _dot(h, wg_ref[...])
    y_b = _dot(og_ref[...], wb_ref[...])
    v = jax.nn.gelu(uv[:, D_MODEL:])
    vc = v - jnp.mean(v, axis=-1, keepdims=True)
    vn = vc * lax.rsqrt(jnp.mean(vc * vc, axis=-1, keepdims=True) + NORM_EPS) * lng_ref[...] + lnb_ref[...]
    vb = vn.astype(BF16)
    for n in range(n_sub):
        rows = slice(n * A_CHUNK, (n + 1) * A_CHUNK)
        for g in range(A_GROUPS):
            cols = slice(g * LANES, (g + 1) * LANES)
            s = _dot(ws_ref[g], vb[rows, cols]) + bs_ref[:, cols]
            ua_ref[rows, cols] = (jax.nn.gelu(uv[rows, cols]) * s).astype(BF16)
    y_a = _dot(ua_ref[...], wa_ref[...])
    gates = jax.nn.sigmoid(gates_pre)
    t = gates[:, :D_MODEL] * y_a + gates[:, D_MODEL:] * y_b
    y = _dot(t.astype(BF16), wo_ref[...])
    o_ref[...] = x + mod_ref[:, 2 * D_MODEL:3 * D_MODEL] * y


def _mix_call(x2d, og, mod3, g1, wuv16, wg16, ln_g, ln_b, ws16, bs_cols, wa16, wb16, wo16, tm, tiles_per_batch):
    m = x2d.shape[0]
    rows = pl.BlockSpec((tm, D_MODEL), lambda i: (i, 0))
    return pl.pallas_call(
        functools.partial(_mix_kernel, n_sub=tm // A_CHUNK),
        grid=(m // tm,),
        in_specs=[rows, rows,
                  pl.BlockSpec((None, 1, 6 * D_MODEL), lambda i: (i // tiles_per_batch, 0, 0)),
                  _resident((1, D_MODEL)),
                  _resident((D_MODEL, 2 * D_MODEL)), _resident((D_MODEL, 2 * D_MODEL)),
                  _resident((1, D_MODEL)), _resident((1, D_MODEL)),
                  _resident((A_GROUPS, A_CHUNK, A_CHUNK)), _resident((A_CHUNK, D_MODEL)),
                  _resident((D_MODEL, D_MODEL)), _resident((D_MODEL, D_MODEL)), _resident((D_MODEL, D_MODEL))],
        out_specs=rows,
        out_shape=jax.ShapeDtypeStruct((m, D_MODEL), F32),
        scratch_shapes=[pltpu.VMEM((tm, D_MODEL), BF16)],
        compiler_params=pltpu.CompilerParams(
            dimension_semantics=("parallel",), vmem_limit_bytes=VMEM_LIMIT),
        name="mix",
    )(x2d, og, mod3, g1, wuv16, wg16, ln_g, ln_b, ws16, bs_cols, wa16, wb16, wo16)


def _ffn_kernel(x_ref, mod_ref, g2_ref, wup_ref, cw_ref, cb_ref, wd_ref, gf_ref, o_ref, *, n_split):
    tm = x_ref.shape[0]
    tf = D_FF // n_split
    x = x_ref[...]
    h = _norm_mod(x, g2_ref[...], mod_ref[:, 3 * D_MODEL:4 * D_MODEL],
                  mod_ref[:, 4 * D_MODEL:5 * D_MODEL]).astype(BF16)
    col_in_row = lax.broadcasted_iota(jnp.int32, (tm, tf), 0) & (GRID_W - 1)
    acc = None
    for j in range(n_split):
        cols = slice(j * tf, (j + 1) * tf)
        a = _dot(h, wup_ref[:, cols])
        b = _dot(h, wup_ref[:, D_FF + j * tf:D_FF + (j + 1) * tf])
        a_prev = jnp.where(col_in_row == 0, 0.0, pltpu.roll(a, 1, 0))
        a_next = jnp.where(col_in_row == GRID_W - 1, 0.0, pltpu.roll(a, tm - 1, 0))
        ac = a_prev * cw_ref[0:1, cols] + a * cw_ref[1:2, cols] + a_next * cw_ref[2:3, cols] + cb_ref[:, cols]
        part = _dot((jax.nn.gelu(ac) * b).astype(BF16), wd_ref[cols, :])
        acc = part if acc is None else acc + part
    x2 = x + mod_ref[:, 5 * D_MODEL:6 * D_MODEL] * acc
    ms = jnp.mean(x2 * x2, axis=-1, keepdims=True)
    o_ref[...] = x2 * lax.rsqrt(ms + NORM_EPS) * gf_ref[...]


def _ffn_call(x1, mod3, norm2_g, w_up16, conv_w, conv_b, w_down16, final_g, tm, n_split, tiles_per_batch):
    m = x1.shape[0]
    assert D_FF % (n_split * LANES) == 0 and tm % GRID_W == 0
    return pl.pallas_call(
        functools.partial(_ffn_kernel, n_split=n_split),
        grid=(m // tm,),
        in_specs=[pl.BlockSpec((tm, D_MODEL), lambda i: (i, 0)),
                  pl.BlockSpec((None, 1, 6 * D_MODEL), lambda i: (i // tiles_per_batch, 0, 0)),
                  _resident((1, D_MODEL)),
                  _resident((D_MODEL, 2 * D_FF)),
                  _resident((3, D_FF)), _resident((1, D_FF)),
                  _resident((D_FF, D_MODEL)),
                  _resident((1, D_MODEL))],
        out_specs=pl.BlockSpec((tm, D_MODEL), lambda i: (i, 0)),
        out_shape=jax.ShapeDtypeStruct((m, D_MODEL), F32),
        compiler_params=pltpu.CompilerParams(
            dimension_semantics=("parallel",), vmem_limit_bytes=VMEM_LIMIT),
        name="ffn",
    )(x1, mod3, norm2_g, w_up16, conv_w, conv_b, w_down16, final_g)


def kernel(x, c, ctx, c_ctx, w_mod, b_mod, norm1_g, w_in, conv_qkv, a_log, dt_bias, onorm_g, w_proj_b,
           a_ln_g, a_ln_b, a_ws, a_bs, w_proj_a, w_out, norm2_g, w_up, ffn_conv_w, ffn_conv_b, w_down,
           final_g):
    bsz, length, _ = x.shape
    ctx_len = ctx.shape[1]
    assert w_mod.shape[0] == 1 and bsz <= 7
    assert length % 1024 == 0 and ctx_len % CHUNK == 0

    w = w_in[0]
    w_kvqz = jnp.concatenate([w[:, :OFF_BA], w[:, OFF_Q:OFF_U]], axis=1).astype(BF16)
    w_kv = w[:, :OFF_BA].astype(BF16)
    w_ba = jnp.pad(w[:, OFF_BA:OFF_Q], ((0, 0), (0, LANES - 4 * N_HEADS))).astype(BF16)
    w_uv = w[:, OFF_U:OFF_GA].astype(BF16)
    w_gates = w[:, OFF_GA:].astype(BF16)
    lane_pad = (2 * N_HEADS, LANES - 4 * N_HEADS)
    a_lane = jnp.pad(jnp.exp(a_log[0].astype(F32)).reshape(-1), lane_pad).reshape(1, LANES)
    dt_lane = jnp.pad(dt_bias[0].astype(F32).reshape(-1), lane_pad).reshape(1, LANES)
    bs_cols = jnp.repeat(a_bs[0].T, LANES, axis=1)

    cond8 = jnp.concatenate([c, c_ctx[None, :], jnp.zeros((7 - bsz, D_MODEL), F32)], axis=0)
    mod3 = _mod_call(cond8, w_mod[0], b_mod[0]).reshape(8, 1, 6 * D_MODEL)

    x2d = x.reshape(bsz * length, D_MODEL)
    g1 = norm1_g[0].reshape(1, D_MODEL)
    tm_in = 1024
    tiles_per_seq = length // tm_in
    p2d, gcol, grow = _inproj_call(x2d, mod3, lambda i: i // tiles_per_seq, g1, w_kvqz, w_ba, conv_qkv[0],
                                   a_lane, dt_lane, ('k', 'v', 'q', 'z'), tm_in, tiles_per_seq, "in_proj")
    pc2d, cgcol, cgrow = _inproj_call(ctx.reshape(bsz * ctx_len, D_MODEL), mod3, lambda i: bsz, g1, w_kv, w_ba,
                                      conv_qkv[0], a_lane, dt_lane, ('k', 'v'), ctx_len, 1, "in_proj_ctx")
    p3 = p2d.reshape(bsz, length, 4 * B_WIDTH)
    pc3 = pc2d.reshape(bsz, ctx_len, 2 * B_WIDTH)
    og = _delta_call(p3, gcol.reshape(bsz, length, LANES), grow, pc3, cgcol.reshape(bsz, ctx_len, LANES), cgrow,
                     onorm_g[0].reshape(1, HEAD_DIM))

    tm = 512
    tiles_per_batch = length // tm
    x1 = _mix_call(x2d, og.reshape(bsz * length, B_WIDTH), mod3, g1, w_uv, w_gates,
                   a_ln_g[0].reshape(1, -1), a_ln_b[0].reshape(1, -1), a_ws[0].astype(BF16), bs_cols,
                   w_proj_a[0].astype(BF16), w_proj_b[0].astype(BF16), w_out[0].astype(BF16), tm, tiles_per_batch)
    out = _ffn_call(x1, mod3, norm2_g[0].reshape(1, -1), w_up[0].astype(BF16), ffn_conv_w[0],
                    ffn_conv_b[0].reshape(1, -1), w_down[0].astype(BF16), final_g.reshape(1, -1),
                    tm, 1, tiles_per_batch)
    return out.reshape(bsz, length, D_MODEL)
```

```python
import functools

import jax
import jax.numpy as jnp
from jax import lax
from jax.experimental import pallas as pl
from jax.experimental.pallas import tpu as pltpu

F32 = jnp.float32
BF16 = jnp.bfloat16

D_MODEL = 1024
GRID_W = 64
NORM_EPS = 1e-6
N_HEADS = 8
HEAD_DIM = 128
B_WIDTH = N_HEADS * HEAD_DIM
A_GROUPS = 8
A_CHUNK = 128
D_FF = 2816
OFF_BA = 2 * B_WIDTH
OFF_Q = OFF_BA + 4 * N_HEADS
OFF_U = OFF_Q + 2 * B_WIDTH
OFF_GA = OFF_U + 2 * D_MODEL
LANES = 128
MXU_COLS = 256
CHUNK = 128
GROUP = 8

VMEM_LIMIT = 56 * 1024 * 1024


def _silu(x):
    return x * jax.nn.sigmoid(x)


def _dot(a, b):
    return jnp.dot(a, b, preferred_element_type=F32)


def _dot_nt(a, b):
    return lax.dot_general(a, b, (((1,), (1,)), ((), ())), preferred_element_type=F32)


def _mod_kernel(c_ref, w_ref, b_ref, o_ref):
    cond = _silu(c_ref[...])
    o_ref[...] = jnp.dot(cond, w_ref[...], preferred_element_type=F32,
                         precision=lax.Precision.HIGHEST) + b_ref[...]


def _mod_call(cond8, w_mod, b_mod):
    n = w_mod.shape[1]
    tn = 1536
    return pl.pallas_call(
        _mod_kernel,
        grid=(n // tn,),
        in_specs=[pl.BlockSpec((8, D_MODEL), lambda j: (0, 0)),
                  pl.BlockSpec((D_MODEL, tn), lambda j: (0, j)),
                  pl.BlockSpec((1, tn), lambda j: (0, j))],
        out_specs=pl.BlockSpec((8, tn), lambda j: (0, j)),
        out_shape=jax.ShapeDtypeStruct((8, n), F32),
        name="mod",
    )(cond8, w_mod, b_mod.reshape(1, n))


def _norm_mod(x, g, shift, scale):
    ms = jnp.mean(x * x, axis=-1, keepdims=True)
    y = x * lax.rsqrt(ms + NORM_EPS) * g
    return y * (1.0 + scale) + shift


def _l2norm(t):
    return t * lax.rsqrt(jnp.sum(t * t, axis=-1, keepdims=True) + NORM_EPS)


def _gate_columns(ba, a_lane, dt_lane):
    n_sub = ba.shape[0] // CHUNK
    lane = lax.broadcasted_iota(jnp.int32, ba.shape, 1)
    ri = lax.broadcasted_iota(jnp.int32, (CHUNK, CHUNK), 0)
    ci = lax.broadcasted_iota(jnp.int32, (CHUNK, CHUNK), 1)
    tri_lo, tri_up = (ri >= ci).astype(F32), (ri <= ci).astype(F32)
    beta = jax.nn.sigmoid(ba)
    y = ba + dt_lane
    softplus = jnp.maximum(y, 0.0) + jnp.log1p(jnp.exp(-jnp.abs(y)))
    g = jnp.where((lane >= 2 * N_HEADS) & (lane < 4 * N_HEADS), -a_lane * softplus, 0.0)
    part = 4 * N_HEADS
    hi = g.astype(BF16).astype(F32)
    mid = (g - hi).astype(BF16).astype(F32)
    lo = (g - hi - mid).astype(BF16).astype(F32)
    packed = (hi + pltpu.roll(mid, part, 1) + pltpu.roll(lo, 2 * part, 1)).astype(BF16)

    def unpack(r):
        return r + pltpu.roll(r, LANES - part, 1) + pltpu.roll(r, LANES - 2 * part, 1)

    tri_lo, tri_up = tri_lo.astype(BF16), tri_up.astype(BF16)
    parts = [packed[s * CHUNK:(s + 1) * CHUNK, :] for s in range(n_sub)]
    pre = [unpack(_dot(tri_lo, x)) for x in parts]
    suf = [unpack(_dot(tri_up, x)) for x in parts]
    lane = lax.broadcasted_iota(jnp.int32, (CHUNK, LANES), 1)
    return [jnp.where(lane < 2 * N_HEADS, beta[s * CHUNK:(s + 1) * CHUNK, :],
                      jnp.where(lane < 3 * N_HEADS, pre[s], suf[s])) for s in range(n_sub)]


def _resident(shape):
    return pl.BlockSpec(shape, lambda i: (0,) * len(shape), pipeline_mode=pl.Buffered(1))


def _slab_order(kinds, slabs_per_group):
    heavy = [(g, s) for g, kind in enumerate(kinds) if kind != 'z' for s in range(slabs_per_group)]
    light = [(g, s) for g, kind in enumerate(kinds) if kind == 'z' for s in range(slabs_per_group)]
    every = max(len(heavy) // max(len(light), 1), 1)
    order = []
    for n, item in enumerate(heavy):
        order.append(item)
        if light and (n + 1) % every == 0:
            order.append(light.pop(0))
    return order + light


def _inproj_kernel(x_ref, xb_ref, xa_ref, mod_ref, g_ref, w_ref, wba_ref, cw_ref, a_ref, dt_ref,
                   o_ref, gcol_ref, grow_ref, h_ref, halo_ref, *, tiles_per_seq, kinds):
    i = pl.program_id(0)
    tm = o_ref.shape[0]
    shift, scale = mod_ref[:, 0:D_MODEL], mod_ref[:, D_MODEL:2 * D_MODEL]
    h_ref[...] = _norm_mod(x_ref[...], g_ref[...], shift, scale).astype(BF16)
    halo = jnp.concatenate([xb_ref[...], xa_ref[...]], axis=0)
    halo_ref[...] = _norm_mod(halo, g_ref[...], shift, scale).astype(BF16)
    ba = _dot(h_ref[...], wba_ref[...])
    for s, col in enumerate(_gate_columns(ba, a_ref[...], dt_ref[...])):
        rows = slice(s * CHUNK, (s + 1) * CHUNK)
        gcol_ref[rows, :] = col
        grow_ref[:, rows] = col.T

    tile_in_seq = i % tiles_per_seq
    slab = MXU_COLS
    row8 = lax.broadcasted_iota(jnp.int32, (8, slab), 0)
    n_parts = max(tm // 256, 1)
    part = tm // n_parts
    tiled = (part // 8, 8, slab)
    for g, s in _slab_order(kinds, B_WIDTH // slab):
        kind = kinds[g]
        col0 = g * B_WIDTH + s * slab
        cols = slice(col0, col0 + slab)
        ps = [_dot(h_ref[n * part:(n + 1) * part, :], w_ref[:, cols]) for n in range(n_parts)]
        if kind != 'z':
            edge = _dot(halo_ref[...], w_ref[:, cols])
            taps = [jnp.broadcast_to(cw_ref[r:r + 1, cols], (8, slab)) for r in range(3)]
        for n, p in enumerate(ps):
            if kind != 'z':
                prev_row = ps[n - 1][part - 1:part] if n > 0 else jnp.where(tile_in_seq > 0, edge[7:8, :], 0.0)
                next_row = (ps[n + 1][0:1] if n + 1 < n_parts
                            else jnp.where(tile_in_seq < tiles_per_seq - 1, edge[8:9, :], 0.0))
                xp = pltpu.roll(p, 1, 0)
                xp = jnp.concatenate([jnp.where(row8 == 0, prev_row, xp[0:8]), xp[8:]], axis=0)
                xn = pltpu.roll(p, part - 1, 0)
                xn = jnp.concatenate([xn[:part - 8], jnp.where(row8 == 7, next_row, xn[part - 8:])], axis=0)
                p = (xp.reshape(tiled) * taps[0] + p.reshape(tiled) * taps[1]
                     + xn.reshape(tiled) * taps[2]).reshape(part, slab)
            y = _silu(p)
            rows = slice(n * part, (n + 1) * part)
            for hd in range(slab // HEAD_DIM):
                head = y[:, hd * HEAD_DIM:(hd + 1) * HEAD_DIM]
                if kind in ('k', 'q'):
                    head = _l2norm(head)
                if kind == 'q':
                    head = head * (HEAD_DIM ** -0.5)
                o_ref[rows, col0 + hd * HEAD_DIM:col0 + (hd + 1) * HEAD_DIM] = head.astype(o_ref.dtype)


def _inproj_call(x2d, mod3, mod_row_of_tile, norm_g, w, w_ba, conv_w, a_lane, dt_lane, kinds, tm, tiles_per_seq,
                 name):
    m = x2d.shape[0]
    n = w.shape[1]
    assert n == B_WIDTH * len(kinds) and tm % CHUNK == 0
    assert all(kind != 'z' for kind in kinds[:conv_w.shape[1] // B_WIDTH]) and 'z' not in kinds[:-1]
    last_halo = m // 8 - 1
    return pl.pallas_call(
        functools.partial(_inproj_kernel, tiles_per_seq=tiles_per_seq, kinds=kinds),
        grid=(m // tm,),
        in_specs=[pl.BlockSpec((tm, D_MODEL), lambda i: (i, 0)),
                  pl.BlockSpec((8, D_MODEL), lambda i: (jnp.maximum(i * (tm // 8) - 1, 0), 0)),
                  pl.BlockSpec((8, D_MODEL), lambda i: (jnp.minimum((i + 1) * (tm // 8), last_halo), 0)),
                  pl.BlockSpec((None, 1, 6 * D_MODEL), lambda i: (mod_row_of_tile(i), 0, 0)),
                  _resident((1, D_MODEL)),
                  _resident((D_MODEL, n)),
                  _resident((D_MODEL, LANES)),
                  _resident(conv_w.shape),
                  _resident((1, LANES)), _resident((1, LANES))],
        out_specs=[pl.BlockSpec((tm, n), lambda i: (i, 0)),
                   pl.BlockSpec((tm, LANES), lambda i: (i, 0)),
                   pl.BlockSpec((LANES, tm), lambda i: (0, i))],
        out_shape=[jax.ShapeDtypeStruct((m, n), BF16), jax.ShapeDtypeStruct((m, LANES), F32),
                   jax.ShapeDtypeStruct((LANES, m), F32)],
        scratch_shapes=[pltpu.VMEM((tm, D_MODEL), BF16), pltpu.VMEM((16, D_MODEL), BF16)],
        compiler_params=pltpu.CompilerParams(
            dimension_semantics=("parallel",), vmem_limit_bytes=VMEM_LIMIT),
        name=name,
    )(x2d, x2d, x2d, mod3, norm_g, w, w_ba, conv_w, a_lane, dt_lane)


def _run_interleaved(*gens):
    results = [None] * len(gens)
    live = list(range(len(gens)))
    while live:
        for idx in list(live):
            try:
                next(gens[idx])
            except StopIteration as stop:
                results[idx] = stop.value
                live.remove(idx)
    return results


TRI_INCL, TRI_STRICT, TRI_BLK16, TRI_EYE = 0, 2, 4, 5
MERGE_SIZES = (32, 64, 128)


def _mask_constants():
    ri = lax.broadcasted_iota(jnp.int32, (CHUNK, CHUNK), 0)
    ci = lax.broadcasted_iota(jnp.int32, (CHUNK, CHUNK), 1)

    def same_block(size):
        return (ri // size) == (ci // size)

    tri = jnp.stack([ri >= ci, ri <= ci, ri > ci, ri < ci, same_block(16), ri == ci]).astype(F32)
    off = jnp.stack([same_block(s) & ~same_block(s // 2) for s in MERGE_SIZES]).astype(BF16)
    return tri, off


def _tri_inverse_stages(n_mats, tri_ref, off_ref):
    n16 = [n * tri_ref[TRI_BLK16] for n in n_mats]
    ts = [tri_ref[TRI_EYE] + n for n in n16]
    pbs = [n.astype(BF16) for n in n16]
    pbs = [_dot(p, p).astype(BF16) for p in pbs]
    nbs = [n.astype(BF16) for n in n_mats]
    yield
    for _ in range(2):
        ts = [t + _dot(t.astype(BF16), p) for t, p in zip(ts, pbs)]
        pbs = [_dot(p, p).astype(BF16) for p in pbs]
        yield
    ts = [t + _dot(t.astype(BF16), p) for t, p in zip(ts, pbs)]
    yield
    tbs = [t.astype(BF16) for t in ts]
    for level in range(len(MERGE_SIZES)):
        xs = [_dot(nb * off_ref[level], tb).astype(BF16) for nb, tb in zip(nbs, tbs)]
        yield
        tbs = [tb + _dot(tb, x).astype(BF16) for tb, x in zip(tbs, xs)]
        yield
    return tbs


def _chunk_start(c):
    return c * CHUNK if isinstance(c, int) else pl.multiple_of(c * CHUNK, CHUNK)


def _prep_stages(chunks, h, k_ref, v_ref, q_ref, gcol_ref, grow_ref, tri_ref, off_ref, early, late):
    lane = lax.broadcasted_iota(jnp.int32, (CHUNK, LANES), 1)
    head_row = lax.broadcasted_iota(jnp.int32, (N_HEADS, CHUNK), 0)

    acts, prods = [], []
    for c in chunks:
        rows = pl.ds(_chunk_start(c), CHUNK)
        kb16 = k_ref[rows, :]
        k = kb16.astype(F32)
        kt16 = k.T.astype(BF16)
        v = v_ref[rows, :].astype(F32)
        if q_ref is not None:
            qb16 = q_ref[rows, :]
            aa = _dot(jnp.concatenate([kb16, qb16], axis=0), kt16)
            acts.append((k, v, qb16.astype(F32)))
            prods.append((aa[:CHUNK], aa[CHUNK:]))
        else:
            acts.append((k, v, None))
            prods.append((_dot(kb16, kt16), None))
    yield

    neg_ms, rhss = [], []
    for i, (c, (k, v, q), (a_kk, a_qk)) in enumerate(zip(chunks, acts, prods)):
        r0 = _chunk_start(c)
        gates = gcol_ref[pl.ds(r0, CHUNK), :]
        for d in range(2):
            beta = jnp.sum(jnp.where(lane == d * N_HEADS + h, gates, 0.0), axis=1, keepdims=True)
            gc = jnp.sum(jnp.where(lane == (2 + d) * N_HEADS + h, gates, 0.0), axis=1, keepdims=True)
            all_heads = grow_ref[(2 + d) * N_HEADS:(3 + d) * N_HEADS, pl.ds(r0, CHUNK)]
            gc_row = jnp.sum(jnp.where(head_row == h, all_heads, 0.0), axis=0, keepdims=True)
            decay_ij = jnp.exp(jnp.minimum(gc - gc_row, 0.0))
            neg_ms.append(a_kk * (-beta) * (decay_ij * tri_ref[TRI_STRICT + d]))
            eg = jnp.exp(gc)
            rhss.append(jnp.concatenate([k * (beta * eg), v * beta], axis=1).astype(BF16))
            g_end = gc[CHUNK - 1:CHUNK, :] if d == 0 else gc[0:1, :]
            kd = k * jnp.exp(g_end - gc)
            decay = jnp.broadcast_to(jnp.exp(g_end), (8, LANES))
            if q is not None:
                early(i, d, kd, decay, q * eg, a_qk * (decay_ij * tri_ref[TRI_INCL + d]))
            else:
                early(i, d, kd, decay, None, None)

    t_invs = yield from _tri_inverse_stages(neg_ms, tri_ref, off_ref)
    wus = [_dot(t, rhs) for t, rhs in zip(t_invs, rhss)]
    yield
    for n, wu in enumerate(wus):
        late(n // 2, n % 2, wu[:, :HEAD_DIM], wu[:, HEAD_DIM:])


def _delta_kernel(pk_ref, pv_ref, pq_ref, gcol_ref, grow_ref, ck_ref, cv_ref, cgcol_ref, cgrow_ref,
                  sz_ref, onorm_ref, tri_ref, off_ref, og_ref,
                  wq_sc, kq_sc, u_sc, ge_sc, s0_sc, o_sc, *, n_chunks, n_ctx_chunks, n_heads_total):
    t = pl.program_id(0)
    h = jnp.minimum(t, n_heads_total - 1) % N_HEADS
    slot_p = t % 2
    slot_s = 1 - slot_p

    @pl.when(t == 0)
    def _():
        wq_sc[1] = jnp.zeros(wq_sc.shape[1:], wq_sc.dtype)
        kq_sc[1] = jnp.zeros(kq_sc.shape[1:], kq_sc.dtype)
        u_sc[1] = jnp.zeros(u_sc.shape[1:], u_sc.dtype)
        ge_sc[1] = jnp.zeros(ge_sc.shape[1:], ge_sc.dtype)
        s0_sc[1] = jnp.zeros(s0_sc.shape[1:], s0_sc.dtype)

    def context_states():
        kept = {}

        def early(c, d, kd, decay, qg, qk):
            kept[c, d] = [kd.T.astype(BF16), decay]

        def late(c, d, w, u):
            kept[c, d] += [w.astype(BF16), u]

        yield from _prep_stages(list(range(n_ctx_chunks)), h, ck_ref, cv_ref, None, cgcol_ref, cgrow_ref,
                                tri_ref, off_ref, early, late)
        states = [jnp.zeros((HEAD_DIM, HEAD_DIM), F32)] * 2
        for c in range(n_ctx_chunks):
            steps = [kept[c, 0], kept[n_ctx_chunks - 1 - c, 1]]
            ws = [_dot(w, s.astype(BF16)) for (_, _, w, _), s in zip(steps, states)]
            yield
            kv = [_dot(kd_t, (u - x).astype(BF16)) for (kd_t, _, _, u), x in zip(steps, ws)]
            yield
            states = [s * decay[0:1, :] + x for s, (_, decay, _, _), x in zip(states, steps, kv)]
        for d in range(2):
            s0_sc[slot_p, d] = states[d]

    def prepare(i):
        chunks = [i * GROUP + g for g in range(GROUP)]

        def early(n, d, kd, decay, qg, qk):
            c = chunks[n]
            kq_sc[slot_p, d, c, 0:CHUNK, :] = kd.T.astype(BF16)
            kq_sc[slot_p, d, c, CHUNK:2 * CHUNK, :] = qk.astype(BF16)
            wq_sc[slot_p, d, c, CHUNK:2 * CHUNK, :] = qg.astype(BF16)
            ge_sc[slot_p, d, c] = decay

        def late(n, d, w, u):
            c = chunks[n]
            wq_sc[slot_p, d, c, 0:CHUNK, :] = w.astype(BF16)
            u_sc[slot_p, d, pl.ds(_chunk_start(c), CHUNK), :] = u

        yield from _prep_stages(chunks, h, pk_ref, pv_ref, pq_ref, gcol_ref, grow_ref, tri_ref, off_ref,
                                early, late)

    def finish(c, o):
        r0 = _chunk_start(c)
        y = o * lax.rsqrt(jnp.mean(o * o, axis=-1, keepdims=True) + NORM_EPS) * onorm_ref[...]
        og_ref[pl.ds(r0, CHUNK), :] = (y * sz_ref[pl.ds(r0, CHUNK), :].astype(F32)).astype(og_ref.dtype)

    def scan(i, states, second_half):
        for g in range(GROUP):
            j = i * GROUP + g
            chunks = (j, n_chunks - 1 - j)
            ws = [_dot(wq_sc[slot_s, d, c], s.astype(BF16)) for d, (c, s) in enumerate(zip(chunks, states))]
            yield
            v_new = [u_sc[slot_s, d, pl.ds(_chunk_start(c), CHUNK), :] - x[:CHUNK]
                     for d, (c, x) in enumerate(zip(chunks, ws))]
            kv = [_dot(kq_sc[slot_s, d, c], vn.astype(BF16)) for d, (c, vn) in enumerate(zip(chunks, v_new))]
            yield
            states = [s * ge_sc[slot_s, d, c][0:1, :] + x[:CHUNK]
                      for d, (c, s, x) in enumerate(zip(chunks, states, kv))]
            for c, x, y in zip(chunks, ws, kv):
                o = x[CHUNK:] + y[CHUNK:]
                rows = pl.ds(_chunk_start(c), CHUNK)
                if second_half:
                    finish(c, o_sc[rows, :] + o)
                else:
                    o_sc[rows, :] = o
        return tuple(states)

    def body(second_half, i, states, *also):
        return _run_interleaved(scan(i, states, second_half), prepare(i), *also)[0]

    n_iter = n_chunks // GROUP
    states = (s0_sc[slot_s, 0], s0_sc[slot_s, 1])
    states = body(False, 0, states, context_states())
    states = lax.fori_loop(1, n_iter // 2, functools.partial(body, False), states)
    lax.fori_loop(n_iter // 2, n_iter, functools.partial(body, True), states)


def _delta_call(p3, gcol, grow, pc3, cgcol, cgrow, onorm_g):
    bsz, length, _ = p3.shape
    ctx_len = pc3.shape[1]
    n_chunks = length // CHUNK
    n_ctx_chunks = ctx_len // CHUNK
    n_heads_total = bsz * N_HEADS
    assert n_chunks % (2 * GROUP) == 0
    tri_masks, off_masks = _mask_constants()

    def prepared(t):
        t = jnp.minimum(t, n_heads_total - 1)
        return t // N_HEADS, t % N_HEADS

    def scanned(t):
        t = jnp.maximum(t - 1, 0)
        return t // N_HEADS, t % N_HEADS

    def head_cols(rows, first_block, which):
        def index(t):
            b, h = which(t)
            return b, 0, first_block + h
        return pl.BlockSpec((None, rows, LANES), index)

    def gates_col(rows):
        return pl.BlockSpec((None, rows, LANES), lambda t: (prepared(t)[0], 0, 0), pipeline_mode=pl.Buffered(1))

    def gates_row(rows):
        return pl.BlockSpec((LANES, rows), lambda t: (0, prepared(t)[0]), pipeline_mode=pl.Buffered(1))

    return pl.pallas_call(
        functools.partial(_delta_kernel, n_chunks=n_chunks, n_ctx_chunks=n_ctx_chunks,
                          n_heads_total=n_heads_total),
        grid=(n_heads_total + 1,),
        in_specs=[head_cols(length, 0, prepared), head_cols(length, N_HEADS, prepared),
                  head_cols(length, 2 * N_HEADS, prepared),
                  gates_col(length), gates_row(length),
                  head_cols(ctx_len, 0, prepared), head_cols(ctx_len, N_HEADS, prepared),
                  gates_col(ctx_len), gates_row(ctx_len),
                  head_cols(length, 3 * N_HEADS, scanned),
                  pl.BlockSpec((1, LANES), lambda t: (0, 0)),
                  pl.BlockSpec(tri_masks.shape, lambda t: (0, 0, 0), pipeline_mode=pl.Buffered(1)),
                  pl.BlockSpec(off_masks.shape, lambda t: (0, 0, 0), pipeline_mode=pl.Buffered(1))],
        out_specs=head_cols(length, 0, scanned),
        out_shape=jax.ShapeDtypeStruct((bsz, length, B_WIDTH), BF16),
        scratch_shapes=[pltpu.VMEM((2, 2, n_chunks, 2 * CHUNK, LANES), BF16),
                        pltpu.VMEM((2, 2, n_chunks, 2 * CHUNK, LANES), BF16),
                        pltpu.VMEM((2, 2, length, LANES), F32),
                        pltpu.VMEM((2, 2, n_chunks, 8, LANES), F32),
                        pltpu.VMEM((2, 2, HEAD_DIM, HEAD_DIM), F32),
                        pltpu.VMEM((length, LANES), F32)],
        compiler_params=pltpu.CompilerParams(
            dimension_semantics=("arbitrary",), vmem_limit_bytes=62 * 1024 * 1024),
        name="delta",
    )(p3, p3, p3, gcol, grow, pc3, pc3, cgcol, cgrow, p3, onorm_g, tri_masks, off_masks)


def _mix_kernel(x_ref, og_ref, mod_ref, g1_ref, wuv_ref, wg_ref, lng_ref, lnb_ref, ws_ref, bs_ref,
                wa_ref, wb_ref, wo_ref, o_ref, ua_ref, *, n_sub):
    x = x_ref[...]
    h = _norm_mod(x, g1_ref[...], mod_ref[:, 0:D_MODEL], mod_ref[:, D_MODEL:2 * D_MODEL]).astype(BF16)
    uv = _dot(h, wuv_ref[...])
    gates_pre = _dot(h, wg_ref[...])
    y_b = _dot(og_ref[...], wb_ref[...])
    v = jax.nn.gelu(uv[:, D_MODEL:])
    vc = v - jnp.mean(v, axis=-1, keepdims=True)
    vn = vc * lax.rsqrt(jnp.mean(vc * vc, axis=-1, keepdims=True) + NORM_EPS) * lng_ref[...] + lnb_ref[...]
    vb = vn.astype(BF16)
    for n in range(n_sub):
        rows = slice(n * A_CHUNK, (n + 1) * A_CHUNK)
        for g in range(A_GROUPS):
            cols = slice(g * LANES, (g + 1) * LANES)
            s = _dot(ws_ref[g], vb[rows, cols]) + bs_ref[:, cols]
            ua_ref[rows, cols] = (jax.nn.gelu(uv[rows, cols]) * s).astype(BF16)
    y_a = _dot(ua_ref[...], wa_ref[...])
    gates = jax.nn.sigmoid(gates_pre)
    t = gates[:, :D_MODEL] * y_a + gates[:, D_MODEL:] * y_b
    y = _dot(t.astype(BF16), wo_ref[...])
    o_ref[...] = x + mod_ref[:, 2 * D_MODEL:3 * D_MODEL] * y


def _mix_call(x2d, og, mod3, g1, wuv16, wg16, ln_g, ln_b, ws16, bs_cols, wa16, wb16, wo16, tm, tiles_per_batch):
    m = x2d.shape[0]
    rows = pl.BlockSpec((tm, D_MODEL), lambda i: (i, 0))
    return pl.pallas_call(
        functools.partial(_mix_kernel, n_sub=tm // A_CHUNK),
        grid=(m // tm,),
        in_specs=[rows, rows,
                  pl.BlockSpec((None, 1, 6 * D_MODEL), lambda i: (i // tiles_per_batch, 0, 0)),
                  _resident((1, D_MODEL)),
                  _resident((D_MODEL, 2 * D_MODEL)), _resident((D_MODEL, 2 * D_MODEL)),
                  _resident((1, D_MODEL)), _resident((1, D_MODEL)),
                  _resident((A_GROUPS, A_CHUNK, A_CHUNK)), _resident((A_CHUNK, D_MODEL)),
                  _resident((D_MODEL, D_MODEL)), _resident((D_MODEL, D_MODEL)), _resident((D_MODEL, D_MODEL))],
        out_specs=rows,
        out_shape=jax.ShapeDtypeStruct((m, D_MODEL), F32),
        scratch_shapes=[pltpu.VMEM((tm, D_MODEL), BF16)],
        compiler_params=pltpu.CompilerParams(
            dimension_semantics=("parallel",), vmem_limit_bytes=VMEM_LIMIT),
        name="mix",
    )(x2d, og, mod3, g1, wuv16, wg16, ln_g, ln_b, ws16, bs_cols, wa16, wb16, wo16)


def _ffn_kernel(x_ref, mod_ref, g2_ref, wup_ref, cw_ref, cb_ref, wd_ref, gf_ref, o_ref, *, n_parts):
    tm = x_ref.shape[0]
    part = tm // n_parts
    col_in_row = lax.broadcasted_iota(jnp.int32, (part, D_FF), 0) & (GRID_W - 1)
    xs, ups = [], []
    for n in range(n_parts):
        x = x_ref[n * part:(n + 1) * part, :]
        h = _norm_mod(x, g2_ref[...], mod_ref[:, 3 * D_MODEL:4 * D_MODEL],
                      mod_ref[:, 4 * D_MODEL:5 * D_MODEL]).astype(BF16)
        xs.append(x)
        ups.append((_dot(h, wup_ref[:, :D_FF]), _dot(h, wup_ref[:, D_FF:])))
    for n, (x, (a, b)) in enumerate(zip(xs, ups)):
        a_prev = jnp.where(col_in_row == 0, 0.0, pltpu.roll(a, 1, 0))
        a_next = jnp.where(col_in_row == GRID_W - 1, 0.0, pltpu.roll(a, part - 1, 0))
        ac = a_prev * cw_ref[0:1, :] + a * cw_ref[1:2, :] + a_next * cw_ref[2:3, :] + cb_ref[...]
        y = _dot((jax.nn.gelu(ac) * b).astype(BF16), wd_ref[...])
        x2 = x + mod_ref[:, 5 * D_MODEL:6 * D_MODEL] * y
        ms = jnp.mean(x2 * x2, axis=-1, keepdims=True)
        o_ref[n * part:(n + 1) * part, :] = x2 * lax.rsqrt(ms + NORM_EPS) * gf_ref[...]


def _ffn_call(x1, mod3, norm2_g, w_up16, conv_w, conv_b, w_down16, final_g, tm, n_parts, tiles_per_batch):
    m = x1.shape[0]
    assert tm % (n_parts * GRID_W) == 0
    return pl.pallas_call(
        functools.partial(_ffn_kernel, n_parts=n_parts),
        grid=(m // tm,),
        in_specs=[pl.BlockSpec((tm, D_MODEL), lambda i: (i, 0)),
                  pl.BlockSpec((None, 1, 6 * D_MODEL), lambda i: (i // tiles_per_batch, 0, 0)),
                  _resident((1, D_MODEL)),
                  _resident((D_MODEL, 2 * D_FF)),
                  _resident((3, D_FF)), _resident((1, D_FF)),
                  _resident((D_FF, D_MODEL)),
                  _resident((1, D_MODEL))],
        out_specs=pl.BlockSpec((tm, D_MODEL), lambda i: (i, 0)),
        out_shape=jax.ShapeDtypeStruct((m, D_MODEL), F32),
        compiler_params=pltpu.CompilerParams(
            dimension_semantics=("parallel",), vmem_limit_bytes=VMEM_LIMIT),
        name="ffn",
    )(x1, mod3, norm2_g, w_up16, conv_w, conv_b, w_down16, final_g)


def kernel(x, c, ctx, c_ctx, w_mod, b_mod, norm1_g, w_in, conv_qkv, a_log, dt_bias, onorm_g, w_proj_b,
           a_ln_g, a_ln_b, a_ws, a_bs, w_proj_a, w_out, norm2_g, w_up, ffn_conv_w, ffn_conv_b, w_down,
           final_g):
    bsz, length, _ = x.shape
    ctx_len = ctx.shape[1]
    assert w_mod.shape[0] == 1 and bsz <= 7
    assert length % 1024 == 0 and ctx_len % CHUNK == 0

    w = w_in[0]
    w_kvqz = jnp.concatenate([w[:, :OFF_BA], w[:, OFF_Q:OFF_U]], axis=1).astype(BF16)
    w_kv = w[:, :OFF_BA].astype(BF16)
    w_ba = jnp.pad(w[:, OFF_BA:OFF_Q], ((0, 0), (0, LANES - 4 * N_HEADS))).astype(BF16)
    w_uv = w[:, OFF_U:OFF_GA].astype(BF16)
    w_gates = w[:, OFF_GA:].astype(BF16)
    lane_pad = (2 * N_HEADS, LANES - 4 * N_HEADS)
    a_lane = jnp.pad(jnp.exp(a_log[0].astype(F32)).reshape(-1), lane_pad).reshape(1, LANES)
    dt_lane = jnp.pad(dt_bias[0].astype(F32).reshape(-1), lane_pad).reshape(1, LANES)
    bs_cols = jnp.repeat(a_bs[0].T, LANES, axis=1)

    cond8 = jnp.concatenate([c, c_ctx[None, :], jnp.zeros((7 - bsz, D_MODEL), F32)], axis=0)
    mod3 = _mod_call(cond8, w_mod[0], b_mod[0]).reshape(8, 1, 6 * D_MODEL)

    x2d = x.reshape(bsz * length, D_MODEL)
    g1 = norm1_g[0].reshape(1, D_MODEL)
    tm_in = 1024
    tiles_per_seq = length // tm_in
    p2d, gcol, grow = _inproj_call(x2d, mod3, lambda i: i // tiles_per_seq, g1, w_kvqz, w_ba, conv_qkv[0],
                                   a_lane, dt_lane, ('k', 'v', 'q', 'z'), tm_in, tiles_per_seq, "in_proj")
    pc2d, cgcol, cgrow = _inproj_call(ctx.reshape(bsz * ctx_len, D_MODEL), mod3, lambda i: bsz, g1, w_kv, w_ba,
                                      conv_qkv[0], a_lane, dt_lane, ('k', 'v'), ctx_len, 1, "in_proj_ctx")
    p3 = p2d.reshape(bsz, length, 4 * B_WIDTH)
    pc3 = pc2d.reshape(bsz, ctx_len, 2 * B_WIDTH)
    og = _delta_call(p3, gcol.reshape(bsz, length, LANES), grow, pc3, cgcol.reshape(bsz, ctx_len, LANES), cgrow,
                     onorm_g[0].reshape(1, HEAD_DIM))

    tm = 512
    tiles_per_batch = length // tm
    x1 = _mix_call(x2d, og.reshape(bsz * length, B_WIDTH), mod3, g1, w_uv, w_gates,
                   a_ln_g[0].reshape(1, -1), a_ln_b[0].reshape(1, -1), a_ws[0].astype(BF16), bs_cols,
                   w_proj_a[0].astype(BF16), w_proj_b[0].astype(BF16), w_out[0].astype(BF16), tm, tiles_per_batch)
    out = _ffn_call(x1, mod3, norm2_g[0].reshape(1, -1), w_up[0].astype(BF16), ffn_conv_w[0],
                    ffn_conv_b[0].reshape(1, -1), w_down[0].astype(BF16), final_g.reshape(1, -1),
                    tm, 2, tiles_per_batch)
    return out.reshape(bsz, length, D_MODEL)
```

```python
import functools

import jax
import jax.numpy as jnp
from jax import lax
from jax.experimental import pallas as pl
from jax.experimental.pallas import tpu as pltpu

F32 = jnp.float32
BF16 = jnp.bfloat16

D_MODEL = 1024
GRID_W = 64
NORM_EPS = 1e-6
N_HEADS = 8
HEAD_DIM = 128
B_WIDTH = N_HEADS * HEAD_DIM
A_GROUPS = 8
A_CHUNK = 128
D_FF = 2816
OFF_BA = 2 * B_WIDTH
OFF_Q = OFF_BA + 4 * N_HEADS
OFF_U = OFF_Q + 2 * B_WIDTH
OFF_GA = OFF_U + 2 * D_MODEL
LANES = 128
MXU_COLS = 256
CHUNK = 128
GROUP = 8

VMEM_LIMIT = 56 * 1024 * 1024


def _silu(x):
    return x * jax.nn.sigmoid(x)


def _dot(a, b):
    return jnp.dot(a, b, preferred_element_type=F32)


def _dot_nt(a, b):
    return lax.dot_general(a, b, (((1,), (1,)), ((), ())), preferred_element_type=F32)


def _mod_kernel(c_ref, w_ref, b_ref, o_ref):
    cond = _silu(c_ref[...])
    o_ref[...] = jnp.dot(cond, w_ref[...], preferred_element_type=F32,
                         precision=lax.Precision.HIGHEST) + b_ref[...]


def _mod_call(cond8, w_mod, b_mod):
    n = w_mod.shape[1]
    tn = 1536
    return pl.pallas_call(
        _mod_kernel,
        grid=(n // tn,),
        in_specs=[pl.BlockSpec((8, D_MODEL), lambda j: (0, 0)),
                  pl.BlockSpec((D_MODEL, tn), lambda j: (0, j)),
                  pl.BlockSpec((1, tn), lambda j: (0, j))],
        out_specs=pl.BlockSpec((8, tn), lambda j: (0, j)),
        out_shape=jax.ShapeDtypeStruct((8, n), F32),
        name="mod",
    )(cond8, w_mod, b_mod.reshape(1, n))


TAIL_BLOCK = 512


def _tail_kernel(a_ref, b_ref, o_ref, *, lane_offset):
    both = jnp.concatenate([a_ref[...], b_ref[...]], axis=1)
    o_ref[...] = both[:, lane_offset:lane_offset + TAIL_BLOCK].astype(o_ref.dtype)


def _tail_call(w):
    k, n_in = w.shape
    n_out = n_in - OFF_Q
    first, lane_offset = divmod(OFF_Q, LANES)
    assert n_out % TAIL_BLOCK == 0 and (first * LANES) % TAIL_BLOCK == 0 and 0 < lane_offset < LANES
    first_a = first * LANES // TAIL_BLOCK
    per_block = TAIL_BLOCK // LANES
    return pl.pallas_call(
        functools.partial(_tail_kernel, lane_offset=lane_offset),
        grid=(n_out // TAIL_BLOCK,),
        in_specs=[pl.BlockSpec((k, TAIL_BLOCK), lambda j: (0, first_a + j)),
                  pl.BlockSpec((k, LANES), lambda j: (0, first + (j + 1) * per_block))],
        out_specs=pl.BlockSpec((k, TAIL_BLOCK), lambda j: (0, j)),
        out_shape=jax.ShapeDtypeStruct((k, n_out), BF16),
        name="w_tail",
    )(w, w)


def _norm_mod(x, g, shift, scale):
    ms = jnp.mean(x * x, axis=-1, keepdims=True)
    y = x * lax.rsqrt(ms + NORM_EPS) * g
    return y * (1.0 + scale) + shift


def _l2norm(t):
    return t * lax.rsqrt(jnp.sum(t * t, axis=-1, keepdims=True) + NORM_EPS)


def _gate_columns(ba, a_lane, dt_lane):
    n_sub = ba.shape[0] // CHUNK
    lane = lax.broadcasted_iota(jnp.int32, ba.shape, 1)
    ri = lax.broadcasted_iota(jnp.int32, (CHUNK, CHUNK), 0)
    ci = lax.broadcasted_iota(jnp.int32, (CHUNK, CHUNK), 1)
    tri_lo, tri_up = (ri >= ci).astype(F32), (ri <= ci).astype(F32)
    beta = jax.nn.sigmoid(ba)
    y = ba + dt_lane
    softplus = jnp.maximum(y, 0.0) + jnp.log1p(jnp.exp(-jnp.abs(y)))
    g = jnp.where((lane >= 2 * N_HEADS) & (lane < 4 * N_HEADS), -a_lane * softplus, 0.0)
    part = 4 * N_HEADS
    hi = g.astype(BF16).astype(F32)
    mid = (g - hi).astype(BF16).astype(F32)
    lo = (g - hi - mid).astype(BF16).astype(F32)
    packed = (hi + pltpu.roll(mid, part, 1) + pltpu.roll(lo, 2 * part, 1)).astype(BF16)

    def unpack(r):
        return r + pltpu.roll(r, LANES - part, 1) + pltpu.roll(r, LANES - 2 * part, 1)

    tri_lo, tri_up = tri_lo.astype(BF16), tri_up.astype(BF16)
    parts = [packed[s * CHUNK:(s + 1) * CHUNK, :] for s in range(n_sub)]
    pre = [unpack(_dot(tri_lo, x)) for x in parts]
    suf = [unpack(_dot(tri_up, x)) for x in parts]
    lane = lax.broadcasted_iota(jnp.int32, (CHUNK, LANES), 1)
    return [jnp.where(lane < 2 * N_HEADS, beta[s * CHUNK:(s + 1) * CHUNK, :],
                      jnp.where(lane < 3 * N_HEADS, pre[s], suf[s])) for s in range(n_sub)]


def _resident(shape):
    return pl.BlockSpec(shape, lambda i: (0,) * len(shape), pipeline_mode=pl.Buffered(1))


def _slab_order(kinds, slabs_per_group):
    heavy = [(g, s) for g, kind in enumerate(kinds) if kind != 'z' for s in range(slabs_per_group)]
    light = [(g, s) for g, kind in enumerate(kinds) if kind == 'z' for s in range(slabs_per_group)]
    every = max(len(heavy) // max(len(light), 1), 1)
    order = []
    for n, item in enumerate(heavy):
        order.append(item)
        if light and (n + 1) % every == 0:
            order.append(light.pop(0))
    return order + light


def _inproj_kernel(x_ref, xb_ref, xa_ref, mod_ref, g_ref, wba_ref, cw_ref, a_ref, dt_ref, *rest,
                   tiles_per_seq, kinds):
    n_w = (len(kinds) + 1) // 2
    w_refs = rest[:n_w]
    o_ref, gcol_ref, grow_ref, h_ref, halo_ref = rest[n_w:]
    i = pl.program_id(0)
    tm = o_ref.shape[0]
    shift, scale = mod_ref[:, 0:D_MODEL], mod_ref[:, D_MODEL:2 * D_MODEL]
    h_ref[...] = _norm_mod(x_ref[...], g_ref[...], shift, scale).astype(BF16)
    halo = jnp.concatenate([xb_ref[...], xa_ref[...]], axis=0)
    halo_ref[...] = _norm_mod(halo, g_ref[...], shift, scale).astype(BF16)
    ba = _dot(h_ref[...], wba_ref[...])
    for s, col in enumerate(_gate_columns(ba, a_ref[...], dt_ref[...])):
        rows = slice(s * CHUNK, (s + 1) * CHUNK)
        gcol_ref[rows, :] = col
        grow_ref[:, rows] = col.T

    tile_in_seq = i % tiles_per_seq
    slab = MXU_COLS
    row8 = lax.broadcasted_iota(jnp.int32, (8, slab), 0)
    n_parts = max(tm // 256, 1)
    part = tm // n_parts
    tiled = (part // 8, 8, slab)
    for g, s in _slab_order(kinds, B_WIDTH // slab):
        kind = kinds[g]
        col0 = g * B_WIDTH + s * slab
        cols = slice(col0, col0 + slab)
        w_cols = slice(col0 % (2 * B_WIDTH), col0 % (2 * B_WIDTH) + slab)
        w_ref = w_refs[g // 2]
        ps = [_dot(h_ref[n * part:(n + 1) * part, :], w_ref[:, w_cols]) for n in range(n_parts)]
        if kind != 'z':
            edge = _dot(halo_ref[...], w_ref[:, w_cols])
            taps = [jnp.broadcast_to(cw_ref[r:r + 1, cols], (8, slab)) for r in range(3)]
        for n, p in enumerate(ps):
            if kind != 'z':
                prev_row = ps[n - 1][part - 1:part] if n > 0 else jnp.where(tile_in_seq > 0, edge[7:8, :], 0.0)
                next_row = (ps[n + 1][0:1] if n + 1 < n_parts
                            else jnp.where(tile_in_seq < tiles_per_seq - 1, edge[8:9, :], 0.0))
                xp = pltpu.roll(p, 1, 0)
                xp = jnp.concatenate([jnp.where(row8 == 0, prev_row, xp[0:8]), xp[8:]], axis=0)
                xn = pltpu.roll(p, part - 1, 0)
                xn = jnp.concatenate([xn[:part - 8], jnp.where(row8 == 7, next_row, xn[part - 8:])], axis=0)
                p = (xp.reshape(tiled) * taps[0] + p.reshape(tiled) * taps[1]
                     + xn.reshape(tiled) * taps[2]).reshape(part, slab)
            y = _silu(p)
            rows = slice(n * part, (n + 1) * part)
            for hd in range(slab // HEAD_DIM):
                head = y[:, hd * HEAD_DIM:(hd + 1) * HEAD_DIM]
                if kind in ('k', 'q'):
                    head = _l2norm(head)
                if kind == 'q':
                    head = head * (HEAD_DIM ** -0.5)
                o_ref[rows, col0 + hd * HEAD_DIM:col0 + (hd + 1) * HEAD_DIM] = head.astype(o_ref.dtype)


def _inproj_call(x2d, mod3, mod_row_of_tile, norm_g, w_pairs, w_ba, conv_w, a_lane, dt_lane, kinds, tm,
                 tiles_per_seq, name):
    m = x2d.shape[0]
    n = B_WIDTH * len(kinds)
    assert len(kinds) == 2 * len(w_pairs) and tm % CHUNK == 0
    assert all(kind != 'z' for kind in kinds[:conv_w.shape[1] // B_WIDTH]) and 'z' not in kinds[:-1]
    last_halo = m // 8 - 1

    def pair_spec(block):
        return pl.BlockSpec((D_MODEL, 2 * B_WIDTH), lambda i: (0, block), pipeline_mode=pl.Buffered(1))

    return pl.pallas_call(
        functools.partial(_inproj_kernel, tiles_per_seq=tiles_per_seq, kinds=kinds),
        grid=(m // tm,),
        in_specs=[pl.BlockSpec((tm, D_MODEL), lambda i: (i, 0)),
                  pl.BlockSpec((8, D_MODEL), lambda i: (jnp.maximum(i * (tm // 8) - 1, 0), 0)),
                  pl.BlockSpec((8, D_MODEL), lambda i: (jnp.minimum((i + 1) * (tm // 8), last_halo), 0)),
                  pl.BlockSpec((None, 1, 6 * D_MODEL), lambda i: (mod_row_of_tile(i), 0, 0)),
                  _resident((1, D_MODEL)),
                  _resident((D_MODEL, LANES)),
                  _resident(conv_w.shape),
                  _resident((1, LANES)), _resident((1, LANES))] + [pair_spec(blk) for _, blk in w_pairs],
        out_specs=[pl.BlockSpec((tm, n), lambda i: (i, 0)),
                   pl.BlockSpec((tm, LANES), lambda i: (i, 0)),
                   pl.BlockSpec((LANES, tm), lambda i: (0, i))],
        out_shape=[jax.ShapeDtypeStruct((m, n), BF16), jax.ShapeDtypeStruct((m, LANES), F32),
                   jax.ShapeDtypeStruct((LANES, m), F32)],
        scratch_shapes=[pltpu.VMEM((tm, D_MODEL), BF16), pltpu.VMEM((16, D_MODEL), BF16)],
        compiler_params=pltpu.CompilerParams(
            dimension_semantics=("parallel",), vmem_limit_bytes=VMEM_LIMIT),
        name=name,
    )(x2d, x2d, x2d, mod3, norm_g, w_ba, conv_w, a_lane, dt_lane, *[arr for arr, _ in w_pairs])


def _run_interleaved(*gens):
    results = [None] * len(gens)
    live = list(range(len(gens)))
    while live:
        for idx in list(live):
            try:
                next(gens[idx])
            except StopIteration as stop:
                results[idx] = stop.value
                live.remove(idx)
    return results


TRI_INCL, TRI_STRICT, TRI_BLK16, TRI_EYE = 0, 2, 4, 5
MERGE_SIZES = (32, 64, 128)


def _mask_constants():
    ri = lax.broadcasted_iota(jnp.int32, (CHUNK, CHUNK), 0)
    ci = lax.broadcasted_iota(jnp.int32, (CHUNK, CHUNK), 1)

    def same_block(size):
        return (ri // size) == (ci // size)

    tri = jnp.stack([ri >= ci, ri <= ci, ri > ci, ri < ci, same_block(16), ri == ci]).astype(F32)
    off = jnp.stack([same_block(s) & ~same_block(s // 2) for s in MERGE_SIZES]).astype(BF16)
    return tri, off


def _tri_inverse_stages(n_mats, tri_ref, off_ref):
    n16 = [n * tri_ref[TRI_BLK16] for n in n_mats]
    ts = [tri_ref[TRI_EYE] + n for n in n16]
    pbs = [n.astype(BF16) for n in n16]
    pbs = [_dot(p, p).astype(BF16) for p in pbs]
    nbs = [n.astype(BF16) for n in n_mats]
    yield
    for _ in range(2):
        ts = [t + _dot(t.astype(BF16), p) for t, p in zip(ts, pbs)]
        pbs = [_dot(p, p).astype(BF16) for p in pbs]
        yield
    ts = [t + _dot(t.astype(BF16), p) for t, p in zip(ts, pbs)]
    yield
    tbs = [t.astype(BF16) for t in ts]
    for level in range(len(MERGE_SIZES)):
        xs = [_dot(nb * off_ref[level], tb).astype(BF16) for nb, tb in zip(nbs, tbs)]
        yield
        tbs = [tb + _dot(tb, x).astype(BF16) for tb, x in zip(tbs, xs)]
        yield
    return tbs


def _chunk_start(c):
    return c * CHUNK if isinstance(c, int) else pl.multiple_of(c * CHUNK, CHUNK)


def _prep_stages(chunks, h, k_ref, v_ref, q_ref, gcol_ref, grow_ref, tri_ref, off_ref, early, late):
    lane = lax.broadcasted_iota(jnp.int32, (CHUNK, LANES), 1)
    head_row = lax.broadcasted_iota(jnp.int32, (N_HEADS, CHUNK), 0)

    acts, prods = [], []
    for c in chunks:
        rows = pl.ds(_chunk_start(c), CHUNK)
        kb16 = k_ref[rows, :]
        k = kb16.astype(F32)
        kt16 = k.T.astype(BF16)
        v = v_ref[rows, :].astype(F32)
        if q_ref is not None:
            qb16 = q_ref[rows, :]
            aa = _dot(jnp.concatenate([kb16, qb16], axis=0), kt16)
            acts.append((k, v, qb16.astype(F32)))
            prods.append((aa[:CHUNK], aa[CHUNK:]))
        else:
            acts.append((k, v, None))
            prods.append((_dot(kb16, kt16), None))
    yield

    neg_ms, rhss = [], []
    for i, (c, (k, v, q), (a_kk, a_qk)) in enumerate(zip(chunks, acts, prods)):
        r0 = _chunk_start(c)
        gates = gcol_ref[pl.ds(r0, CHUNK), :]
        for d in range(2):
            beta = jnp.sum(jnp.where(lane == d * N_HEADS + h, gates, 0.0), axis=1, keepdims=True)
            gc = jnp.sum(jnp.where(lane == (2 + d) * N_HEADS + h, gates, 0.0), axis=1, keepdims=True)
            all_heads = grow_ref[(2 + d) * N_HEADS:(3 + d) * N_HEADS, pl.ds(r0, CHUNK)]
            gc_row = jnp.sum(jnp.where(head_row == h, all_heads, 0.0), axis=0, keepdims=True)
            decay_ij = jnp.exp(jnp.minimum(gc - gc_row, 0.0))
            neg_ms.append(a_kk * (-beta) * (decay_ij * tri_ref[TRI_STRICT + d]))
            eg = jnp.exp(gc)
            rhss.append(jnp.concatenate([k * (beta * eg), v * beta], axis=1).astype(BF16))
            g_end = gc[CHUNK - 1:CHUNK, :] if d == 0 else gc[0:1, :]
            kd = k * jnp.exp(g_end - gc)
            decay = jnp.broadcast_to(jnp.exp(g_end), (8, LANES))
            if q is not None:
                early(i, d, kd, decay, q * eg, a_qk * (decay_ij * tri_ref[TRI_INCL + d]))
            else:
                early(i, d, kd, decay, None, None)

    t_invs = yield from _tri_inverse_stages(neg_ms, tri_ref, off_ref)
    wus = [_dot(t, rhs) for t, rhs in zip(t_invs, rhss)]
    yield
    for n, wu in enumerate(wus):
        late(n // 2, n % 2, wu[:, :HEAD_DIM], wu[:, HEAD_DIM:])


def _delta_kernel(pk_ref, pv_ref, pq_ref, gcol_ref, grow_ref, ck_ref, cv_ref, cgcol_ref, cgrow_ref,
                  sz_ref, onorm_ref, tri_ref, off_ref, og_ref,
                  wq_sc, kq_sc, u_sc, ge_sc, s0_sc, o_sc, *, n_chunks, n_ctx_chunks, n_heads_total):
    t = pl.program_id(0)
    h = jnp.minimum(t, n_heads_total - 1) % N_HEADS
    slot_p = t % 2
    slot_s = 1 - slot_p

    @pl.when(t == 0)
    def _():
        wq_sc[1] = jnp.zeros(wq_sc.shape[1:], wq_sc.dtype)
        kq_sc[1] = jnp.zeros(kq_sc.shape[1:], kq_sc.dtype)
        u_sc[1] = jnp.zeros(u_sc.shape[1:], u_sc.dtype)
        ge_sc[1] = jnp.zeros(ge_sc.shape[1:], ge_sc.dtype)
        s0_sc[1] = jnp.zeros(s0_sc.shape[1:], s0_sc.dtype)

    def context_states():
        kept = {}

        def early(c, d, kd, decay, qg, qk):
            kept[c, d] = [kd.T.astype(BF16), decay]

        def late(c, d, w, u):
            kept[c, d] += [w.astype(BF16), u]

        yield from _prep_stages(list(range(n_ctx_chunks)), h, ck_ref, cv_ref, None, cgcol_ref, cgrow_ref,
                                tri_ref, off_ref, early, late)
        states = [jnp.zeros((HEAD_DIM, HEAD_DIM), F32)] * 2
        for c in range(n_ctx_chunks):
            steps = [kept[c, 0], kept[n_ctx_chunks - 1 - c, 1]]
            ws = [_dot(w, s.astype(BF16)) for (_, _, w, _), s in zip(steps, states)]
            yield
            kv = [_dot(kd_t, (u - x).astype(BF16)) for (kd_t, _, _, u), x in zip(steps, ws)]
            yield
            states = [s * decay[0:1, :] + x for s, (_, decay, _, _), x in zip(states, steps, kv)]
        for d in range(2):
            s0_sc[slot_p, d] = states[d]

    def prepare(i):
        chunks = [i * GROUP + g for g in range(GROUP)]

        def early(n, d, kd, decay, qg, qk):
            c = chunks[n]
            kq_sc[slot_p, d, c, 0:CHUNK, :] = kd.T.astype(BF16)
            kq_sc[slot_p, d, c, CHUNK:2 * CHUNK, :] = qk.astype(BF16)
            wq_sc[slot_p, d, c, CHUNK:2 * CHUNK, :] = qg.astype(BF16)
            ge_sc[slot_p, d, c] = decay

        def late(n, d, w, u):
            c = chunks[n]
            wq_sc[slot_p, d, c, 0:CHUNK, :] = w.astype(BF16)
            u_sc[slot_p, d, pl.ds(_chunk_start(c), CHUNK), :] = u

        yield from _prep_stages(chunks, h, pk_ref, pv_ref, pq_ref, gcol_ref, grow_ref, tri_ref, off_ref,
                                early, late)

    def finish(c, o):
        r0 = _chunk_start(c)
        y = o * lax.rsqrt(jnp.mean(o * o, axis=-1, keepdims=True) + NORM_EPS) * onorm_ref[...]
        og_ref[pl.ds(r0, CHUNK), :] = (y * sz_ref[pl.ds(r0, CHUNK), :].astype(F32)).astype(og_ref.dtype)

    def scan(i, states, second_half):
        for g in range(GROUP):
            j = i * GROUP + g
            chunks = (j, n_chunks - 1 - j)
            ws = [_dot(wq_sc[slot_s, d, c], s.astype(BF16)) for d, (c, s) in enumerate(zip(chunks, states))]
            yield
            v_new = [u_sc[slot_s, d, pl.ds(_chunk_start(c), CHUNK), :] - x[:CHUNK]
                     for d, (c, x) in enumerate(zip(chunks, ws))]
            kv = [_dot(kq_sc[slot_s, d, c], vn.astype(BF16)) for d, (c, vn) in enumerate(zip(chunks, v_new))]
            yield
            states = [s * ge_sc[slot_s, d, c][0:1, :] + x[:CHUNK]
                      for d, (c, s, x) in enumerate(zip(chunks, states, kv))]
            for c, x, y in zip(chunks, ws, kv):
                o = x[CHUNK:] + y[CHUNK:]
                rows = pl.ds(_chunk_start(c), CHUNK)
                if second_half:
                    finish(c, o_sc[rows, :] + o)
                else:
                    o_sc[rows, :] = o
        return tuple(states)

    def body(second_half, i, states, *also):
        return _run_interleaved(scan(i, states, second_half), prepare(i), *also)[0]

    n_iter = n_chunks // GROUP
    states = (s0_sc[slot_s, 0], s0_sc[slot_s, 1])
    states = body(False, 0, states, context_states())
    for i in range(1, n_iter):
        states = body(i >= n_iter // 2, i, states)


def _delta_call(p3, gcol, grow, pc3, cgcol, cgrow, onorm_g):
    bsz, length, _ = p3.shape
    ctx_len = pc3.shape[1]
    n_chunks = length // CHUNK
    n_ctx_chunks = ctx_len // CHUNK
    n_heads_total = bsz * N_HEADS
    assert n_chunks % (2 * GROUP) == 0
    tri_masks, off_masks = _mask_constants()

    def prepared(t):
        t = jnp.minimum(t, n_heads_total - 1)
        return t // N_HEADS, t % N_HEADS

    def scanned(t):
        t = jnp.maximum(t - 1, 0)
        return t // N_HEADS, t % N_HEADS

    def head_cols(rows, first_block, which):
        def index(t):
            b, h = which(t)
            return b, 0, first_block + h
        return pl.BlockSpec((None, rows, LANES), index)

    def gates_col(rows):
        return pl.BlockSpec((None, rows, LANES), lambda t: (prepared(t)[0], 0, 0), pipeline_mode=pl.Buffered(1))

    def gates_row(rows):
        return pl.BlockSpec((LANES, rows), lambda t: (0, prepared(t)[0]), pipeline_mode=pl.Buffered(1))

    return pl.pallas_call(
        functools.partial(_delta_kernel, n_chunks=n_chunks, n_ctx_chunks=n_ctx_chunks,
                          n_heads_total=n_heads_total),
        grid=(n_heads_total + 1,),
        in_specs=[head_cols(length, 0, prepared), head_cols(length, N_HEADS, prepared),
                  head_cols(length, 2 * N_HEADS, prepared),
                  gates_col(length), gates_row(length),
                  head_cols(ctx_len, 0, prepared), head_cols(ctx_len, N_HEADS, prepared),
                  gates_col(ctx_len), gates_row(ctx_len),
                  head_cols(length, 3 * N_HEADS, scanned),
                  pl.BlockSpec((1, LANES), lambda t: (0, 0)),
                  pl.BlockSpec(tri_masks.shape, lambda t: (0, 0, 0), pipeline_mode=pl.Buffered(1)),
                  pl.BlockSpec(off_masks.shape, lambda t: (0, 0, 0), pipeline_mode=pl.Buffered(1))],
        out_specs=head_cols(length, 0, scanned),
        out_shape=jax.ShapeDtypeStruct((bsz, length, B_WIDTH), BF16),
        scratch_shapes=[pltpu.VMEM((2, 2, n_chunks, 2 * CHUNK, LANES), BF16),
                        pltpu.VMEM((2, 2, n_chunks, 2 * CHUNK, LANES), BF16),
                        pltpu.VMEM((2, 2, length, LANES), F32),
                        pltpu.VMEM((2, 2, n_chunks, 8, LANES), F32),
                        pltpu.VMEM((2, 2, HEAD_DIM, HEAD_DIM), F32),
                        pltpu.VMEM((length, LANES), F32)],
        compiler_params=pltpu.CompilerParams(
            dimension_semantics=("arbitrary",), vmem_limit_bytes=62 * 1024 * 1024),
        name="delta",
    )(p3, p3, p3, gcol, grow, pc3, pc3, cgcol, cgrow, p3, onorm_g, tri_masks, off_masks)


def _mix_kernel(x_ref, og_ref, mod_ref, g1_ref, wuv_ref, wg_ref, lng_ref, lnb_ref, ws_ref, bs_ref,
                wa_ref, wb_ref, wo_ref, o_ref, ua_ref, *, n_sub):
    x = x_ref[...]
    h = _norm_mod(x, g1_ref[...], mod_ref[:, 0:D_MODEL], mod_ref[:, D_MODEL:2 * D_MODEL]).astype(BF16)
    uv = _dot(h, wuv_ref[...])
    gates_pre = _dot(h, wg_ref[...])
    y_b = _dot(og_ref[...], wb_ref[...])
    v = jax.nn.gelu(uv[:, D_MODEL:])
    vc = v - jnp.mean(v, axis=-1, keepdims=True)
    vn = vc * lax.rsqrt(jnp.mean(vc * vc, axis=-1, keepdims=True) + NORM_EPS) * lng_ref[...] + lnb_ref[...]
    vb = vn.astype(BF16)
    for n in range(n_sub):
        rows = slice(n * A_CHUNK, (n + 1) * A_CHUNK)
        for g in range(A_GROUPS):
            cols = slice(g * LANES, (g + 1) * LANES)
            s = _dot(ws_ref[g], vb[rows, cols]) + bs_ref[:, cols]
            ua_ref[rows, cols] = (jax.nn.gelu(uv[rows, cols]) * s).astype(BF16)
    y_a = _dot(ua_ref[...], wa_ref[...])
    gates = jax.nn.sigmoid(gates_pre)
    t = gates[:, :D_MODEL] * y_a + gates[:, D_MODEL:] * y_b
    y = _dot(t.astype(BF16), wo_ref[...])
    o_ref[...] = x + mod_ref[:, 2 * D_MODEL:3 * D_MODEL] * y


def _mix_call(x2d, og, mod3, g1, w_tail16, ln_g, ln_b, ws16, bs_cols, wa16, wb16, wo16, tm, tiles_per_batch):
    m = x2d.shape[0]
    rows = pl.BlockSpec((tm, D_MODEL), lambda i: (i, 0))

    def tail_block(block):
        return pl.BlockSpec((D_MODEL, 2 * D_MODEL), lambda i: (0, block), pipeline_mode=pl.Buffered(1))

    return pl.pallas_call(
        functools.partial(_mix_kernel, n_sub=tm // A_CHUNK),
        grid=(m // tm,),
        in_specs=[rows, rows,
                  pl.BlockSpec((None, 1, 6 * D_MODEL), lambda i: (i // tiles_per_batch, 0, 0)),
                  _resident((1, D_MODEL)),
                  tail_block(1), tail_block(2),
                  _resident((1, D_MODEL)), _resident((1, D_MODEL)),
                  _resident((A_GROUPS, A_CHUNK, A_CHUNK)), _resident((A_CHUNK, D_MODEL)),
                  _resident((D_MODEL, D_MODEL)), _resident((D_MODEL, D_MODEL)), _resident((D_MODEL, D_MODEL))],
        out_specs=rows,
        out_shape=jax.ShapeDtypeStruct((m, D_MODEL), F32),
        scratch_shapes=[pltpu.VMEM((tm, D_MODEL), BF16)],
        compiler_params=pltpu.CompilerParams(
            dimension_semantics=("parallel",), vmem_limit_bytes=VMEM_LIMIT),
        name="mix",
    )(x2d, og, mod3, g1, w_tail16, w_tail16, ln_g, ln_b, ws16, bs_cols, wa16, wb16, wo16)


def _ffn_kernel(x_ref, mod_ref, g2_ref, wup_ref, cw_ref, cb_ref, wd_ref, gf_ref, o_ref, *, n_parts):
    tm = x_ref.shape[0]
    part = tm // n_parts
    col_in_row = lax.broadcasted_iota(jnp.int32, (part, D_FF), 0) & (GRID_W - 1)
    xs, ups = [], []
    for n in range(n_parts):
        x = x_ref[n * part:(n + 1) * part, :]
        h = _norm_mod(x, g2_ref[...], mod_ref[:, 3 * D_MODEL:4 * D_MODEL],
                      mod_ref[:, 4 * D_MODEL:5 * D_MODEL]).astype(BF16)
        xs.append(x)
        ups.append((_dot(h, wup_ref[:, :D_FF]), _dot(h, wup_ref[:, D_FF:])))
    for n, (x, (a, b)) in enumerate(zip(xs, ups)):
        a_prev = jnp.where(col_in_row == 0, 0.0, pltpu.roll(a, 1, 0))
        a_next = jnp.where(col_in_row == GRID_W - 1, 0.0, pltpu.roll(a, part - 1, 0))
        ac = a_prev * cw_ref[0:1, :] + a * cw_ref[1:2, :] + a_next * cw_ref[2:3, :] + cb_ref[...]
        y = _dot((jax.nn.gelu(ac) * b).astype(BF16), wd_ref[...])
        x2 = x + mod_ref[:, 5 * D_MODEL:6 * D_MODEL] * y
        ms = jnp.mean(x2 * x2, axis=-1, keepdims=True)
        o_ref[n * part:(n + 1) * part, :] = x2 * lax.rsqrt(ms + NORM_EPS) * gf_ref[...]


def _ffn_call(x1, mod3, norm2_g, w_up16, conv_w, conv_b, w_down16, final_g, tm, n_parts, tiles_per_batch):
    m = x1.shape[0]
    assert tm % (n_parts * GRID_W) == 0
    return pl.pallas_call(
        functools.partial(_ffn_kernel, n_parts=n_parts),
        grid=(m // tm,),
        in_specs=[pl.BlockSpec((tm, D_MODEL), lambda i: (i, 0)),
                  pl.BlockSpec((None, 1, 6 * D_MODEL), lambda i: (i // tiles_per_batch, 0, 0)),
                  _resident((1, D_MODEL)),
                  _resident((D_MODEL, 2 * D_FF)),
                  _resident((3, D_FF)), _resident((1, D_FF)),
                  _resident((D_FF, D_MODEL)),
                  _resident((1, D_MODEL))],
        out_specs=pl.BlockSpec((tm, D_MODEL), lambda i: (i, 0)),
        out_shape=jax.ShapeDtypeStruct((m, D_MODEL), F32),
        compiler_params=pltpu.CompilerParams(
            dimension_semantics=("parallel",), vmem_limit_bytes=VMEM_LIMIT),
        name="ffn",
    )(x1, mod3, norm2_g, w_up16, conv_w, conv_b, w_down16, final_g)


def kernel(x, c, ctx, c_ctx, w_mod, b_mod, norm1_g, w_in, conv_qkv, a_log, dt_bias, onorm_g, w_proj_b,
           a_ln_g, a_ln_b, a_ws, a_bs, w_proj_a, w_out, norm2_g, w_up, ffn_conv_w, ffn_conv_b, w_down,
           final_g):
    bsz, length, _ = x.shape
    ctx_len = ctx.shape[1]
    assert w_mod.shape[0] == 1 and bsz <= 7
    assert length % 1024 == 0 and ctx_len % CHUNK == 0

    w = w_in[0]
    w_kv = w[:, :OFF_BA].astype(BF16)
    w_ba = jnp.pad(w[:, OFF_BA:OFF_Q], ((0, 0), (0, LANES - 4 * N_HEADS))).astype(BF16)
    w_tail = _tail_call(w)
    lane_pad = (2 * N_HEADS, LANES - 4 * N_HEADS)
    a_lane = jnp.pad(jnp.exp(a_log[0].astype(F32)).reshape(-1), lane_pad).reshape(1, LANES)
    dt_lane = jnp.pad(dt_bias[0].astype(F32).reshape(-1), lane_pad).reshape(1, LANES)
    bs_cols = jnp.repeat(a_bs[0].T, LANES, axis=1)

    cond8 = jnp.concatenate([c, c_ctx[None, :], jnp.zeros((7 - bsz, D_MODEL), F32)], axis=0)
    mod3 = _mod_call(cond8, w_mod[0], b_mod[0]).reshape(8, 1, 6 * D_MODEL)

    x2d = x.reshape(bsz * length, D_MODEL)
    g1 = norm1_g[0].reshape(1, D_MODEL)
    tm_in = 1024
    tiles_per_seq = length // tm_in
    p2d, gcol, grow = _inproj_call(x2d, mod3, lambda i: i // tiles_per_seq, g1, [(w_kv, 0), (w_tail, 0)], w_ba,
                                   conv_qkv[0], a_lane, dt_lane, ('k', 'v', 'q', 'z'), tm_in, tiles_per_seq,
                                   "in_proj")
    pc2d, cgcol, cgrow = _inproj_call(ctx.reshape(bsz * ctx_len, D_MODEL), mod3, lambda i: bsz, g1, [(w_kv, 0)],
                                      w_ba, conv_qkv[0], a_lane, dt_lane, ('k', 'v'), ctx_len, 1, "in_proj_ctx")
    p3 = p2d.reshape(bsz, length, 4 * B_WIDTH)
    pc3 = pc2d.reshape(bsz, ctx_len, 2 * B_WIDTH)
    og = _delta_call(p3, gcol.reshape(bsz, length, LANES), grow, pc3, cgcol.reshape(bsz, ctx_len, LANES), cgrow,
                     onorm_g[0].reshape(1, HEAD_DIM))

    tm = 512
    tiles_per_batch = length // tm
    x1 = _mix_call(x2d, og.reshape(bsz * length, B_WIDTH), mod3, g1, w_tail,
                   a_ln_g[0].reshape(1, -1), a_ln_b[0].reshape(1, -1), a_ws[0].astype(BF16), bs_cols,
                   w_proj_a[0].astype(BF16), w_proj_b[0].astype(BF16), w_out[0].astype(BF16), tm, tiles_per_batch)
    out = _ffn_call(x1, mod3, norm2_g[0].reshape(1, -1), w_up[0].astype(BF16), ffn_conv_w[0],
                    ffn_conv_b[0].reshape(1, -1), w_down[0].astype(BF16), final_g.reshape(1, -1),
                    tm, 2, tiles_per_batch)
    return out.reshape(bsz, length, D_MODEL)
```

```python
import functools

import jax
import jax.numpy as jnp
from jax import lax
from jax.experimental import pallas as pl
from jax.experimental.pallas import tpu as pltpu

F32 = jnp.float32
BF16 = jnp.bfloat16

D_MODEL = 1024
GRID_W = 64
NORM_EPS = 1e-6
N_HEADS = 8
HEAD_DIM = 128
B_WIDTH = N_HEADS * HEAD_DIM
A_GROUPS = 8
A_CHUNK = 128
D_FF = 2816
OFF_BA = 2 * B_WIDTH
OFF_Q = OFF_BA + 4 * N_HEADS
OFF_U = OFF_Q + 2 * B_WIDTH
OFF_GA = OFF_U + 2 * D_MODEL
LANES = 128
MXU_COLS = 256
CHUNK = 128
GROUP = 8

VMEM_LIMIT = 56 * 1024 * 1024


def _silu(x):
    return x * jax.nn.sigmoid(x)


def _dot(a, b):
    return jnp.dot(a, b, preferred_element_type=F32)


def _dot_nt(a, b):
    return lax.dot_general(a, b, (((1,), (1,)), ((), ())), preferred_element_type=F32)


def _mod_kernel(c_ref, w_ref, b_ref, o_ref):
    cond = _silu(c_ref[...])
    o_ref[...] = jnp.dot(cond, w_ref[...], preferred_element_type=F32,
                         precision=lax.Precision.HIGHEST) + b_ref[...]


def _mod_call(cond8, w_mod, b_mod):
    n = w_mod.shape[1]
    tn = 1536
    return pl.pallas_call(
        _mod_kernel,
        grid=(n // tn,),
        in_specs=[pl.BlockSpec((8, D_MODEL), lambda j: (0, 0)),
                  pl.BlockSpec((D_MODEL, tn), lambda j: (0, j)),
                  pl.BlockSpec((1, tn), lambda j: (0, j))],
        out_specs=pl.BlockSpec((8, tn), lambda j: (0, j)),
        out_shape=jax.ShapeDtypeStruct((8, n), F32),
        name="mod",
    )(cond8, w_mod, b_mod.reshape(1, n))


def _norm_mod(x, g, shift, scale):
    ms = jnp.mean(x * x, axis=-1, keepdims=True)
    y = x * lax.rsqrt(ms + NORM_EPS) * g
    return y * (1.0 + scale) + shift


def _l2norm(t):
    return t * lax.rsqrt(jnp.sum(t * t, axis=-1, keepdims=True) + NORM_EPS)


def _gate_columns(ba, a_lane, dt_lane):
    n_sub = ba.shape[0] // CHUNK
    lane = lax.broadcasted_iota(jnp.int32, ba.shape, 1)
    ri = lax.broadcasted_iota(jnp.int32, (CHUNK, CHUNK), 0)
    ci = lax.broadcasted_iota(jnp.int32, (CHUNK, CHUNK), 1)
    tri_lo, tri_up = (ri >= ci).astype(F32), (ri <= ci).astype(F32)
    beta = jax.nn.sigmoid(ba)
    y = ba + dt_lane
    softplus = jnp.maximum(y, 0.0) + jnp.log1p(jnp.exp(-jnp.abs(y)))
    g = jnp.where((lane >= 2 * N_HEADS) & (lane < 4 * N_HEADS), -a_lane * softplus, 0.0)
    part = 4 * N_HEADS
    hi = g.astype(BF16).astype(F32)
    mid = (g - hi).astype(BF16).astype(F32)
    lo = (g - hi - mid).astype(BF16).astype(F32)
    packed = (hi + pltpu.roll(mid, part, 1) + pltpu.roll(lo, 2 * part, 1)).astype(BF16)

    def unpack(r):
        return r + pltpu.roll(r, LANES - part, 1) + pltpu.roll(r, LANES - 2 * part, 1)

    tri_lo, tri_up = tri_lo.astype(BF16), tri_up.astype(BF16)
    parts = [packed[s * CHUNK:(s + 1) * CHUNK, :] for s in range(n_sub)]
    pre = [unpack(_dot(tri_lo, x)) for x in parts]
    suf = [unpack(_dot(tri_up, x)) for x in parts]
    lane = lax.broadcasted_iota(jnp.int32, (CHUNK, LANES), 1)
    return [jnp.where(lane < 2 * N_HEADS, beta[s * CHUNK:(s + 1) * CHUNK, :],
                      jnp.where(lane < 3 * N_HEADS, pre[s], suf[s])) for s in range(n_sub)]


def _resident(shape):
    return pl.BlockSpec(shape, lambda i: (0,) * len(shape), pipeline_mode=pl.Buffered(1))


def _slab_order(kinds, slabs_per_group):
    heavy = [(g, s) for g, kind in enumerate(kinds) if kind != 'z' for s in range(slabs_per_group)]
    light = [(g, s) for g, kind in enumerate(kinds) if kind == 'z' for s in range(slabs_per_group)]
    every = max(len(heavy) // max(len(light), 1), 1)
    order = []
    for n, item in enumerate(heavy):
        order.append(item)
        if light and (n + 1) % every == 0:
            order.append(light.pop(0))
    return order + light


def _inproj_kernel(x_ref, xb_ref, xa_ref, mod_ref, g_ref, wba_ref, cw_ref, a_ref, dt_ref, *rest,
                   tiles_per_seq, kinds):
    n_w = (len(kinds) + 1) // 2
    w_refs = rest[:n_w]
    o_ref, gcol_ref, grow_ref, h_ref, halo_ref = rest[n_w:]
    i = pl.program_id(0)
    tm = o_ref.shape[0]
    shift, scale = mod_ref[:, 0:D_MODEL], mod_ref[:, D_MODEL:2 * D_MODEL]
    h_ref[...] = _norm_mod(x_ref[...], g_ref[...], shift, scale).astype(BF16)
    halo = jnp.concatenate([xb_ref[...], xa_ref[...]], axis=0)
    halo_ref[...] = _norm_mod(halo, g_ref[...], shift, scale).astype(BF16)
    ba = _dot(h_ref[...], wba_ref[...])
    for s, col in enumerate(_gate_columns(ba, a_ref[...], dt_ref[...])):
        rows = slice(s * CHUNK, (s + 1) * CHUNK)
        gcol_ref[rows, :] = col
        grow_ref[:, rows] = col.T

    tile_in_seq = i % tiles_per_seq
    slab = MXU_COLS
    row8 = lax.broadcasted_iota(jnp.int32, (8, slab), 0)
    n_parts = max(tm // 256, 1)
    part = tm // n_parts
    tiled = (part // 8, 8, slab)
    for g, s in _slab_order(kinds, B_WIDTH // slab):
        kind = kinds[g]
        col0 = g * B_WIDTH + s * slab
        cols = slice(col0, col0 + slab)
        w_cols = slice(col0 % (2 * B_WIDTH), col0 % (2 * B_WIDTH) + slab)
        w_ref = w_refs[g // 2]
        ps = [_dot(h_ref[n * part:(n + 1) * part, :], w_ref[:, w_cols]) for n in range(n_parts)]
        if kind != 'z':
            edge = _dot(halo_ref[...], w_ref[:, w_cols])
            taps = [jnp.broadcast_to(cw_ref[r:r + 1, cols], (8, slab)) for r in range(3)]
        for n, p in enumerate(ps):
            if kind != 'z':
                prev_row = ps[n - 1][part - 1:part] if n > 0 else jnp.where(tile_in_seq > 0, edge[7:8, :], 0.0)
                next_row = (ps[n + 1][0:1] if n + 1 < n_parts
                            else jnp.where(tile_in_seq < tiles_per_seq - 1, edge[8:9, :], 0.0))
                xp = pltpu.roll(p, 1, 0)
                xp = jnp.concatenate([jnp.where(row8 == 0, prev_row, xp[0:8]), xp[8:]], axis=0)
                xn = pltpu.roll(p, part - 1, 0)
                xn = jnp.concatenate([xn[:part - 8], jnp.where(row8 == 7, next_row, xn[part - 8:])], axis=0)
                p = (xp.reshape(tiled) * taps[0] + p.reshape(tiled) * taps[1]
                     + xn.reshape(tiled) * taps[2]).reshape(part, slab)
            y = _silu(p)
            rows = slice(n * part, (n + 1) * part)
            for hd in range(slab // HEAD_DIM):
                head = y[:, hd * HEAD_DIM:(hd + 1) * HEAD_DIM]
                if kind in ('k', 'q'):
                    head = _l2norm(head)
                if kind == 'q':
                    head = head * (HEAD_DIM ** -0.5)
                o_ref[rows, col0 + hd * HEAD_DIM:col0 + (hd + 1) * HEAD_DIM] = head.astype(o_ref.dtype)


def _inproj_call(x2d, mod3, mod_row_of_tile, norm_g, w_pairs, w_ba, conv_w, a_lane, dt_lane, kinds, tm,
                 tiles_per_seq, name):
    m = x2d.shape[0]
    n = B_WIDTH * len(kinds)
    assert len(kinds) == 2 * len(w_pairs) and tm % CHUNK == 0
    assert all(kind != 'z' for kind in kinds[:conv_w.shape[1] // B_WIDTH]) and 'z' not in kinds[:-1]
    last_halo = m // 8 - 1

    def pair_spec(block):
        return pl.BlockSpec((D_MODEL, 2 * B_WIDTH), lambda i: (0, block), pipeline_mode=pl.Buffered(1))

    return pl.pallas_call(
        functools.partial(_inproj_kernel, tiles_per_seq=tiles_per_seq, kinds=kinds),
        grid=(m // tm,),
        in_specs=[pl.BlockSpec((tm, D_MODEL), lambda i: (i, 0)),
                  pl.BlockSpec((8, D_MODEL), lambda i: (jnp.maximum(i * (tm // 8) - 1, 0), 0)),
                  pl.BlockSpec((8, D_MODEL), lambda i: (jnp.minimum((i + 1) * (tm // 8), last_halo), 0)),
                  pl.BlockSpec((None, 1, 6 * D_MODEL), lambda i: (mod_row_of_tile(i), 0, 0)),
                  _resident((1, D_MODEL)),
                  _resident((D_MODEL, LANES)),
                  _resident(conv_w.shape),
                  _resident((1, LANES)), _resident((1, LANES))] + [pair_spec(blk) for _, blk in w_pairs],
        out_specs=[pl.BlockSpec((tm, n), lambda i: (i, 0)),
                   pl.BlockSpec((tm, LANES), lambda i: (i, 0)),
                   pl.BlockSpec((LANES, tm), lambda i: (0, i))],
        out_shape=[jax.ShapeDtypeStruct((m, n), BF16), jax.ShapeDtypeStruct((m, LANES), F32),
                   jax.ShapeDtypeStruct((LANES, m), F32)],
        scratch_shapes=[pltpu.VMEM((tm, D_MODEL), BF16), pltpu.VMEM((16, D_MODEL), BF16)],
        compiler_params=pltpu.CompilerParams(
            dimension_semantics=("parallel",), vmem_limit_bytes=VMEM_LIMIT),
        name=name,
    )(x2d, x2d, x2d, mod3, norm_g, w_ba, conv_w, a_lane, dt_lane, *[arr for arr, _ in w_pairs])


def _run_interleaved(*gens):
    results = [None] * len(gens)
    live = list(range(len(gens)))
    while live:
        for idx in list(live):
            try:
                next(gens[idx])
            except StopIteration as stop:
                results[idx] = stop.value
                live.remove(idx)
    return results


TRI_INCL, TRI_STRICT, TRI_BLK16, TRI_EYE = 0, 2, 4, 5
MERGE_SIZES = (32, 64, 128)


def _mask_constants():
    ri = lax.broadcasted_iota(jnp.int32, (CHUNK, CHUNK), 0)
    ci = lax.broadcasted_iota(jnp.int32, (CHUNK, CHUNK), 1)

    def same_block(size):
        return (ri // size) == (ci // size)

    tri = jnp.stack([ri >= ci, ri <= ci, ri > ci, ri < ci, same_block(16), ri == ci]).astype(F32)
    off = jnp.stack([same_block(s) & ~same_block(s // 2) for s in MERGE_SIZES]).astype(BF16)
    return tri, off


def _tri_inverse_stages(n_mats, tri_ref, off_ref):
    n16 = [n * tri_ref[TRI_BLK16] for n in n_mats]
    ts = [tri_ref[TRI_EYE] + n for n in n16]
    pbs = [n.astype(BF16) for n in n16]
    pbs = [_dot(p, p).astype(BF16) for p in pbs]
    nbs = [n.astype(BF16) for n in n_mats]
    yield
    for _ in range(2):
        ts = [t + _dot(t.astype(BF16), p) for t, p in zip(ts, pbs)]
        pbs = [_dot(p, p).astype(BF16) for p in pbs]
        yield
    ts = [t + _dot(t.astype(BF16), p) for t, p in zip(ts, pbs)]
    yield
    tbs = [t.astype(BF16) for t in ts]
    for level in range(len(MERGE_SIZES)):
        xs = [_dot(nb * off_ref[level], tb).astype(BF16) for nb, tb in zip(nbs, tbs)]
        yield
        tbs = [tb + _dot(tb, x).astype(BF16) for tb, x in zip(tbs, xs)]
        yield
    return tbs


def _chunk_start(c):
    return c * CHUNK if isinstance(c, int) else pl.multiple_of(c * CHUNK, CHUNK)


def _prep_stages(chunks, h, k_ref, v_ref, q_ref, gcol_ref, grow_ref, tri_ref, off_ref, early, late):
    lane = lax.broadcasted_iota(jnp.int32, (CHUNK, LANES), 1)
    head_row = lax.broadcasted_iota(jnp.int32, (N_HEADS, CHUNK), 0)

    acts, prods = [], []
    for c in chunks:
        rows = pl.ds(_chunk_start(c), CHUNK)
        kb16 = k_ref[rows, :]
        k = kb16.astype(F32)
        kt16 = k.T.astype(BF16)
        v = v_ref[rows, :].astype(F32)
        if q_ref is not None:
            qb16 = q_ref[rows, :]
            aa = _dot(jnp.concatenate([kb16, qb16], axis=0), kt16)
            acts.append((k, v, qb16.astype(F32)))
            prods.append((aa[:CHUNK], aa[CHUNK:]))
        else:
            acts.append((k, v, None))
            prods.append((_dot(kb16, kt16), None))
    yield

    neg_ms, rhss = [], []
    for i, (c, (k, v, q), (a_kk, a_qk)) in enumerate(zip(chunks, acts, prods)):
        r0 = _chunk_start(c)
        gates = gcol_ref[pl.ds(r0, CHUNK), :]
        for d in range(2):
            beta = jnp.sum(jnp.where(lane == d * N_HEADS + h, gates, 0.0), axis=1, keepdims=True)
            gc = jnp.sum(jnp.where(lane == (2 + d) * N_HEADS + h, gates, 0.0), axis=1, keepdims=True)
            all_heads = grow_ref[(2 + d) * N_HEADS:(3 + d) * N_HEADS, pl.ds(r0, CHUNK)]
            gc_row = jnp.sum(jnp.where(head_row == h, all_heads, 0.0), axis=0, keepdims=True)
            decay_ij = jnp.exp(jnp.minimum(gc - gc_row, 0.0))
            neg_ms.append(a_kk * (-beta) * (decay_ij * tri_ref[TRI_STRICT + d]))
            eg = jnp.exp(gc)
            rhss.append(jnp.concatenate([k * (beta * eg), v * beta], axis=1).astype(BF16))
            g_end = gc[CHUNK - 1:CHUNK, :] if d == 0 else gc[0:1, :]
            kd = k * jnp.exp(g_end - gc)
            decay = jnp.broadcast_to(jnp.exp(g_end), (8, LANES))
            if q is not None:
                early(i, d, kd, decay, q * eg, a_qk * (decay_ij * tri_ref[TRI_INCL + d]))
            else:
                early(i, d, kd, decay, None, None)

    t_invs = yield from _tri_inverse_stages(neg_ms, tri_ref, off_ref)
    wus = [_dot(t, rhs) for t, rhs in zip(t_invs, rhss)]
    yield
    for n, wu in enumerate(wus):
        late(n // 2, n % 2, wu[:, :HEAD_DIM], wu[:, HEAD_DIM:])


def _delta_kernel(pk_ref, pv_ref, pq_ref, gcol_ref, grow_ref, ck_ref, cv_ref, cgcol_ref, cgrow_ref,
                  sz_ref, onorm_ref, tri_ref, off_ref, og_ref,
                  wq_sc, kq_sc, u_sc, ge_sc, s0_sc, o_sc, *, n_chunks, n_ctx_chunks, n_heads_total):
    t = pl.program_id(0)
    h = jnp.minimum(t, n_heads_total - 1) % N_HEADS
    slot_p = t % 2
    slot_s = 1 - slot_p

    @pl.when(t == 0)
    def _():
        wq_sc[1] = jnp.zeros(wq_sc.shape[1:], wq_sc.dtype)
        kq_sc[1] = jnp.zeros(kq_sc.shape[1:], kq_sc.dtype)
        u_sc[1] = jnp.zeros(u_sc.shape[1:], u_sc.dtype)
        ge_sc[1] = jnp.zeros(ge_sc.shape[1:], ge_sc.dtype)
        s0_sc[1] = jnp.zeros(s0_sc.shape[1:], s0_sc.dtype)

    def context_states():
        kept = {}

        def early(c, d, kd, decay, qg, qk):
            kept[c, d] = [kd.T.astype(BF16), decay]

        def late(c, d, w, u):
            kept[c, d] += [w.astype(BF16), u]

        yield from _prep_stages(list(range(n_ctx_chunks)), h, ck_ref, cv_ref, None, cgcol_ref, cgrow_ref,
                                tri_ref, off_ref, early, late)
        states = [jnp.zeros((HEAD_DIM, HEAD_DIM), F32)] * 2
        for c in range(n_ctx_chunks):
            steps = [kept[c, 0], kept[n_ctx_chunks - 1 - c, 1]]
            ws = [_dot(w, s.astype(BF16)) for (_, _, w, _), s in zip(steps, states)]
            yield
            kv = [_dot(kd_t, (u - x).astype(BF16)) for (kd_t, _, _, u), x in zip(steps, ws)]
            yield
            states = [s * decay[0:1, :] + x for s, (_, decay, _, _), x in zip(states, steps, kv)]
        for d in range(2):
            s0_sc[slot_p, d] = states[d]

    def prepare(i):
        chunks = [i * GROUP + g for g in range(GROUP)]

        def early(n, d, kd, decay, qg, qk):
            c = chunks[n]
            kq_sc[slot_p, d, c, 0:CHUNK, :] = kd.T.astype(BF16)
            kq_sc[slot_p, d, c, CHUNK:2 * CHUNK, :] = qk.astype(BF16)
            wq_sc[slot_p, d, c, CHUNK:2 * CHUNK, :] = qg.astype(BF16)
            ge_sc[slot_p, d, c] = decay

        def late(n, d, w, u):
            c = chunks[n]
            wq_sc[slot_p, d, c, 0:CHUNK, :] = w.astype(BF16)
            u_sc[slot_p, d, pl.ds(_chunk_start(c), CHUNK), :] = u

        yield from _prep_stages(chunks, h, pk_ref, pv_ref, pq_ref, gcol_ref, grow_ref, tri_ref, off_ref,
                                early, late)

    def finish(c, o):
        r0 = _chunk_start(c)
        y = o * lax.rsqrt(jnp.mean(o * o, axis=-1, keepdims=True) + NORM_EPS) * onorm_ref[...]
        og_ref[pl.ds(r0, CHUNK), :] = (y * sz_ref[pl.ds(r0, CHUNK), :].astype(F32)).astype(og_ref.dtype)

    def scan(i, states, second_half):
        for g in range(GROUP):
            j = i * GROUP + g
            chunks = (j, n_chunks - 1 - j)
            ws = [_dot(wq_sc[slot_s, d, c], s.astype(BF16)) for d, (c, s) in enumerate(zip(chunks, states))]
            yield
            v_new = [u_sc[slot_s, d, pl.ds(_chunk_start(c), CHUNK), :] - x[:CHUNK]
                     for d, (c, x) in enumerate(zip(chunks, ws))]
            kv = [_dot(kq_sc[slot_s, d, c], vn.astype(BF16)) for d, (c, vn) in enumerate(zip(chunks, v_new))]
            yield
            states = [s * ge_sc[slot_s, d, c][0:1, :] + x[:CHUNK]
                      for d, (c, s, x) in enumerate(zip(chunks, states, kv))]
            for c, x, y in zip(chunks, ws, kv):
                o = x[CHUNK:] + y[CHUNK:]
                rows = pl.ds(_chunk_start(c), CHUNK)
                if second_half:
                    finish(c, o_sc[rows, :] + o)
                else:
                    o_sc[rows, :] = o
        return tuple(states)

    def body(second_half, i, states, *also):
        return _run_interleaved(scan(i, states, second_half), prepare(i), *also)[0]

    n_iter = n_chunks // GROUP
    states = (s0_sc[slot_s, 0], s0_sc[slot_s, 1])
    states = body(False, 0, states, context_states())
    for i in range(1, n_iter):
        states = body(i >= n_iter // 2, i, states)


def _delta_call(p3, gcol, grow, pc3, cgcol, cgrow, onorm_g):
    bsz, length, _ = p3.shape
    ctx_len = pc3.shape[1]
    n_chunks = length // CHUNK
    n_ctx_chunks = ctx_len // CHUNK
    n_heads_total = bsz * N_HEADS
    assert n_chunks % (2 * GROUP) == 0
    tri_masks, off_masks = _mask_constants()

    def prepared(t):
        t = jnp.minimum(t, n_heads_total - 1)
        return t // N_HEADS, t % N_HEADS

    def scanned(t):
        t = jnp.maximum(t - 1, 0)
        return t // N_HEADS, t % N_HEADS

    def head_cols(rows, first_block, which):
        def index(t):
            b, h = which(t)
            return b, 0, first_block + h
        return pl.BlockSpec((None, rows, LANES), index)

    def gates_col(rows):
        return pl.BlockSpec((None, rows, LANES), lambda t: (prepared(t)[0], 0, 0), pipeline_mode=pl.Buffered(1))

    def gates_row(rows):
        return pl.BlockSpec((LANES, rows), lambda t: (0, prepared(t)[0]), pipeline_mode=pl.Buffered(1))

    return pl.pallas_call(
        functools.partial(_delta_kernel, n_chunks=n_chunks, n_ctx_chunks=n_ctx_chunks,
                          n_heads_total=n_heads_total),
        grid=(n_heads_total + 1,),
        in_specs=[head_cols(length, 0, prepared), head_cols(length, N_HEADS, prepared),
                  head_cols(length, 2 * N_HEADS, prepared),
                  gates_col(length), gates_row(length),
                  head_cols(ctx_len, 0, prepared), head_cols(ctx_len, N_HEADS, prepared),
                  gates_col(ctx_len), gates_row(ctx_len),
                  head_cols(length, 3 * N_HEADS, scanned),
                  pl.BlockSpec((1, LANES), lambda t: (0, 0)),
                  pl.BlockSpec(tri_masks.shape, lambda t: (0, 0, 0), pipeline_mode=pl.Buffered(1)),
                  pl.BlockSpec(off_masks.shape, lambda t: (0, 0, 0), pipeline_mode=pl.Buffered(1))],
        out_specs=head_cols(length, 0, scanned),
        out_shape=jax.ShapeDtypeStruct((bsz, length, B_WIDTH), BF16),
        scratch_shapes=[pltpu.VMEM((2, 2, n_chunks, 2 * CHUNK, LANES), BF16),
                        pltpu.VMEM((2, 2, n_chunks, 2 * CHUNK, LANES), BF16),
                        pltpu.VMEM((2, 2, length, LANES), F32),
                        pltpu.VMEM((2, 2, n_chunks, 8, LANES), F32),
                        pltpu.VMEM((2, 2, HEAD_DIM, HEAD_DIM), F32),
                        pltpu.VMEM((length, LANES), F32)],
        compiler_params=pltpu.CompilerParams(
            dimension_semantics=("arbitrary",), vmem_limit_bytes=62 * 1024 * 1024),
        name="delta",
    )(p3, p3, p3, gcol, grow, pc3, pc3, cgcol, cgrow, p3, onorm_g, tri_masks, off_masks)


def _mix_kernel(x_ref, og_ref, mod_ref, g1_ref, wuv_ref, wg_ref, lng_ref, lnb_ref, ws_ref, bs_ref,
                wa_ref, wb_ref, wo_ref, o_ref, ua_ref, *, n_sub):
    x = x_ref[...]
    h = _norm_mod(x, g1_ref[...], mod_ref[:, 0:D_MODEL], mod_ref[:, D_MODEL:2 * D_MODEL]).astype(BF16)
    uv = _dot(h, wuv_ref[...])
    gates_pre = _dot(h, wg_ref[...])
    y_b = _dot(og_ref[...], wb_ref[...])
    v = jax.nn.gelu(uv[:, D_MODEL:])
    vc = v - jnp.mean(v, axis=-1, keepdims=True)
    vn = vc * lax.rsqrt(jnp.mean(vc * vc, axis=-1, keepdims=True) + NORM_EPS) * lng_ref[...] + lnb_ref[...]
    vb = vn.astype(BF16)
    for n in range(n_sub):
        rows = slice(n * A_CHUNK, (n + 1) * A_CHUNK)
        for g in range(A_GROUPS):
            cols = slice(g * LANES, (g + 1) * LANES)
            s = _dot(ws_ref[g], vb[rows, cols]) + bs_ref[:, cols]
            ua_ref[rows, cols] = (jax.nn.gelu(uv[rows, cols]) * s).astype(BF16)
    y_a = _dot(ua_ref[...], wa_ref[...])
    gates = jax.nn.sigmoid(gates_pre)
    t = gates[:, :D_MODEL] * y_a + gates[:, D_MODEL:] * y_b
    y = _dot(t.astype(BF16), wo_ref[...])
    o_ref[...] = x + mod_ref[:, 2 * D_MODEL:3 * D_MODEL] * y


def _mix_call(x2d, og, mod3, g1, w_tail16, ln_g, ln_b, ws16, bs_cols, wa16, wb16, wo16, tm, tiles_per_batch):
    m = x2d.shape[0]
    rows = pl.BlockSpec((tm, D_MODEL), lambda i: (i, 0))

    def tail_block(block):
        return pl.BlockSpec((D_MODEL, 2 * D_MODEL), lambda i: (0, block), pipeline_mode=pl.Buffered(1))

    return pl.pallas_call(
        functools.partial(_mix_kernel, n_sub=tm // A_CHUNK),
        grid=(m // tm,),
        in_specs=[rows, rows,
                  pl.BlockSpec((None, 1, 6 * D_MODEL), lambda i: (i // tiles_per_batch, 0, 0)),
                  _resident((1, D_MODEL)),
                  tail_block(1), tail_block(2),
                  _resident((1, D_MODEL)), _resident((1, D_MODEL)),
                  _resident((A_GROUPS, A_CHUNK, A_CHUNK)), _resident((A_CHUNK, D_MODEL)),
                  _resident((D_MODEL, D_MODEL)), _resident((D_MODEL, D_MODEL)), _resident((D_MODEL, D_MODEL))],
        out_specs=rows,
        out_shape=jax.ShapeDtypeStruct((m, D_MODEL), F32),
        scratch_shapes=[pltpu.VMEM((tm, D_MODEL), BF16)],
        compiler_params=pltpu.CompilerParams(
            dimension_semantics=("parallel",), vmem_limit_bytes=VMEM_LIMIT),
        name="mix",
    )(x2d, og, mod3, g1, w_tail16, w_tail16, ln_g, ln_b, ws16, bs_cols, wa16, wb16, wo16)


def _ffn_kernel(x_ref, mod_ref, g2_ref, wup_ref, cw_ref, cb_ref, wd_ref, gf_ref, o_ref, *, n_parts):
    tm = x_ref.shape[0]
    part = tm // n_parts
    col_in_row = lax.broadcasted_iota(jnp.int32, (part, D_FF), 0) & (GRID_W - 1)
    xs, ups = [], []
    for n in range(n_parts):
        x = x_ref[n * part:(n + 1) * part, :]
        h = _norm_mod(x, g2_ref[...], mod_ref[:, 3 * D_MODEL:4 * D_MODEL],
                      mod_ref[:, 4 * D_MODEL:5 * D_MODEL]).astype(BF16)
        xs.append(x)
        ups.append((_dot(h, wup_ref[:, :D_FF]), _dot(h, wup_ref[:, D_FF:])))
    for n, (x, (a, b)) in enumerate(zip(xs, ups)):
        a_prev = jnp.where(col_in_row == 0, 0.0, pltpu.roll(a, 1, 0))
        a_next = jnp.where(col_in_row == GRID_W - 1, 0.0, pltpu.roll(a, part - 1, 0))
        ac = a_prev * cw_ref[0:1, :] + a * cw_ref[1:2, :] + a_next * cw_ref[2:3, :] + cb_ref[...]
        y = _dot((jax.nn.gelu(ac) * b).astype(BF16), wd_ref[...])
        x2 = x + mod_ref[:, 5 * D_MODEL:6 * D_MODEL] * y
        ms = jnp.mean(x2 * x2, axis=-1, keepdims=True)
        o_ref[n * part:(n + 1) * part, :] = x2 * lax.rsqrt(ms + NORM_EPS) * gf_ref[...]


def _ffn_call(x1, mod3, norm2_g, w_up16, conv_w, conv_b, w_down16, final_g, tm, n_parts, tiles_per_batch):
    m = x1.shape[0]
    assert tm % (n_parts * GRID_W) == 0
    return pl.pallas_call(
        functools.partial(_ffn_kernel, n_parts=n_parts),
        grid=(m // tm,),
        in_specs=[pl.BlockSpec((tm, D_MODEL), lambda i: (i, 0)),
                  pl.BlockSpec((None, 1, 6 * D_MODEL), lambda i: (i // tiles_per_batch, 0, 0)),
                  _resident((1, D_MODEL)),
                  _resident((D_MODEL, 2 * D_FF)),
                  _resident((3, D_FF)), _resident((1, D_FF)),
                  _resident((D_FF, D_MODEL)),
                  _resident((1, D_MODEL))],
        out_specs=pl.BlockSpec((tm, D_MODEL), lambda i: (i, 0)),
        out_shape=jax.ShapeDtypeStruct((m, D_MODEL), F32),
        compiler_params=pltpu.CompilerParams(
            dimension_semantics=("parallel",), vmem_limit_bytes=VMEM_LIMIT),
        name="ffn",
    )(x1, mod3, norm2_g, w_up16, conv_w, conv_b, w_down16, final_g)


def kernel(x, c, ctx, c_ctx, w_mod, b_mod, norm1_g, w_in, conv_qkv, a_log, dt_bias, onorm_g, w_proj_b,
           a_ln_g, a_ln_b, a_ws, a_bs, w_proj_a, w_out, norm2_g, w_up, ffn_conv_w, ffn_conv_b, w_down,
           final_g):
    bsz, length, _ = x.shape
    ctx_len = ctx.shape[1]
    assert w_mod.shape[0] == 1 and bsz <= 7
    assert length % 1024 == 0 and ctx_len % CHUNK == 0

    w = w_in[0]
    w_kv = w[:, :OFF_BA].astype(BF16)
    w_ba = jnp.pad(w[:, OFF_BA:OFF_Q], ((0, 0), (0, LANES - 4 * N_HEADS))).astype(BF16)
    w_tail = w[:, OFF_Q:].astype(BF16)
    lane_pad = (2 * N_HEADS, LANES - 4 * N_HEADS)
    a_lane = jnp.pad(jnp.exp(a_log[0].astype(F32)).reshape(-1), lane_pad).reshape(1, LANES)
    dt_lane = jnp.pad(dt_bias[0].astype(F32).reshape(-1), lane_pad).reshape(1, LANES)
    bs_cols = jnp.repeat(a_bs[0].T, LANES, axis=1)

    cond8 = jnp.concatenate([c, c_ctx[None, :], jnp.zeros((7 - bsz, D_MODEL), F32)], axis=0)
    mod3 = _mod_call(cond8, w_mod[0], b_mod[0]).reshape(8, 1, 6 * D_MODEL)

    x2d = x.reshape(bsz * length, D_MODEL)
    g1 = norm1_g[0].reshape(1, D_MODEL)
    tm_in = 1024
    tiles_per_seq = length // tm_in
    p2d, gcol, grow = _inproj_call(x2d, mod3, lambda i: i // tiles_per_seq, g1, [(w_kv, 0), (w_tail, 0)], w_ba,
                                   conv_qkv[0], a_lane, dt_lane, ('k', 'v', 'q', 'z'), tm_in, tiles_per_seq,
                                   "in_proj")
    pc2d, cgcol, cgrow = _inproj_call(ctx.reshape(bsz * ctx_len, D_MODEL), mod3, lambda i: bsz, g1, [(w_kv, 0)],
                                      w_ba, conv_qkv[0], a_lane, dt_lane, ('k', 'v'), ctx_len, 1, "in_proj_ctx")
    p3 = p2d.reshape(bsz, length, 4 * B_WIDTH)
    pc3 = pc2d.reshape(bsz, ctx_len, 2 * B_WIDTH)
    og = _delta_call(p3, gcol.reshape(bsz, length, LANES), grow, pc3, cgcol.reshape(bsz, ctx_len, LANES), cgrow,
                     onorm_g[0].reshape(1, HEAD_DIM))

    tm = 512
    tiles_per_batch = length // tm
    x1 = _mix_call(x2d, og.reshape(bsz * length, B_WIDTH), mod3, g1, w_tail,
                   a_ln_g[0].reshape(1, -1), a_ln_b[0].reshape(1, -1), a_ws[0].astype(BF16), bs_cols,
                   w_proj_a[0].astype(BF16), w_proj_b[0].astype(BF16), w_out[0].astype(BF16), tm, tiles_per_batch)
    out = _ffn_call(x1, mod3, norm2_g[0].reshape(1, -1), w_up[0].astype(BF16), ffn_conv_w[0],
                    ffn_conv_b[0].reshape(1, -1), w_down[0].astype(BF16), final_g.reshape(1, -1),
                    tm, 2, tiles_per_batch)
    return out.reshape(bsz, length, D_MODEL)
```

```python
import functools

import jax
import jax.numpy as jnp
from jax import lax
from jax.experimental import pallas as pl
from jax.experimental.pallas import tpu as pltpu

F32 = jnp.float32
BF16 = jnp.bfloat16

D_MODEL = 1024
GRID_W = 64
NORM_EPS = 1e-6
N_HEADS = 8
HEAD_DIM = 128
B_WIDTH = N_HEADS * HEAD_DIM
A_GROUPS = 8
A_CHUNK = 128
D_FF = 2816
OFF_BA = 2 * B_WIDTH
OFF_Q = OFF_BA + 4 * N_HEADS

LANES = 128
MXU_COLS = 256
VMEM_BYTES = 64 * 1024 * 1024

CHUNK = 128
GROUP = 8
TM_IN = 1024
IN_PART_ROWS = 256
TM = 512
FFN_ROW_PARTS = 2
MIX_ROW_PARTS = 1
VMEM_LIMIT = 56 * 1024 * 1024
DELTA_VMEM_LIMIT = VMEM_BYTES - 2 * 1024 * 1024


def _silu(x):
    return x * jax.nn.sigmoid(x)


def _dot(a, b):
    return jnp.dot(a, b, preferred_element_type=F32)


def _dot16(a, b):
    return jnp.dot(a, b, preferred_element_type=F32).astype(BF16)


def _mod_kernel(c_ref, w_ref, b_ref, o_ref):
    cond = _silu(c_ref[...])
    o_ref[...] = jnp.dot(cond, w_ref[...], preferred_element_type=F32,
                         precision=lax.Precision.HIGHEST) + b_ref[...]


def _mod_call(cond8, w_mod, b_mod):
    n = w_mod.shape[1]
    tn = 1536
    return pl.pallas_call(
        _mod_kernel,
        grid=(n // tn,),
        in_specs=[pl.BlockSpec((8, D_MODEL), lambda j: (0, 0)),
                  pl.BlockSpec((D_MODEL, tn), lambda j: (0, j)),
                  pl.BlockSpec((1, tn), lambda j: (0, j))],
        out_specs=pl.BlockSpec((8, tn), lambda j: (0, j)),
        out_shape=jax.ShapeDtypeStruct((8, n), F32),
        name="mod",
    )(cond8, w_mod, b_mod.reshape(1, n))


def _norm_mod(x, g, shift, scale):
    ms = jnp.mean(x * x, axis=-1, keepdims=True)
    y = x * lax.rsqrt(ms + NORM_EPS) * g
    return y * (1.0 + scale) + shift


def _l2norm(t):
    return t * lax.rsqrt(jnp.sum(t * t, axis=-1, keepdims=True) + NORM_EPS)


def _gate_columns(ba, a_lane, dt_lane):
    n_sub = ba.shape[0] // CHUNK
    lane = lax.broadcasted_iota(jnp.int32, ba.shape, 1)
    ri = lax.broadcasted_iota(jnp.int32, (CHUNK, CHUNK), 0)
    ci = lax.broadcasted_iota(jnp.int32, (CHUNK, CHUNK), 1)
    tri_lo, tri_up = (ri >= ci).astype(F32), (ri <= ci).astype(F32)
    beta = jax.nn.sigmoid(ba)
    y = ba + dt_lane
    softplus = jnp.maximum(y, 0.0) + jnp.log1p(jnp.exp(-jnp.abs(y)))
    g = jnp.where((lane >= 2 * N_HEADS) & (lane < 4 * N_HEADS), -a_lane * softplus, 0.0)
    part = 4 * N_HEADS
    hi = g.astype(BF16).astype(F32)
    mid = (g - hi).astype(BF16).astype(F32)
    lo = (g - hi - mid).astype(BF16).astype(F32)
    packed = (hi + pltpu.roll(mid, part, 1) + pltpu.roll(lo, 2 * part, 1)).astype(BF16)

    def unpack(r):
        return r + pltpu.roll(r, LANES - part, 1) + pltpu.roll(r, LANES - 2 * part, 1)

    tri_lo, tri_up = tri_lo.astype(BF16), tri_up.astype(BF16)
    parts = [packed[s * CHUNK:(s + 1) * CHUNK, :] for s in range(n_sub)]
    pre = [unpack(_dot(tri_lo, x)) for x in parts]
    suf = [unpack(_dot(tri_up, x)) for x in parts]
    lane = lax.broadcasted_iota(jnp.int32, (CHUNK, LANES), 1)
    return [jnp.where(lane < 2 * N_HEADS, beta[s * CHUNK:(s + 1) * CHUNK, :],
                      jnp.where(lane < 3 * N_HEADS, pre[s], suf[s])) for s in range(n_sub)]


def _resident(shape):
    return pl.BlockSpec(shape, lambda i: (0,) * len(shape), pipeline_mode=pl.Buffered(1))


def _slab_order(kinds, slabs_per_group):
    heavy = [(g, s) for g, kind in enumerate(kinds) if kind != 'z' for s in range(slabs_per_group)]
    light = [(g, s) for g, kind in enumerate(kinds) if kind == 'z' for s in range(slabs_per_group)]
    every = max(len(heavy) // max(len(light), 1), 1)
    order = []
    for n, item in enumerate(heavy):
        order.append(item)
        if light and (n + 1) % every == 0:
            order.append(light.pop(0))
    return order + light


def _inproj_kernel(x_ref, xb_ref, xa_ref, mod_ref, g_ref, wba_ref, cw_ref, a_ref, dt_ref, *rest,
                   tiles_per_seq, kinds):
    n_w = (len(kinds) + 1) // 2
    w_refs = rest[:n_w]
    o_ref, gcol_ref, grow_ref, h_ref, halo_ref = rest[n_w:]
    i = pl.program_id(0)
    tm = o_ref.shape[0]
    shift, scale = mod_ref[:, 0:D_MODEL], mod_ref[:, D_MODEL:2 * D_MODEL]
    h_ref[...] = _norm_mod(x_ref[...], g_ref[...], shift, scale).astype(BF16)
    halo = jnp.concatenate([xb_ref[...], xa_ref[...]], axis=0)
    halo_ref[...] = _norm_mod(halo, g_ref[...], shift, scale).astype(BF16)
    ba = _dot(h_ref[...], wba_ref[...])
    for s, col in enumerate(_gate_columns(ba, a_ref[...], dt_ref[...])):
        rows = slice(s * CHUNK, (s + 1) * CHUNK)
        gcol_ref[rows, :] = col
        grow_ref[:, rows] = col.T

    tile_in_seq = i % tiles_per_seq
    slab = MXU_COLS
    row8 = lax.broadcasted_iota(jnp.int32, (8, slab), 0)
    n_parts = max(tm // IN_PART_ROWS, 1)
    part = tm // n_parts
    tiled = (part // 8, 8, slab)
    for g, s in _slab_order(kinds, B_WIDTH // slab):
        kind = kinds[g]
        col0 = g * B_WIDTH + s * slab
        cols = slice(col0, col0 + slab)
        w_cols = slice(col0 % (2 * B_WIDTH), col0 % (2 * B_WIDTH) + slab)
        w_ref = w_refs[g // 2]
        ps = [_dot(h_ref[n * part:(n + 1) * part, :], w_ref[:, w_cols]) for n in range(n_parts)]
        if kind != 'z':
            edge = _dot(halo_ref[...], w_ref[:, w_cols])
            taps = [jnp.broadcast_to(cw_ref[r:r + 1, cols], (8, slab)) for r in range(3)]
        for n, p in enumerate(ps):
            if kind != 'z':
                prev_row = ps[n - 1][part - 1:part] if n > 0 else jnp.where(tile_in_seq > 0, edge[7:8, :], 0.0)
                next_row = (ps[n + 1][0:1] if n + 1 < n_parts
                            else jnp.where(tile_in_seq < tiles_per_seq - 1, edge[8:9, :], 0.0))
                xp = pltpu.roll(p, 1, 0)
                xp = jnp.concatenate([jnp.where(row8 == 0, prev_row, xp[0:8]), xp[8:]], axis=0)
                xn = pltpu.roll(p, part - 1, 0)
                xn = jnp.concatenate([xn[:part - 8], jnp.where(row8 == 7, next_row, xn[part - 8:])], axis=0)
                p = (xp.reshape(tiled) * taps[0] + p.reshape(tiled) * taps[1]
                     + xn.reshape(tiled) * taps[2]).reshape(part, slab)
            y = _silu(p)
            rows = slice(n * part, (n + 1) * part)
            for hd in range(slab // HEAD_DIM):
                head = y[:, hd * HEAD_DIM:(hd + 1) * HEAD_DIM]
                if kind in ('k', 'q'):
                    head = _l2norm(head)
                if kind == 'q':
                    head = head * (HEAD_DIM ** -0.5)
                o_ref[rows, col0 + hd * HEAD_DIM:col0 + (hd + 1) * HEAD_DIM] = head.astype(o_ref.dtype)


def _inproj_call(x2d, mod3, mod_row_of_tile, norm_g, w_pairs, w_ba, conv_w, a_lane, dt_lane, kinds, tm,
                 tiles_per_seq, name):
    m = x2d.shape[0]
    n = B_WIDTH * len(kinds)
    assert len(kinds) == 2 * len(w_pairs) and tm % CHUNK == 0
    assert all(kind != 'z' for kind in kinds[:conv_w.shape[1] // B_WIDTH]) and 'z' not in kinds[:-1]
    last_halo = m // 8 - 1

    def pair_spec(block):
        return pl.BlockSpec((D_MODEL, 2 * B_WIDTH), lambda i: (0, block), pipeline_mode=pl.Buffered(1))

    return pl.pallas_call(
        functools.partial(_inproj_kernel, tiles_per_seq=tiles_per_seq, kinds=kinds),
        grid=(m // tm,),
        in_specs=[pl.BlockSpec((tm, D_MODEL), lambda i: (i, 0)),
                  pl.BlockSpec((8, D_MODEL), lambda i: (jnp.maximum(i * (tm // 8) - 1, 0), 0)),
                  pl.BlockSpec((8, D_MODEL), lambda i: (jnp.minimum((i + 1) * (tm // 8), last_halo), 0)),
                  pl.BlockSpec((None, 1, 6 * D_MODEL), lambda i: (mod_row_of_tile(i), 0, 0)),
                  _resident((1, D_MODEL)),
                  _resident((D_MODEL, LANES)),
                  _resident(conv_w.shape),
                  _resident((1, LANES)), _resident((1, LANES))] + [pair_spec(blk) for _, blk in w_pairs],
        out_specs=[pl.BlockSpec((tm, n), lambda i: (i, 0)),
                   pl.BlockSpec((tm, LANES), lambda i: (i, 0)),
                   pl.BlockSpec((LANES, tm), lambda i: (0, i))],
        out_shape=[jax.ShapeDtypeStruct((m, n), BF16), jax.ShapeDtypeStruct((m, LANES), F32),
                   jax.ShapeDtypeStruct((LANES, m), F32)],
        scratch_shapes=[pltpu.VMEM((tm, D_MODEL), BF16), pltpu.VMEM((16, D_MODEL), BF16)],
        compiler_params=pltpu.CompilerParams(
            dimension_semantics=("parallel",), vmem_limit_bytes=VMEM_LIMIT),
        name=name,
    )(x2d, x2d, x2d, mod3, norm_g, w_ba, conv_w, a_lane, dt_lane, *[arr for arr, _ in w_pairs])


def _run_interleaved(*gens):
    results = [None] * len(gens)
    live = list(range(len(gens)))
    while live:
        for idx in list(live):
            try:
                next(gens[idx])
            except StopIteration as stop:
                results[idx] = stop.value
                live.remove(idx)
    return results


TRI_INCL, TRI_STRICT, TRI_BLK16, TRI_EYE = 0, 2, 4, 5
MERGE_SIZES = (32, 64, 128)


def _mask_constants():
    ri = lax.broadcasted_iota(jnp.int32, (CHUNK, CHUNK), 0)
    ci = lax.broadcasted_iota(jnp.int32, (CHUNK, CHUNK), 1)

    def same_block(size):
        return (ri // size) == (ci // size)

    tri = jnp.stack([ri >= ci, ri <= ci, ri > ci, ri < ci, same_block(16), ri == ci]).astype(F32)
    off = jnp.stack([same_block(s) & ~same_block(s // 2) for s in MERGE_SIZES]).astype(BF16)
    return tri, off


def _tri_inverse_stages(n_mats, tri_ref, off_ref):
    n16 = [n * tri_ref[TRI_BLK16] for n in n_mats]
    ts = [tri_ref[TRI_EYE] + n for n in n16]
    pbs = [n.astype(BF16) for n in n16]
    pbs = [_dot16(p, p) for p in pbs]
    nbs = [n.astype(BF16) for n in n_mats]
    yield
    for _ in range(2):
        ts = [t + _dot(t.astype(BF16), p) for t, p in zip(ts, pbs)]
        pbs = [_dot16(p, p) for p in pbs]
        yield
    ts = [t + _dot(t.astype(BF16), p) for t, p in zip(ts, pbs)]
    yield
    tbs = [t.astype(BF16) for t in ts]
    for level in range(len(MERGE_SIZES)):
        xs = [_dot16(nb * off_ref[level], tb) for nb, tb in zip(nbs, tbs)]
        yield
        tbs = [tb + _dot16(tb, x) for tb, x in zip(tbs, xs)]
        yield
    return tbs


def _chunk_start(c):
    return c * CHUNK if isinstance(c, int) else pl.multiple_of(c * CHUNK, CHUNK)


def _prep_stages(chunks, h, k_ref, v_ref, q_ref, gcol_ref, grow_ref, tri_ref, off_ref, early, late):
    lane = lax.broadcasted_iota(jnp.int32, (CHUNK, LANES), 1)
    head_row = lax.broadcasted_iota(jnp.int32, (N_HEADS, CHUNK), 0)

    acts, prods = [], []
    for c in chunks:
        rows = pl.ds(_chunk_start(c), CHUNK)
        kb16 = k_ref[rows, :]
        k = kb16.astype(F32)
        kt16 = k.T.astype(BF16)
        v = v_ref[rows, :].astype(F32)
        if q_ref is not None:
            qb16 = q_ref[rows, :]
            aa = _dot(jnp.concatenate([kb16, qb16], axis=0), kt16)
            acts.append((k, v, qb16.astype(F32)))
            prods.append((aa[:CHUNK], aa[CHUNK:]))
        else:
            acts.append((k, v, None))
            prods.append((_dot(kb16, kt16), None))
    yield

    neg_ms, rhss = [], []
    for i, (c, (k, v, q), (a_kk, a_qk)) in enumerate(zip(chunks, acts, prods)):
        r0 = _chunk_start(c)
        gates = gcol_ref[pl.ds(r0, CHUNK), :]
        for d in range(2):
            beta = jnp.sum(jnp.where(lane == d * N_HEADS + h, gates, 0.0), axis=1, keepdims=True)
            gc = jnp.sum(jnp.where(lane == (2 + d) * N_HEADS + h, gates, 0.0), axis=1, keepdims=True)
            all_heads = grow_ref[(2 + d) * N_HEADS:(3 + d) * N_HEADS, pl.ds(r0, CHUNK)]
            gc_row = jnp.sum(jnp.where(head_row == h, all_heads, 0.0), axis=0, keepdims=True)
            decay_ij = jnp.exp(jnp.minimum(gc - gc_row, 0.0))
            neg_ms.append(a_kk * (-beta) * (decay_ij * tri_ref[TRI_STRICT + d]))
            eg = jnp.exp(gc)
            rhss.append(jnp.concatenate([k * (beta * eg), v * beta], axis=1).astype(BF16))
            g_end = gc[CHUNK - 1:CHUNK, :] if d == 0 else gc[0:1, :]
            kd = k * jnp.exp(g_end - gc)
            decay = jnp.broadcast_to(jnp.exp(g_end), (8, LANES))
            if q is not None:
                early(i, d, kd, decay, q * eg, a_qk * (decay_ij * tri_ref[TRI_INCL + d]))
            else:
                early(i, d, kd, decay, None, None)

    t_invs = yield from _tri_inverse_stages(neg_ms, tri_ref, off_ref)
    wus = [_dot(t, rhs) for t, rhs in zip(t_invs, rhss)]
    yield
    for n, wu in enumerate(wus):
        late(n // 2, n % 2, wu[:, :HEAD_DIM], wu[:, HEAD_DIM:])


def _delta_kernel(pk_ref, pv_ref, pq_ref, gcol_ref, grow_ref, ck_ref, cv_ref, cgcol_ref, cgrow_ref,
                  sz_ref, onorm_ref, tri_ref, off_ref, og_ref,
                  wq_sc, kq_sc, u_sc, ge_sc, s0_sc, o_sc, *, n_chunks, n_ctx_chunks, n_heads_total):
    t = pl.program_id(0)
    h = jnp.minimum(t, n_heads_total - 1) % N_HEADS
    slot_p = t % 2
    slot_s = 1 - slot_p

    @pl.when(t == 0)
    def _():
        wq_sc[1] = jnp.zeros(wq_sc.shape[1:], wq_sc.dtype)
        kq_sc[1] = jnp.zeros(kq_sc.shape[1:], kq_sc.dtype)
        u_sc[1] = jnp.zeros(u_sc.shape[1:], u_sc.dtype)
        ge_sc[1] = jnp.zeros(ge_sc.shape[1:], ge_sc.dtype)
        s0_sc[1] = jnp.zeros(s0_sc.shape[1:], s0_sc.dtype)

    def context_states():
        kept = {}

        def early(c, d, kd, decay, qg, qk):
            kept[c, d] = [kd.T.astype(BF16), decay]

        def late(c, d, w, u):
            kept[c, d] += [w.astype(BF16), u]

        yield from _prep_stages(list(range(n_ctx_chunks)), h, ck_ref, cv_ref, None, cgcol_ref, cgrow_ref,
                                tri_ref, off_ref, early, late)
        states = [jnp.zeros((HEAD_DIM, HEAD_DIM), F32)] * 2
        for c in range(n_ctx_chunks):
            steps = [kept[c, 0], kept[n_ctx_chunks - 1 - c, 1]]
            ws = [_dot(w, s.astype(BF16)) for (_, _, w, _), s in zip(steps, states)]
            yield
            kv = [_dot(kd_t, (u - x).astype(BF16)) for (kd_t, _, _, u), x in zip(steps, ws)]
            yield
            states = [s * decay[0:1, :] + x for s, (_, decay, _, _), x in zip(states, steps, kv)]
        for d in range(2):
            s0_sc[slot_p, d] = states[d]

    def prepare(i):
        chunks = [i * GROUP + g for g in range(GROUP)]

        def early(n, d, kd, decay, qg, qk):
            c = chunks[n]
            kq_sc[slot_p, d, c, 0:CHUNK, :] = kd.T.astype(BF16)
            kq_sc[slot_p, d, c, CHUNK:2 * CHUNK, :] = qk.astype(BF16)
            wq_sc[slot_p, d, c, CHUNK:2 * CHUNK, :] = qg.astype(BF16)
            ge_sc[slot_p, d, c] = decay

        def late(n, d, w, u):
            c = chunks[n]
            wq_sc[slot_p, d, c, 0:CHUNK, :] = w.astype(BF16)
            u_sc[slot_p, d, pl.ds(_chunk_start(c), CHUNK), :] = u

        yield from _prep_stages(chunks, h, pk_ref, pv_ref, pq_ref, gcol_ref, grow_ref, tri_ref, off_ref,
                                early, late)

    def finish(c, o):
        r0 = _chunk_start(c)
        y = o * lax.rsqrt(jnp.mean(o * o, axis=-1, keepdims=True) + NORM_EPS) * onorm_ref[...]
        og_ref[pl.ds(r0, CHUNK), :] = (y * sz_ref[pl.ds(r0, CHUNK), :].astype(F32)).astype(og_ref.dtype)

    def scan(i, states, second_half):
        for g in range(GROUP):
            j = i * GROUP + g
            chunks = (j, n_chunks - 1 - j)
            ws = [_dot(wq_sc[slot_s, d, c], s.astype(BF16)) for d, (c, s) in enumerate(zip(chunks, states))]
            yield
            v_new = [u_sc[slot_s, d, pl.ds(_chunk_start(c), CHUNK), :] - x[:CHUNK]
                     for d, (c, x) in enumerate(zip(chunks, ws))]
            kv = [_dot(kq_sc[slot_s, d, c], vn.astype(BF16)) for d, (c, vn) in enumerate(zip(chunks, v_new))]
            yield
            states = [s * ge_sc[slot_s, d, c][0:1, :] + x[:CHUNK]
                      for d, (c, s, x) in enumerate(zip(chunks, states, kv))]
            for c, x, y in zip(chunks, ws, kv):
                o = x[CHUNK:] + y[CHUNK:]
                rows = pl.ds(_chunk_start(c), CHUNK)
                if second_half:
                    finish(c, o_sc[rows, :] + o)
                else:
                    o_sc[rows, :] = o
        return tuple(states)

    def body(second_half, i, states, *also):
        return _run_interleaved(scan(i, states, second_half), prepare(i), *also)[0]

    n_iter = n_chunks // GROUP
    states = (s0_sc[slot_s, 0], s0_sc[slot_s, 1])
    states = body(False, 0, states, context_states())
    for i in range(1, n_iter):
        states = body(i >= n_iter // 2, i, states)


def _delta_call(p3, gcol, grow, pc3, cgcol, cgrow, onorm_g):
    bsz, length, _ = p3.shape
    ctx_len = pc3.shape[1]
    n_chunks = length // CHUNK
    n_ctx_chunks = ctx_len // CHUNK
    n_heads_total = bsz * N_HEADS
    assert n_chunks % (2 * GROUP) == 0
    tri_masks, off_masks = _mask_constants()

    def prepared(t):
        t = jnp.minimum(t, n_heads_total - 1)
        return t // N_HEADS, t % N_HEADS

    def scanned(t):
        t = jnp.maximum(t - 1, 0)
        return t // N_HEADS, t % N_HEADS

    def head_cols(rows, first_block, which):
        def index(t):
            b, h = which(t)
            return b, 0, first_block + h
        return pl.BlockSpec((None, rows, LANES), index)

    def gates_col(rows):
        return pl.BlockSpec((None, rows, LANES), lambda t: (prepared(t)[0], 0, 0), pipeline_mode=pl.Buffered(1))

    def gates_row(rows):
        return pl.BlockSpec((LANES, rows), lambda t: (0, prepared(t)[0]), pipeline_mode=pl.Buffered(1))

    return pl.pallas_call(
        functools.partial(_delta_kernel, n_chunks=n_chunks, n_ctx_chunks=n_ctx_chunks,
                          n_heads_total=n_heads_total),
        grid=(n_heads_total + 1,),
        in_specs=[head_cols(length, 0, prepared), head_cols(length, N_HEADS, prepared),
                  head_cols(length, 2 * N_HEADS, prepared),
                  gates_col(length), gates_row(length),
                  head_cols(ctx_len, 0, prepared), head_cols(ctx_len, N_HEADS, prepared),
                  gates_col(ctx_len), gates_row(ctx_len),
                  head_cols(length, 3 * N_HEADS, scanned),
                  pl.BlockSpec((1, LANES), lambda t: (0, 0)),
                  pl.BlockSpec(tri_masks.shape, lambda t: (0, 0, 0), pipeline_mode=pl.Buffered(1)),
                  pl.BlockSpec(off_masks.shape, lambda t: (0, 0, 0), pipeline_mode=pl.Buffered(1))],
        out_specs=head_cols(length, 0, scanned),
        out_shape=jax.ShapeDtypeStruct((bsz, length, B_WIDTH), BF16),
        scratch_shapes=[pltpu.VMEM((2, 2, n_chunks, 2 * CHUNK, LANES), BF16),
                        pltpu.VMEM((2, 2, n_chunks, 2 * CHUNK, LANES), BF16),
                        pltpu.VMEM((2, 2, length, LANES), F32),
                        pltpu.VMEM((2, 2, n_chunks, 8, LANES), F32),
                        pltpu.VMEM((2, 2, HEAD_DIM, HEAD_DIM), F32),
                        pltpu.VMEM((length, LANES), F32)],
        compiler_params=pltpu.CompilerParams(
            dimension_semantics=("arbitrary",), vmem_limit_bytes=DELTA_VMEM_LIMIT),
        name="delta",
    )(p3, p3, p3, gcol, grow, pc3, pc3, cgcol, cgrow, p3, onorm_g, tri_masks, off_masks)


def _mix_kernel(x_ref, og_ref, mod_ref, g1_ref, wuv_ref, wg_ref, lng_ref, lnb_ref, ws_ref, bs_ref,
                wa_ref, wb_ref, wo_ref, o_ref, ua_ref, *, n_parts):
    tm = x_ref.shape[0]
    part = tm // n_parts
    early = []
    for n in range(n_parts):
        rows = slice(n * part, (n + 1) * part)
        x = x_ref[rows, :]
        h = _norm_mod(x, g1_ref[...], mod_ref[:, 0:D_MODEL], mod_ref[:, D_MODEL:2 * D_MODEL]).astype(BF16)
        early.append((x, _dot(h, wuv_ref[...]), _dot(h, wg_ref[...]), _dot(og_ref[rows, :], wb_ref[...])))
    for n, (x, uv, gates_pre, y_b) in enumerate(early):
        v = jax.nn.gelu(uv[:, D_MODEL:])
        vc = v - jnp.mean(v, axis=-1, keepdims=True)
        vn = vc * lax.rsqrt(jnp.mean(vc * vc, axis=-1, keepdims=True) + NORM_EPS) * lng_ref[...] + lnb_ref[...]
        vb = vn.astype(BF16)
        for c in range(part // A_CHUNK):
            rows = slice(c * A_CHUNK, (c + 1) * A_CHUNK)
            out_rows = slice(n * part + c * A_CHUNK, n * part + (c + 1) * A_CHUNK)
            for g in range(A_GROUPS):
                cols = slice(g * LANES, (g + 1) * LANES)
                s = _dot(ws_ref[g], vb[rows, cols]) + bs_ref[:, cols]
                ua_ref[out_rows, cols] = (jax.nn.gelu(uv[rows, cols]) * s).astype(BF16)
        rows = slice(n * part, (n + 1) * part)
        y_a = _dot(ua_ref[rows, :], wa_ref[...])
        gates = jax.nn.sigmoid(gates_pre)
        t = gates[:, :D_MODEL] * y_a + gates[:, D_MODEL:] * y_b
        y = _dot(t.astype(BF16), wo_ref[...])
        o_ref[rows, :] = x + mod_ref[:, 2 * D_MODEL:3 * D_MODEL] * y


def _mix_call(x2d, og, mod3, g1, w_tail16, ln_g, ln_b, ws16, bs_cols, wa16, wb16, wo16, tm, n_parts,
              tiles_per_batch):
    m = x2d.shape[0]
    assert tm % (n_parts * A_CHUNK) == 0
    rows = pl.BlockSpec((tm, D_MODEL), lambda i: (i, 0))

    def tail_block(block):
        return pl.BlockSpec((D_MODEL, 2 * D_MODEL), lambda i: (0, block), pipeline_mode=pl.Buffered(1))

    return pl.pallas_call(
        functools.partial(_mix_kernel, n_parts=n_parts),
        grid=(m // tm,),
        in_specs=[rows, rows,
                  pl.BlockSpec((None, 1, 6 * D_MODEL), lambda i: (i // tiles_per_batch, 0, 0)),
                  _resident((1, D_MODEL)),
                  tail_block(1), tail_block(2),
                  _resident((1, D_MODEL)), _resident((1, D_MODEL)),
                  _resident((A_GROUPS, A_CHUNK, A_CHUNK)), _resident((A_CHUNK, D_MODEL)),
                  _resident((D_MODEL, D_MODEL)), _resident((D_MODEL, D_MODEL)), _resident((D_MODEL, D_MODEL))],
        out_specs=rows,
        out_shape=jax.ShapeDtypeStruct((m, D_MODEL), F32),
        scratch_shapes=[pltpu.VMEM((tm, D_MODEL), BF16)],
        compiler_params=pltpu.CompilerParams(
            dimension_semantics=("parallel",), vmem_limit_bytes=VMEM_LIMIT),
        name="mix",
    )(x2d, og, mod3, g1, w_tail16, w_tail16, ln_g, ln_b, ws16, bs_cols, wa16, wb16, wo16)


def _ffn_kernel(x_ref, mod_ref, g2_ref, wup_ref, cw_ref, cb_ref, wd_ref, gf_ref, o_ref, *, n_parts):
    tm = x_ref.shape[0]
    part = tm // n_parts
    col_in_row = lax.broadcasted_iota(jnp.int32, (part, D_FF), 0) & (GRID_W - 1)
    xs, ups = [], []
    for n in range(n_parts):
        x = x_ref[n * part:(n + 1) * part, :]
        h = _norm_mod(x, g2_ref[...], mod_ref[:, 3 * D_MODEL:4 * D_MODEL],
                      mod_ref[:, 4 * D_MODEL:5 * D_MODEL]).astype(BF16)
        xs.append(x)
        ups.append((_dot(h, wup_ref[:, :D_FF]), _dot(h, wup_ref[:, D_FF:])))
    for n, (x, (a, b)) in enumerate(zip(xs, ups)):
        a_prev = jnp.where(col_in_row == 0, 0.0, pltpu.roll(a, 1, 0))
        a_next = jnp.where(col_in_row == GRID_W - 1, 0.0, pltpu.roll(a, part - 1, 0))
        ac = a_prev * cw_ref[0:1, :] + a * cw_ref[1:2, :] + a_next * cw_ref[2:3, :] + cb_ref[...]
        y = _dot((jax.nn.gelu(ac) * b).astype(BF16), wd_ref[...])
        x2 = x + mod_ref[:, 5 * D_MODEL:6 * D_MODEL] * y
        ms = jnp.mean(x2 * x2, axis=-1, keepdims=True)
        o_ref[n * part:(n + 1) * part, :] = x2 * lax.rsqrt(ms + NORM_EPS) * gf_ref[...]


def _ffn_call(x1, mod3, norm2_g, w_up16, conv_w, conv_b, w_down16, final_g, tm, n_parts, tiles_per_batch):
    m = x1.shape[0]
    assert tm % (n_parts * GRID_W) == 0
    return pl.pallas_call(
        functools.partial(_ffn_kernel, n_parts=n_parts),
        grid=(m // tm,),
        in_specs=[pl.BlockSpec((tm, D_MODEL), lambda i: (i, 0)),
                  pl.BlockSpec((None, 1, 6 * D_MODEL), lambda i: (i // tiles_per_batch, 0, 0)),
                  _resident((1, D_MODEL)),
                  _resident((D_MODEL, 2 * D_FF)),
                  _resident((3, D_FF)), _resident((1, D_FF)),
                  _resident((D_FF, D_MODEL)),
                  _resident((1, D_MODEL))],
        out_specs=pl.BlockSpec((tm, D_MODEL), lambda i: (i, 0)),
        out_shape=jax.ShapeDtypeStruct((m, D_MODEL), F32),
        compiler_params=pltpu.CompilerParams(
            dimension_semantics=("parallel",), vmem_limit_bytes=VMEM_LIMIT),
        name="ffn",
    )(x1, mod3, norm2_g, w_up16, conv_w, conv_b, w_down16, final_g)


def kernel(x, c, ctx, c_ctx, w_mod, b_mod, norm1_g, w_in, conv_qkv, a_log, dt_bias, onorm_g, w_proj_b,
           a_ln_g, a_ln_b, a_ws, a_bs, w_proj_a, w_out, norm2_g, w_up, ffn_conv_w, ffn_conv_b, w_down,
           final_g):
    bsz, length, _ = x.shape
    ctx_len = ctx.shape[1]
    assert w_mod.shape[0] == 1 and bsz <= 7
    assert length % TM_IN == 0 and length % TM == 0 and ctx_len % CHUNK == 0

    w = w_in[0]
    w_kv = w[:, :OFF_BA].astype(BF16)
    w_ba = jnp.pad(w[:, OFF_BA:OFF_Q], ((0, 0), (0, LANES - 4 * N_HEADS))).astype(BF16)
    w_tail = w[:, OFF_Q:].astype(BF16)
    lane_pad = (2 * N_HEADS, LANES - 4 * N_HEADS)
    a_lane = jnp.pad(jnp.exp(a_log[0].astype(F32)).reshape(-1), lane_pad).reshape(1, LANES)
    dt_lane = jnp.pad(dt_bias[0].astype(F32).reshape(-1), lane_pad).reshape(1, LANES)
    bs_cols = jnp.repeat(a_bs[0].T, LANES, axis=1)

    cond8 = jnp.concatenate([c, c_ctx[None, :], jnp.zeros((7 - bsz, D_MODEL), F32)], axis=0)
    mod3 = _mod_call(cond8, w_mod[0], b_mod[0]).reshape(8, 1, 6 * D_MODEL)

    x2d = x.reshape(bsz * length, D_MODEL)
    g1 = norm1_g[0].reshape(1, D_MODEL)
    tiles_per_seq = length // TM_IN
    p2d, gcol, grow = _inproj_call(x2d, mod3, lambda i: i // tiles_per_seq, g1, [(w_kv, 0), (w_tail, 0)], w_ba,
                                   conv_qkv[0], a_lane, dt_lane, ('k', 'v', 'q', 'z'), TM_IN, tiles_per_seq,
                                   "in_proj")
    pc2d, cgcol, cgrow = _inproj_call(ctx.reshape(bsz * ctx_len, D_MODEL), mod3, lambda i: bsz, g1, [(w_kv, 0)],
                                      w_ba, conv_qkv[0], a_lane, dt_lane, ('k', 'v'), ctx_len, 1, "in_proj_ctx")
    p3 = p2d.reshape(bsz, length, 4 * B_WIDTH)
    pc3 = pc2d.reshape(bsz, ctx_len, 2 * B_WIDTH)
    og = _delta_call(p3, gcol.reshape(bsz, length, LANES), grow, pc3, cgcol.reshape(bsz, ctx_len, LANES), cgrow,
                     onorm_g[0].reshape(1, HEAD_DIM))

    tiles_per_batch = length // TM
    x1 = _mix_call(x2d, og.reshape(bsz * length, B_WIDTH), mod3, g1, w_tail,
                   a_ln_g[0].reshape(1, -1), a_ln_b[0].reshape(1, -1), a_ws[0].astype(BF16), bs_cols,
                   w_proj_a[0].astype(BF16), w_proj_b[0].astype(BF16), w_out[0].astype(BF16), TM, MIX_ROW_PARTS,
                   tiles_per_batch)
    out = _ffn_call(x1, mod3, norm2_g[0].reshape(1, -1), w_up[0].astype(BF16), ffn_conv_w[0],
                    ffn_conv_b[0].reshape(1, -1), w_down[0].astype(BF16), final_g.reshape(1, -1),
                    TM, FFN_ROW_PARTS, tiles_per_batch)
    return out.reshape(bsz, length, D_MODEL)
```

```python
import functools

import jax
import jax.numpy as jnp
from jax import lax
from jax.experimental import pallas as pl
from jax.experimental.pallas import tpu as pltpu

F32 = jnp.float32
BF16 = jnp.bfloat16

D_MODEL = 1024
GRID_W = 64
NORM_EPS = 1e-6
N_HEADS = 8
HEAD_DIM = 128
B_WIDTH = N_HEADS * HEAD_DIM
A_GROUPS = 8
A_CHUNK = 128
D_FF = 2816
OFF_BA = 2 * B_WIDTH
OFF_Q = OFF_BA + 4 * N_HEADS

LANES = 128
MXU_COLS = 256
VMEM_BYTES = 64 * 1024 * 1024

CHUNK = 128
GROUP = 8
TM_IN = 1024
IN_PART_ROWS = 256
TM = 512
FFN_ROW_PARTS = 2
MIX_ROW_PARTS = 1
VMEM_LIMIT = 56 * 1024 * 1024
DELTA_VMEM_LIMIT = VMEM_BYTES - 2 * 1024 * 1024


def _silu(x):
    return x * jax.nn.sigmoid(x)


def _dot(a, b):
    return jnp.dot(a, b, preferred_element_type=F32)


def _dot16(a, b):
    return jnp.dot(a, b, preferred_element_type=F32).astype(BF16)


def _bf16_terms(x, n_terms):
    terms = []
    for _ in range(n_terms):
        terms.append(x.astype(BF16))
        x = x - terms[-1].astype(F32)
    return terms


def _mod_kernel(c_ref, w_ref, b_ref, o_ref):
    rows = c_ref.shape[0]
    terms = [t.astype(F32) for t in _bf16_terms(_silu(c_ref[...]), 3)]
    cond = jnp.concatenate(terms + [jnp.zeros((rows, D_MODEL), F32)], axis=0).astype(BF16)
    acc = sum(_dot(cond, w_term) for w_term in _bf16_terms(w_ref[...], 2))
    o_ref[...] = acc[0:rows] + acc[rows:2 * rows] + acc[2 * rows:3 * rows] + b_ref[...]


def _mod_call(cond8, w_mod, b_mod):
    n = w_mod.shape[1]
    tn = 1536
    return pl.pallas_call(
        _mod_kernel,
        grid=(n // tn,),
        in_specs=[pl.BlockSpec((8, D_MODEL), lambda j: (0, 0)),
                  pl.BlockSpec((D_MODEL, tn), lambda j: (0, j)),
                  pl.BlockSpec((1, tn), lambda j: (0, j))],
        out_specs=pl.BlockSpec((8, tn), lambda j: (0, j)),
        out_shape=jax.ShapeDtypeStruct((8, n), F32),
        name="mod",
    )(cond8, w_mod, b_mod.reshape(1, n))


def _norm_mod(x, g, shift, scale):
    ms = jnp.mean(x * x, axis=-1, keepdims=True)
    y = x * lax.rsqrt(ms + NORM_EPS) * g
    return y * (1.0 + scale) + shift


def _l2norm(t):
    return t * lax.rsqrt(jnp.sum(t * t, axis=-1, keepdims=True) + NORM_EPS)


def _gate_columns(ba, a_lane, dt_lane):
    n_sub = ba.shape[0] // CHUNK
    lane = lax.broadcasted_iota(jnp.int32, ba.shape, 1)
    ri = lax.broadcasted_iota(jnp.int32, (CHUNK, CHUNK), 0)
    ci = lax.broadcasted_iota(jnp.int32, (CHUNK, CHUNK), 1)
    tri_lo, tri_up = (ri >= ci).astype(F32), (ri <= ci).astype(F32)
    beta = jax.nn.sigmoid(ba)
    y = ba + dt_lane
    softplus = jnp.maximum(y, 0.0) + jnp.log1p(jnp.exp(-jnp.abs(y)))
    g = jnp.where((lane >= 2 * N_HEADS) & (lane < 4 * N_HEADS), -a_lane * softplus, 0.0)
    part = 4 * N_HEADS
    hi = g.astype(BF16).astype(F32)
    mid = (g - hi).astype(BF16).astype(F32)
    lo = (g - hi - mid).astype(BF16).astype(F32)
    packed = (hi + pltpu.roll(mid, part, 1) + pltpu.roll(lo, 2 * part, 1)).astype(BF16)

    def unpack(r):
        return r + pltpu.roll(r, LANES - part, 1) + pltpu.roll(r, LANES - 2 * part, 1)

    tri_lo, tri_up = tri_lo.astype(BF16), tri_up.astype(BF16)
    parts = [packed[s * CHUNK:(s + 1) * CHUNK, :] for s in range(n_sub)]
    pre = [unpack(_dot(tri_lo, x)) for x in parts]
    suf = [unpack(_dot(tri_up, x)) for x in parts]
    lane = lax.broadcasted_iota(jnp.int32, (CHUNK, LANES), 1)
    return [jnp.where(lane < 2 * N_HEADS, beta[s * CHUNK:(s + 1) * CHUNK, :],
                      jnp.where(lane < 3 * N_HEADS, pre[s], suf[s])) for s in range(n_sub)]


def _resident(shape):
    return pl.BlockSpec(shape, lambda i: (0,) * len(shape), pipeline_mode=pl.Buffered(1))


def _slab_order(kinds, slabs_per_group):
    heavy = [(g, s) for g, kind in enumerate(kinds) if kind != 'z' for s in range(slabs_per_group)]
    light = [(g, s) for g, kind in enumerate(kinds) if kind == 'z' for s in range(slabs_per_group)]
    every = max(len(heavy) // max(len(light), 1), 1)
    order = []
    for n, item in enumerate(heavy):
        order.append(item)
        if light and (n + 1) % every == 0:
            order.append(light.pop(0))
    return order + light


def _inproj_kernel(x_ref, xb_ref, xa_ref, mod_ref, g_ref, wba_ref, cw_ref, a_ref, dt_ref, *rest,
                   tiles_per_seq, kinds):
    n_w = (len(kinds) + 1) // 2
    w_refs = rest[:n_w]
    o_ref, gcol_ref, grow_ref, h_ref, halo_ref = rest[n_w:]
    i = pl.program_id(0)
    tm = o_ref.shape[0]
    shift, scale = mod_ref[:, 0:D_MODEL], mod_ref[:, D_MODEL:2 * D_MODEL]
    h_ref[...] = _norm_mod(x_ref[...], g_ref[...], shift, scale).astype(BF16)
    halo = jnp.concatenate([xb_ref[...], xa_ref[...]], axis=0)
    halo_ref[...] = _norm_mod(halo, g_ref[...], shift, scale).astype(BF16)
    ba = _dot(h_ref[...], wba_ref[...])
    for s, col in enumerate(_gate_columns(ba, a_ref[...], dt_ref[...])):
        rows = slice(s * CHUNK, (s + 1) * CHUNK)
        gcol_ref[rows, :] = col
        grow_ref[:, rows] = col.T

    tile_in_seq = i % tiles_per_seq
    slab = MXU_COLS
    row8 = lax.broadcasted_iota(jnp.int32, (8, slab), 0)
    n_parts = max(tm // IN_PART_ROWS, 1)
    part = tm // n_parts
    tiled = (part // 8, 8, slab)
    for g, s in _slab_order(kinds, B_WIDTH // slab):
        kind = kinds[g]
        col0 = g * B_WIDTH + s * slab
        cols = slice(col0, col0 + slab)
        w_cols = slice(col0 % (2 * B_WIDTH), col0 % (2 * B_WIDTH) + slab)
        w_ref = w_refs[g // 2]
        ps = [_dot(h_ref[n * part:(n + 1) * part, :], w_ref[:, w_cols]) for n in range(n_parts)]
        if kind != 'z':
            edge = _dot(halo_ref[...], w_ref[:, w_cols])
            taps = [jnp.broadcast_to(cw_ref[r:r + 1, cols], (8, slab)) for r in range(3)]
        for n, p in enumerate(ps):
            if kind != 'z':
                prev_row = ps[n - 1][part - 1:part] if n > 0 else jnp.where(tile_in_seq > 0, edge[7:8, :], 0.0)
                next_row = (ps[n + 1][0:1] if n + 1 < n_parts
                            else jnp.where(tile_in_seq < tiles_per_seq - 1, edge[8:9, :], 0.0))
                xp = pltpu.roll(p, 1, 0)
                xp = jnp.concatenate([jnp.where(row8 == 0, prev_row, xp[0:8]), xp[8:]], axis=0)
                xn = pltpu.roll(p, part - 1, 0)
                xn = jnp.concatenate([xn[:part - 8], jnp.where(row8 == 7, next_row, xn[part - 8:])], axis=0)
                p = (xp.reshape(tiled) * taps[0] + p.reshape(tiled) * taps[1]
                     + xn.reshape(tiled) * taps[2]).reshape(part, slab)
            y = _silu(p)
            rows = slice(n * part, (n + 1) * part)
            for hd in range(slab // HEAD_DIM):
                head = y[:, hd * HEAD_DIM:(hd + 1) * HEAD_DIM]
                if kind in ('k', 'q'):
                    head = _l2norm(head)
                if kind == 'q':
                    head = head * (HEAD_DIM ** -0.5)
                o_ref[rows, col0 + hd * HEAD_DIM:col0 + (hd + 1) * HEAD_DIM] = head.astype(o_ref.dtype)


def _inproj_call(x2d, mod3, mod_row_of_tile, norm_g, w_pairs, w_ba, conv_w, a_lane, dt_lane, kinds, tm,
                 tiles_per_seq, name):
    m = x2d.shape[0]
    n = B_WIDTH * len(kinds)
    assert len(kinds) == 2 * len(w_pairs) and tm % CHUNK == 0
    assert all(kind != 'z' for kind in kinds[:conv_w.shape[1] // B_WIDTH]) and 'z' not in kinds[:-1]
    last_halo = m // 8 - 1

    def pair_spec(block):
        return pl.BlockSpec((D_MODEL, 2 * B_WIDTH), lambda i: (0, block), pipeline_mode=pl.Buffered(1))

    return pl.pallas_call(
        functools.partial(_inproj_kernel, tiles_per_seq=tiles_per_seq, kinds=kinds),
        grid=(m // tm,),
        in_specs=[pl.BlockSpec((tm, D_MODEL), lambda i: (i, 0)),
                  pl.BlockSpec((8, D_MODEL), lambda i: (jnp.maximum(i * (tm // 8) - 1, 0), 0)),
                  pl.BlockSpec((8, D_MODEL), lambda i: (jnp.minimum((i + 1) * (tm // 8), last_halo), 0)),
                  pl.BlockSpec((None, 1, 6 * D_MODEL), lambda i: (mod_row_of_tile(i), 0, 0)),
                  _resident((1, D_MODEL)),
                  _resident((D_MODEL, LANES)),
                  _resident(conv_w.shape),
                  _resident((1, LANES)), _resident((1, LANES))] + [pair_spec(blk) for _, blk in w_pairs],
        out_specs=[pl.BlockSpec((tm, n), lambda i: (i, 0)),
                   pl.BlockSpec((tm, LANES), lambda i: (i, 0)),
                   pl.BlockSpec((LANES, tm), lambda i: (0, i))],
        out_shape=[jax.ShapeDtypeStruct((m, n), BF16), jax.ShapeDtypeStruct((m, LANES), F32),
                   jax.ShapeDtypeStruct((LANES, m), F32)],
        scratch_shapes=[pltpu.VMEM((tm, D_MODEL), BF16), pltpu.VMEM((16, D_MODEL), BF16)],
        compiler_params=pltpu.CompilerParams(
            dimension_semantics=("parallel",), vmem_limit_bytes=VMEM_LIMIT),
        name=name,
    )(x2d, x2d, x2d, mod3, norm_g, w_ba, conv_w, a_lane, dt_lane, *[arr for arr, _ in w_pairs])


def _run_interleaved(*gens):
    results = [None] * len(gens)
    live = list(range(len(gens)))
    while live:
        for idx in list(live):
            try:
                next(gens[idx])
            except StopIteration as stop:
                results[idx] = stop.value
                live.remove(idx)
    return results


TRI_INCL, TRI_STRICT, TRI_BLK16, TRI_EYE = 0, 2, 4, 5
MERGE_SIZES = (32, 64, 128)


def _mask_constants():
    ri = lax.broadcasted_iota(jnp.int32, (CHUNK, CHUNK), 0)
    ci = lax.broadcasted_iota(jnp.int32, (CHUNK, CHUNK), 1)

    def same_block(size):
        return (ri // size) == (ci // size)

    tri = jnp.stack([ri >= ci, ri <= ci, ri > ci, ri < ci, same_block(16), ri == ci]).astype(F32)
    off = jnp.stack([same_block(s) & ~same_block(s // 2) for s in MERGE_SIZES]).astype(BF16)
    return tri, off


def _tri_inverse_stages(n_mats, tri_ref, off_ref):
    n16 = [n * tri_ref[TRI_BLK16] for n in n_mats]
    ts = [tri_ref[TRI_EYE] + n for n in n16]
    pbs = [n.astype(BF16) for n in n16]
    pbs = [_dot16(p, p) for p in pbs]
    nbs = [n.astype(BF16) for n in n_mats]
    yield
    for _ in range(2):
        ts = [t + _dot(t.astype(BF16), p) for t, p in zip(ts, pbs)]
        pbs = [_dot16(p, p) for p in pbs]
        yield
    ts = [t + _dot(t.astype(BF16), p) for t, p in zip(ts, pbs)]
    yield
    tbs = [t.astype(BF16) for t in ts]
    for level in range(len(MERGE_SIZES)):
        xs = [_dot16(nb * off_ref[level], tb) for nb, tb in zip(nbs, tbs)]
        yield
        tbs = [tb + _dot16(tb, x) for tb, x in zip(tbs, xs)]
        yield
    return tbs


def _chunk_start(c):
    return c * CHUNK if isinstance(c, int) else pl.multiple_of(c * CHUNK, CHUNK)


def _prep_stages(chunks, h, k_ref, v_ref, q_ref, gcol_ref, grow_ref, tri_ref, off_ref, early, late):
    lane = lax.broadcasted_iota(jnp.int32, (CHUNK, LANES), 1)
    head_row = lax.broadcasted_iota(jnp.int32, (N_HEADS, CHUNK), 0)

    acts, prods = [], []
    for c in chunks:
        rows = pl.ds(_chunk_start(c), CHUNK)
        kb16 = k_ref[rows, :]
        k = kb16.astype(F32)
        kt16 = k.T.astype(BF16)
        v = v_ref[rows, :].astype(F32)
        if q_ref is not None:
            qb16 = q_ref[rows, :]
            aa = _dot(jnp.concatenate([kb16, qb16], axis=0), kt16)
            acts.append((k, v, qb16.astype(F32)))
            prods.append((aa[:CHUNK], aa[CHUNK:]))
        else:
            acts.append((k, v, None))
            prods.append((_dot(kb16, kt16), None))
    yield

    neg_ms, rhss = [], []
    for i, (c, (k, v, q), (a_kk, a_qk)) in enumerate(zip(chunks, acts, prods)):
        r0 = _chunk_start(c)
        gates = gcol_ref[pl.ds(r0, CHUNK), :]
        for d in range(2):
            beta = jnp.sum(jnp.where(lane == d * N_HEADS + h, gates, 0.0), axis=1, keepdims=True)
            gc = jnp.sum(jnp.where(lane == (2 + d) * N_HEADS + h, gates, 0.0), axis=1, keepdims=True)
            all_heads = grow_ref[(2 + d) * N_HEADS:(3 + d) * N_HEADS, pl.ds(r0, CHUNK)]
            gc_row = jnp.sum(jnp.where(head_row == h, all_heads, 0.0), axis=0, keepdims=True)
            decay_ij = jnp.exp(jnp.minimum(gc - gc_row, 0.0))
            neg_ms.append(a_kk * (-beta) * (decay_ij * tri_ref[TRI_STRICT + d]))
            eg = jnp.exp(gc)
            rhss.append(jnp.concatenate([k * (beta * eg), v * beta], axis=1).astype(BF16))
            g_end = gc[CHUNK - 1:CHUNK, :] if d == 0 else gc[0:1, :]
            kd = k * jnp.exp(g_end - gc)
            decay = jnp.broadcast_to(jnp.exp(g_end), (8, LANES))
            if q is not None:
                early(i, d, kd, decay, q * eg, a_qk * (decay_ij * tri_ref[TRI_INCL + d]))
            else:
                early(i, d, kd, decay, None, None)

    t_invs = yield from _tri_inverse_stages(neg_ms, tri_ref, off_ref)
    wus = [_dot(t, rhs) for t, rhs in zip(t_invs, rhss)]
    yield
    for n, wu in enumerate(wus):
        late(n // 2, n % 2, wu[:, :HEAD_DIM], wu[:, HEAD_DIM:])


def _delta_kernel(pk_ref, pv_ref, pq_ref, gcol_ref, grow_ref, ck_ref, cv_ref, cgcol_ref, cgrow_ref,
                  sz_ref, onorm_ref, tri_ref, off_ref, og_ref,
                  wq_sc, kq_sc, u_sc, ge_sc, s0_sc, o_sc, *, n_chunks, n_ctx_chunks, n_heads_total):
    t = pl.program_id(0)
    h = jnp.minimum(t, n_heads_total - 1) % N_HEADS
    slot_p = t % 2
    slot_s = 1 - slot_p

    @pl.when(t == 0)
    def _():
        wq_sc[1] = jnp.zeros(wq_sc.shape[1:], wq_sc.dtype)
        kq_sc[1] = jnp.zeros(kq_sc.shape[1:], kq_sc.dtype)
        u_sc[1] = jnp.zeros(u_sc.shape[1:], u_sc.dtype)
        ge_sc[1] = jnp.zeros(ge_sc.shape[1:], ge_sc.dtype)
        s0_sc[1] = jnp.zeros(s0_sc.shape[1:], s0_sc.dtype)

    def context_states():
        kept = {}

        def early(c, d, kd, decay, qg, qk):
            kept[c, d] = [kd.T.astype(BF16), decay]

        def late(c, d, w, u):
            kept[c, d] += [w.astype(BF16), u]

        yield from _prep_stages(list(range(n_ctx_chunks)), h, ck_ref, cv_ref, None, cgcol_ref, cgrow_ref,
                                tri_ref, off_ref, early, late)
        states = [jnp.zeros((HEAD_DIM, HEAD_DIM), F32)] * 2
        for c in range(n_ctx_chunks):
            steps = [kept[c, 0], kept[n_ctx_chunks - 1 - c, 1]]
            ws = [_dot(w, s.astype(BF16)) for (_, _, w, _), s in zip(steps, states)]
            yield
            kv = [_dot(kd_t, (u - x).astype(BF16)) for (kd_t, _, _, u), x in zip(steps, ws)]
            yield
            states = [s * decay[0:1, :] + x for s, (_, decay, _, _), x in zip(states, steps, kv)]
        for d in range(2):
            s0_sc[slot_p, d] = states[d]

    def prepare(i):
        chunks = [i * GROUP + g for g in range(GROUP)]

        def early(n, d, kd, decay, qg, qk):
            c = chunks[n]
            kq_sc[slot_p, d, c, 0:CHUNK, :] = kd.T.astype(BF16)
            kq_sc[slot_p, d, c, CHUNK:2 * CHUNK, :] = qk.astype(BF16)
            wq_sc[slot_p, d, c, CHUNK:2 * CHUNK, :] = qg.astype(BF16)
            ge_sc[slot_p, d, c] = decay

        def late(n, d, w, u):
            c = chunks[n]
            wq_sc[slot_p, d, c, 0:CHUNK, :] = w.astype(BF16)
            u_sc[slot_p, d, pl.ds(_chunk_start(c), CHUNK), :] = u

        yield from _prep_stages(chunks, h, pk_ref, pv_ref, pq_ref, gcol_ref, grow_ref, tri_ref, off_ref,
                                early, late)

    def finish(c, o):
        r0 = _chunk_start(c)
        y = o * lax.rsqrt(jnp.mean(o * o, axis=-1, keepdims=True) + NORM_EPS) * onorm_ref[...]
        og_ref[pl.ds(r0, CHUNK), :] = (y * sz_ref[pl.ds(r0, CHUNK), :].astype(F32)).astype(og_ref.dtype)

    def scan(i, states, second_half):
        for g in range(GROUP):
            j = i * GROUP + g
            chunks = (j, n_chunks - 1 - j)
            ws = [_dot(wq_sc[slot_s, d, c], s.astype(BF16)) for d, (c, s) in enumerate(zip(chunks, states))]
            yield
            v_new = [u_sc[slot_s, d, pl.ds(_chunk_start(c), CHUNK), :] - x[:CHUNK]
                     for d, (c, x) in enumerate(zip(chunks, ws))]
            kv = [_dot(kq_sc[slot_s, d, c], vn.astype(BF16)) for d, (c, vn) in enumerate(zip(chunks, v_new))]
            yield
            states = [s * ge_sc[slot_s, d, c][0:1, :] + x[:CHUNK]
                      for d, (c, s, x) in enumerate(zip(chunks, states, kv))]
            for c, x, y in zip(chunks, ws, kv):
                o = x[CHUNK:] + y[CHUNK:]
                rows = pl.ds(_chunk_start(c), CHUNK)
                if second_half:
                    finish(c, o_sc[rows, :] + o)
                else:
                    o_sc[rows, :] = o
        return tuple(states)

    def body(second_half, i, states, *also):
        return _run_interleaved(scan(i, states, second_half), prepare(i), *also)[0]

    n_iter = n_chunks // GROUP
    states = (s0_sc[slot_s, 0], s0_sc[slot_s, 1])
    states = body(False, 0, states, context_states())
    for i in range(1, n_iter):
        states = body(i >= n_iter // 2, i, states)


def _delta_call(p3, gcol, grow, pc3, cgcol, cgrow, onorm_g):
    bsz, length, _ = p3.shape
    ctx_len = pc3.shape[1]
    n_chunks = length // CHUNK
    n_ctx_chunks = ctx_len // CHUNK
    n_heads_total = bsz * N_HEADS
    assert n_chunks % (2 * GROUP) == 0
    tri_masks, off_masks = _mask_constants()

    def prepared(t):
        t = jnp.minimum(t, n_heads_total - 1)
        return t // N_HEADS, t % N_HEADS

    def scanned(t):
        t = jnp.maximum(t - 1, 0)
        return t // N_HEADS, t % N_HEADS

    def head_cols(rows, first_block, which):
        def index(t):
            b, h = which(t)
            return b, 0, first_block + h
        return pl.BlockSpec((None, rows, LANES), index)

    def gates_col(rows):
        return pl.BlockSpec((None, rows, LANES), lambda t: (prepared(t)[0], 0, 0), pipeline_mode=pl.Buffered(1))

    def gates_row(rows):
        return pl.BlockSpec((LANES, rows), lambda t: (0, prepared(t)[0]), pipeline_mode=pl.Buffered(1))

    return pl.pallas_call(
        functools.partial(_delta_kernel, n_chunks=n_chunks, n_ctx_chunks=n_ctx_chunks,
                          n_heads_total=n_heads_total),
        grid=(n_heads_total + 1,),
        in_specs=[head_cols(length, 0, prepared), head_cols(length, N_HEADS, prepared),
                  head_cols(length, 2 * N_HEADS, prepared),
                  gates_col(length), gates_row(length),
                  head_cols(ctx_len, 0, prepared), head_cols(ctx_len, N_HEADS, prepared),
                  gates_col(ctx_len), gates_row(ctx_len),
                  head_cols(length, 3 * N_HEADS, scanned),
                  pl.BlockSpec((1, LANES), lambda t: (0, 0)),
                  pl.BlockSpec(tri_masks.shape, lambda t: (0, 0, 0), pipeline_mode=pl.Buffered(1)),
                  pl.BlockSpec(off_masks.shape, lambda t: (0, 0, 0), pipeline_mode=pl.Buffered(1))],
        out_specs=head_cols(length, 0, scanned),
        out_shape=jax.ShapeDtypeStruct((bsz, length, B_WIDTH), BF16),
        scratch_shapes=[pltpu.VMEM((2, 2, n_chunks, 2 * CHUNK, LANES), BF16),
                        pltpu.VMEM((2, 2, n_chunks, 2 * CHUNK, LANES), BF16),
                        pltpu.VMEM((2, 2, length, LANES), F32),
                        pltpu.VMEM((2, 2, n_chunks, 8, LANES), F32),
                        pltpu.VMEM((2, 2, HEAD_DIM, HEAD_DIM), F32),
                        pltpu.VMEM((length, LANES), F32)],
        compiler_params=pltpu.CompilerParams(
            dimension_semantics=("arbitrary",), vmem_limit_bytes=DELTA_VMEM_LIMIT),
        name="delta",
    )(p3, p3, p3, gcol, grow, pc3, pc3, cgcol, cgrow, p3, onorm_g, tri_masks, off_masks)


def _mix_kernel(x_ref, og_ref, mod_ref, g1_ref, wuv_ref, wg_ref, lng_ref, lnb_ref, ws_ref, bs_ref,
                wa_ref, wb_ref, wo_ref, o_ref, ua_ref, *, n_parts):
    tm = x_ref.shape[0]
    part = tm // n_parts
    early = []
    for n in range(n_parts):
        rows = slice(n * part, (n + 1) * part)
        x = x_ref[rows, :]
        h = _norm_mod(x, g1_ref[...], mod_ref[:, 0:D_MODEL], mod_ref[:, D_MODEL:2 * D_MODEL]).astype(BF16)
        early.append((x, _dot(h, wuv_ref[...]), _dot(h, wg_ref[...]), _dot(og_ref[rows, :], wb_ref[...])))
    for n, (x, uv, gates_pre, y_b) in enumerate(early):
        v = jax.nn.gelu(uv[:, D_MODEL:])
        vc = v - jnp.mean(v, axis=-1, keepdims=True)
        vn = vc * lax.rsqrt(jnp.mean(vc * vc, axis=-1, keepdims=True) + NORM_EPS) * lng_ref[...] + lnb_ref[...]
        vb = vn.astype(BF16)
        for c in range(part // A_CHUNK):
            rows = slice(c * A_CHUNK, (c + 1) * A_CHUNK)
            out_rows = slice(n * part + c * A_CHUNK, n * part + (c + 1) * A_CHUNK)
            for g in range(A_GROUPS):
                cols = slice(g * LANES, (g + 1) * LANES)
                s = _dot(ws_ref[g], vb[rows, cols]) + bs_ref[:, cols]
                ua_ref[out_rows, cols] = (jax.nn.gelu(uv[rows, cols]) * s).astype(BF16)
        rows = slice(n * part, (n + 1) * part)
        y_a = _dot(ua_ref[rows, :], wa_ref[...])
        gates = jax.nn.sigmoid(gates_pre)
        t = gates[:, :D_MODEL] * y_a + gates[:, D_MODEL:] * y_b
        y = _dot(t.astype(BF16), wo_ref[...])
        o_ref[rows, :] = x + mod_ref[:, 2 * D_MODEL:3 * D_MODEL] * y


def _mix_call(x2d, og, mod3, g1, w_tail16, ln_g, ln_b, ws16, bs_cols, wa16, wb16, wo16, tm, n_parts,
              tiles_per_batch):
    m = x2d.shape[0]
    assert tm % (n_parts * A_CHUNK) == 0
    rows = pl.BlockSpec((tm, D_MODEL), lambda i: (i, 0))

    def tail_block(block):
        return pl.BlockSpec((D_MODEL, 2 * D_MODEL), lambda i: (0, block), pipeline_mode=pl.Buffered(1))

    return pl.pallas_call(
        functools.partial(_mix_kernel, n_parts=n_parts),
        grid=(m // tm,),
        in_specs=[rows, rows,
                  pl.BlockSpec((None, 1, 6 * D_MODEL), lambda i: (i // tiles_per_batch, 0, 0)),
                  _resident((1, D_MODEL)),
                  tail_block(1), tail_block(2),
                  _resident((1, D_MODEL)), _resident((1, D_MODEL)),
                  _resident((A_GROUPS, A_CHUNK, A_CHUNK)), _resident((A_CHUNK, D_MODEL)),
                  _resident((D_MODEL, D_MODEL)), _resident((D_MODEL, D_MODEL)), _resident((D_MODEL, D_MODEL))],
        out_specs=rows,
        out_shape=jax.ShapeDtypeStruct((m, D_MODEL), F32),
        scratch_shapes=[pltpu.VMEM((tm, D_MODEL), BF16)],
        compiler_params=pltpu.CompilerParams(
            dimension_semantics=("parallel",), vmem_limit_bytes=VMEM_LIMIT),
        name="mix",
    )(x2d, og, mod3, g1, w_tail16, w_tail16, ln_g, ln_b, ws16, bs_cols, wa16, wb16, wo16)


def _ffn_kernel(x_ref, mod_ref, g2_ref, wup_ref, cw_ref, cb_ref, wd_ref, gf_ref, o_ref, *, n_parts):
    tm = x_ref.shape[0]
    part = tm // n_parts
    col_in_row = lax.broadcasted_iota(jnp.int32, (part, D_FF), 0) & (GRID_W - 1)
    xs, ups = [], []
    for n in range(n_parts):
        x = x_ref[n * part:(n + 1) * part, :]
        h = _norm_mod(x, g2_ref[...], mod_ref[:, 3 * D_MODEL:4 * D_MODEL],
                      mod_ref[:, 4 * D_MODEL:5 * D_MODEL]).astype(BF16)
        xs.append(x)
        ups.append((_dot(h, wup_ref[:, :D_FF]), _dot(h, wup_ref[:, D_FF:])))
    for n, (x, (a, b)) in enumerate(zip(xs, ups)):
        a_prev = jnp.where(col_in_row == 0, 0.0, pltpu.roll(a, 1, 0))
        a_next = jnp.where(col_in_row == GRID_W - 1, 0.0, pltpu.roll(a, part - 1, 0))
        ac = a_prev * cw_ref[0:1, :] + a * cw_ref[1:2, :] + a_next * cw_ref[2:3, :] + cb_ref[...]
        y = _dot((jax.nn.gelu(ac) * b).astype(BF16), wd_ref[...])
        x2 = x + mod_ref[:, 5 * D_MODEL:6 * D_MODEL] * y
        ms = jnp.mean(x2 * x2, axis=-1, keepdims=True)
        o_ref[n * part:(n + 1) * part, :] = x2 * lax.rsqrt(ms + NORM_EPS) * gf_ref[...]


def _ffn_call(x1, mod3, norm2_g, w_up16, conv_w, conv_b, w_down16, final_g, tm, n_parts, tiles_per_batch):
    m = x1.shape[0]
    assert tm % (n_parts * GRID_W) == 0
    return pl.pallas_call(
        functools.partial(_ffn_kernel, n_parts=n_parts),
        grid=(m // tm,),
        in_specs=[pl.BlockSpec((tm, D_MODEL), lambda i: (i, 0)),
                  pl.BlockSpec((None, 1, 6 * D_MODEL), lambda i: (i // tiles_per_batch, 0, 0)),
                  _resident((1, D_MODEL)),
                  _resident((D_MODEL, 2 * D_FF)),
                  _resident((3, D_FF)), _resident((1, D_FF)),
                  _resident((D_FF, D_MODEL)),
                  _resident((1, D_MODEL))],
        out_specs=pl.BlockSpec((tm, D_MODEL), lambda i: (i, 0)),
        out_shape=jax.ShapeDtypeStruct((m, D_MODEL), F32),
        compiler_params=pltpu.CompilerParams(
            dimension_semantics=("parallel",), vmem_limit_bytes=VMEM_LIMIT),
        name="ffn",
    )(x1, mod3, norm2_g, w_up16, conv_w, conv_b, w_down16, final_g)


def kernel(x, c, ctx, c_ctx, w_mod, b_mod, norm1_g, w_in, conv_qkv, a_log, dt_bias, onorm_g, w_proj_b,
           a_ln_g, a_ln_b, a_ws, a_bs, w_proj_a, w_out, norm2_g, w_up, ffn_conv_w, ffn_conv_b, w_down,
           final_g):
    bsz, length, _ = x.shape
    ctx_len = ctx.shape[1]
    assert w_mod.shape[0] == 1 and bsz <= 7
    assert length % TM_IN == 0 and length % TM == 0 and ctx_len % CHUNK == 0

    w = w_in[0]
    w_kv = w[:, :OFF_BA].astype(BF16)
    w_ba = jnp.pad(w[:, OFF_BA:OFF_Q], ((0, 0), (0, LANES - 4 * N_HEADS))).astype(BF16)
    w_tail = w[:, OFF_Q:].astype(BF16)
    lane_pad = (2 * N_HEADS, LANES - 4 * N_HEADS)
    a_lane = jnp.pad(jnp.exp(a_log[0].astype(F32)).reshape(-1), lane_pad).reshape(1, LANES)
    dt_lane = jnp.pad(dt_bias[0].astype(F32).reshape(-1), lane_pad).reshape(1, LANES)
    bs_cols = jnp.repeat(a_bs[0].T, LANES, axis=1)

    cond8 = jnp.concatenate([c, c_ctx[None, :], jnp.zeros((7 - bsz, D_MODEL), F32)], axis=0)
    mod3 = _mod_call(cond8, w_mod[0], b_mod[0]).reshape(8, 1, 6 * D_MODEL)

    x2d = x.reshape(bsz * length, D_MODEL)
    g1 = norm1_g[0].reshape(1, D_MODEL)
    tiles_per_seq = length // TM_IN
    p2d, gcol, grow = _inproj_call(x2d, mod3, lambda i: i // tiles_per_seq, g1, [(w_kv, 0), (w_tail, 0)], w_ba,
                                   conv_qkv[0], a_lane, dt_lane, ('k', 'v', 'q', 'z'), TM_IN, tiles_per_seq,
                                   "in_proj")
    pc2d, cgcol, cgrow = _inproj_call(ctx.reshape(bsz * ctx_len, D_MODEL), mod3, lambda i: bsz, g1, [(w_kv, 0)],
                                      w_ba, conv_qkv[0], a_lane, dt_lane, ('k', 'v'), ctx_len, 1, "in_proj_ctx")
    p3 = p2d.reshape(bsz, length, 4 * B_WIDTH)
    pc3 = pc2d.reshape(bsz, ctx_len, 2 * B_WIDTH)
    og = _delta_call(p3, gcol.reshape(bsz, length, LANES), grow, pc3, cgcol.reshape(bsz, ctx_len, LANES), cgrow,
                     onorm_g[0].reshape(1, HEAD_DIM))

    tiles_per_batch = length // TM
    x1 = _mix_call(x2d, og.reshape(bsz * length, B_WIDTH), mod3, g1, w_tail,
                   a_ln_g[0].reshape(1, -1), a_ln_b[0].reshape(1, -1), a_ws[0].astype(BF16), bs_cols,
                   w_proj_a[0].astype(BF16), w_proj_b[0].astype(BF16), w_out[0].astype(BF16), TM, MIX_ROW_PARTS,
                   tiles_per_batch)
    out = _ffn_call(x1, mod3, norm2_g[0].reshape(1, -1), w_up[0].astype(BF16), ffn_conv_w[0],
                    ffn_conv_b[0].reshape(1, -1), w_down[0].astype(BF16), final_g.reshape(1, -1),
                    TM, FFN_ROW_PARTS, tiles_per_batch)
    return out.reshape(bsz, length, D_MODEL)
```

```python
import functools

import jax
import jax.numpy as jnp
from jax import lax
from jax.experimental import pallas as pl
from jax.experimental.pallas import tpu as pltpu

F32 = jnp.float32
BF16 = jnp.bfloat16

D_MODEL = 1024
GRID_W = 64
NORM_EPS = 1e-6
N_HEADS = 8
HEAD_DIM = 128
B_WIDTH = N_HEADS * HEAD_DIM
A_GROUPS = 8
A_CHUNK = 128
D_FF = 2816
OFF_BA = 2 * B_WIDTH
OFF_Q = OFF_BA + 4 * N_HEADS

LANES = 128
MXU_COLS = 256
VMEM_BYTES = 64 * 1024 * 1024

CHUNK = 128
GROUP = 8
TM_IN = 1024
IN_PART_ROWS = 256
TM = 512
FFN_ROW_PARTS = 2
MIX_ROW_PARTS = 1
VMEM_LIMIT = 56 * 1024 * 1024
DELTA_VMEM_LIMIT = VMEM_BYTES - 2 * 1024 * 1024


def _silu(x):
    return x * jax.nn.sigmoid(x)


def _dot(a, b):
    return jnp.dot(a, b, preferred_element_type=F32)


def _dot16(a, b):
    return jnp.dot(a, b, preferred_element_type=F32).astype(BF16)


def _bf16_terms(x, n_terms):
    terms = []
    for _ in range(n_terms):
        terms.append(x.astype(BF16))
        x = x - terms[-1].astype(F32)
    return terms


def _mod_kernel(c_ref, w_ref, b_ref, o_ref):
    rows = c_ref.shape[0]
    terms = [t.astype(F32) for t in _bf16_terms(_silu(c_ref[...]), 3)]
    cond = jnp.concatenate(terms + [jnp.zeros((rows, D_MODEL), F32)], axis=0).astype(BF16)
    acc = sum(_dot(cond, w_term) for w_term in _bf16_terms(w_ref[...], 2))
    o_ref[...] = acc[0:rows] + acc[rows:2 * rows] + acc[2 * rows:3 * rows] + b_ref[...]


def _mod_call(cond8, w_mod, b_mod):
    n = w_mod.shape[1]
    tn = 1536
    return pl.pallas_call(
        _mod_kernel,
        grid=(n // tn,),
        in_specs=[pl.BlockSpec((8, D_MODEL), lambda j: (0, 0)),
                  pl.BlockSpec((D_MODEL, tn), lambda j: (0, j)),
                  pl.BlockSpec((1, tn), lambda j: (0, j))],
        out_specs=pl.BlockSpec((8, tn), lambda j: (0, j)),
        out_shape=jax.ShapeDtypeStruct((8, n), F32),
        name="mod",
    )(cond8, w_mod, b_mod.reshape(1, n))


def _norm_mod(x, g, shift, scale):
    ms = jnp.mean(x * x, axis=-1, keepdims=True)
    y = x * lax.rsqrt(ms + NORM_EPS) * g
    return y * (1.0 + scale) + shift


def _l2norm(t):
    return t * lax.rsqrt(jnp.sum(t * t, axis=-1, keepdims=True) + NORM_EPS)


def _gate_columns(ba, a_lane, dt_lane):
    n_sub = ba.shape[0] // CHUNK
    lane = lax.broadcasted_iota(jnp.int32, ba.shape, 1)
    ri = lax.broadcasted_iota(jnp.int32, (CHUNK, CHUNK), 0)
    ci = lax.broadcasted_iota(jnp.int32, (CHUNK, CHUNK), 1)
    tri_lo, tri_up = (ri >= ci).astype(F32), (ri <= ci).astype(F32)
    beta = jax.nn.sigmoid(ba)
    y = ba + dt_lane
    softplus = jnp.maximum(y, 0.0) + jnp.log1p(jnp.exp(-jnp.abs(y)))
    g = jnp.where((lane >= 2 * N_HEADS) & (lane < 4 * N_HEADS), -a_lane * softplus, 0.0)
    part = 4 * N_HEADS
    hi = g.astype(BF16).astype(F32)
    mid = (g - hi).astype(BF16).astype(F32)
    lo = (g - hi - mid).astype(BF16).astype(F32)
    packed = (hi + pltpu.roll(mid, part, 1) + pltpu.roll(lo, 2 * part, 1)).astype(BF16)

    def unpack(r):
        return r + pltpu.roll(r, LANES - part, 1) + pltpu.roll(r, LANES - 2 * part, 1)

    tri_lo, tri_up = tri_lo.astype(BF16), tri_up.astype(BF16)
    parts = [packed[s * CHUNK:(s + 1) * CHUNK, :] for s in range(n_sub)]
    pre = [unpack(_dot(tri_lo, x)) for x in parts]
    suf = [unpack(_dot(tri_up, x)) for x in parts]
    lane = lax.broadcasted_iota(jnp.int32, (CHUNK, LANES), 1)
    return [jnp.where(lane < 2 * N_HEADS, beta[s * CHUNK:(s + 1) * CHUNK, :],
                      jnp.where(lane < 3 * N_HEADS, pre[s], suf[s])) for s in range(n_sub)]


def _resident(shape):
    return pl.BlockSpec(shape, lambda i: (0,) * len(shape), pipeline_mode=pl.Buffered(1))


def _slab_order(kinds, slabs_per_group):
    heavy = [(g, s) for g, kind in enumerate(kinds) if kind != 'z' for s in range(slabs_per_group)]
    light = [(g, s) for g, kind in enumerate(kinds) if kind == 'z' for s in range(slabs_per_group)]
    every = max(len(heavy) // max(len(light), 1), 1)
    order = []
    for n, item in enumerate(heavy):
        order.append(item)
        if light and (n + 1) % every == 0:
            order.append(light.pop(0))
    return order + light


def _transpose_into(w_sc, wt_refs):
    for p, wt_ref in enumerate(wt_refs):
        for c0 in range(0, wt_ref.shape[0], MXU_COLS):
            w_sc[p, :, c0:c0 + MXU_COLS] = wt_ref[c0:c0 + MXU_COLS, :].T


def _inproj_kernel(x_ref, xb_ref, xa_ref, mod_ref, g_ref, wba_ref, cw_ref, a_ref, dt_ref, *rest,
                   tiles_per_seq, kinds):
    n_w = (len(kinds) + 1) // 2
    wt_refs = rest[:n_w]
    o_ref, gcol_ref, grow_ref, h_ref, halo_ref, w_sc = rest[n_w:]
    i = pl.program_id(0)

    @pl.when(i == 0)
    def _():
        _transpose_into(w_sc, wt_refs)

    w_refs = [w_sc.at[p] for p in range(n_w)]
    tm = o_ref.shape[0]
    shift, scale = mod_ref[:, 0:D_MODEL], mod_ref[:, D_MODEL:2 * D_MODEL]
    h_ref[...] = _norm_mod(x_ref[...], g_ref[...], shift, scale).astype(BF16)
    halo = jnp.concatenate([xb_ref[...], xa_ref[...]], axis=0)
    halo_ref[...] = _norm_mod(halo, g_ref[...], shift, scale).astype(BF16)
    ba = lax.dot_general(h_ref[...], wba_ref[...], (((1,), (1,)), ((), ())), preferred_element_type=F32)
    for s, col in enumerate(_gate_columns(ba, a_ref[...], dt_ref[...])):
        rows = slice(s * CHUNK, (s + 1) * CHUNK)
        gcol_ref[rows, :] = col
        grow_ref[:, rows] = col.T

    tile_in_seq = i % tiles_per_seq
    slab = MXU_COLS
    row8 = lax.broadcasted_iota(jnp.int32, (8, slab), 0)
    n_parts = max(tm // IN_PART_ROWS, 1)
    part = tm // n_parts
    tiled = (part // 8, 8, slab)
    for g, s in _slab_order(kinds, B_WIDTH // slab):
        kind = kinds[g]
        col0 = g * B_WIDTH + s * slab
        cols = slice(col0, col0 + slab)
        w_cols = slice(col0 % (2 * B_WIDTH), col0 % (2 * B_WIDTH) + slab)
        w_ref = w_refs[g // 2]
        ps = [_dot(h_ref[n * part:(n + 1) * part, :], w_ref[:, w_cols]) for n in range(n_parts)]
        if kind != 'z':
            edge = _dot(halo_ref[...], w_ref[:, w_cols])
            taps = [jnp.broadcast_to(cw_ref[r:r + 1, cols], (8, slab)) for r in range(3)]
        for n, p in enumerate(ps):
            if kind != 'z':
                prev_row = ps[n - 1][part - 1:part] if n > 0 else jnp.where(tile_in_seq > 0, edge[7:8, :], 0.0)
                next_row = (ps[n + 1][0:1] if n + 1 < n_parts
                            else jnp.where(tile_in_seq < tiles_per_seq - 1, edge[8:9, :], 0.0))
                xp = pltpu.roll(p, 1, 0)
                xp = jnp.concatenate([jnp.where(row8 == 0, prev_row, xp[0:8]), xp[8:]], axis=0)
                xn = pltpu.roll(p, part - 1, 0)
                xn = jnp.concatenate([xn[:part - 8], jnp.where(row8 == 7, next_row, xn[part - 8:])], axis=0)
                p = (xp.reshape(tiled) * taps[0] + p.reshape(tiled) * taps[1]
                     + xn.reshape(tiled) * taps[2]).reshape(part, slab)
            y = _silu(p)
            rows = slice(n * part, (n + 1) * part)
            for hd in range(slab // HEAD_DIM):
                head = y[:, hd * HEAD_DIM:(hd + 1) * HEAD_DIM]
                if kind in ('k', 'q'):
                    head = _l2norm(head)
                if kind == 'q':
                    head = head * (HEAD_DIM ** -0.5)
                o_ref[rows, col0 + hd * HEAD_DIM:col0 + (hd + 1) * HEAD_DIM] = head.astype(o_ref.dtype)


def _inproj_call(x2d, mod3, mod_row_of_tile, norm_g, w_pairs, w_ba, conv_w, a_lane, dt_lane, kinds, tm,
                 tiles_per_seq, name):
    m = x2d.shape[0]
    n = B_WIDTH * len(kinds)
    assert len(kinds) == 2 * len(w_pairs) and tm % CHUNK == 0
    assert all(kind != 'z' for kind in kinds[:conv_w.shape[1] // B_WIDTH]) and 'z' not in kinds[:-1]
    last_halo = m // 8 - 1

    def pair_spec(block):
        return pl.BlockSpec((2 * B_WIDTH, D_MODEL), lambda i: (block, 0), pipeline_mode=pl.Buffered(1))

    return pl.pallas_call(
        functools.partial(_inproj_kernel, tiles_per_seq=tiles_per_seq, kinds=kinds),
        grid=(m // tm,),
        in_specs=[pl.BlockSpec((tm, D_MODEL), lambda i: (i, 0)),
                  pl.BlockSpec((8, D_MODEL), lambda i: (jnp.maximum(i * (tm // 8) - 1, 0), 0)),
                  pl.BlockSpec((8, D_MODEL), lambda i: (jnp.minimum((i + 1) * (tm // 8), last_halo), 0)),
                  pl.BlockSpec((None, 1, 6 * D_MODEL), lambda i: (mod_row_of_tile(i), 0, 0)),
                  _resident((1, D_MODEL)),
                  _resident((LANES, D_MODEL)),
                  _resident(conv_w.shape),
                  _resident((1, LANES)), _resident((1, LANES))] + [pair_spec(blk) for _, blk in w_pairs],
        out_specs=[pl.BlockSpec((tm, n), lambda i: (i, 0)),
                   pl.BlockSpec((tm, LANES), lambda i: (i, 0)),
                   pl.BlockSpec((LANES, tm), lambda i: (0, i))],
        out_shape=[jax.ShapeDtypeStruct((m, n), BF16), jax.ShapeDtypeStruct((m, LANES), F32),
                   jax.ShapeDtypeStruct((LANES, m), F32)],
        scratch_shapes=[pltpu.VMEM((tm, D_MODEL), BF16), pltpu.VMEM((16, D_MODEL), BF16),
                        pltpu.VMEM((len(w_pairs), D_MODEL, 2 * B_WIDTH), BF16)],
        compiler_params=pltpu.CompilerParams(
            dimension_semantics=("arbitrary",), vmem_limit_bytes=VMEM_LIMIT),
        name=name,
    )(x2d, x2d, x2d, mod3, norm_g, w_ba, conv_w, a_lane, dt_lane, *[arr for arr, _ in w_pairs])


def _run_interleaved(*gens):
    results = [None] * len(gens)
    live = list(range(len(gens)))
    while live:
        for idx in list(live):
            try:
                next(gens[idx])
            except StopIteration as stop:
                results[idx] = stop.value
                live.remove(idx)
    return results


TRI_INCL, TRI_STRICT, TRI_BLK16, TRI_EYE = 0, 2, 4, 5
MERGE_SIZES = (32, 64, 128)


def _mask_constants():
    ri = lax.broadcasted_iota(jnp.int32, (CHUNK, CHUNK), 0)
    ci = lax.broadcasted_iota(jnp.int32, (CHUNK, CHUNK), 1)

    def same_block(size):
        return (ri // size) == (ci // size)

    tri = jnp.stack([ri >= ci, ri <= ci, ri > ci, ri < ci, same_block(16), ri == ci]).astype(F32)
    off = jnp.stack([same_block(s) & ~same_block(s // 2) for s in MERGE_SIZES]).astype(BF16)
    return tri, off


def _tri_inverse_stages(n_mats, tri_ref, off_ref):
    n16 = [n * tri_ref[TRI_BLK16] for n in n_mats]
    ts = [tri_ref[TRI_EYE] + n for n in n16]
    pbs = [n.astype(BF16) for n in n16]
    pbs = [_dot16(p, p) for p in pbs]
    nbs = [n.astype(BF16) for n in n_mats]
    yield
    for _ in range(2):
        ts = [t + _dot(t.astype(BF16), p) for t, p in zip(ts, pbs)]
        pbs = [_dot16(p, p) for p in pbs]
        yield
    ts = [t + _dot(t.astype(BF16), p) for t, p in zip(ts, pbs)]
    yield
    tbs = [t.astype(BF16) for t in ts]
    for level in range(len(MERGE_SIZES)):
        xs = [_dot16(nb * off_ref[level], tb) for nb, tb in zip(nbs, tbs)]
        yield
        tbs = [tb + _dot16(tb, x) for tb, x in zip(tbs, xs)]
        yield
    return tbs


def _chunk_start(c):
    return c * CHUNK if isinstance(c, int) else pl.multiple_of(c * CHUNK, CHUNK)


def _prep_stages(chunks, h, k_ref, v_ref, q_ref, gcol_ref, grow_ref, tri_ref, off_ref, early, late):
    lane = lax.broadcasted_iota(jnp.int32, (CHUNK, LANES), 1)
    head_row = lax.broadcasted_iota(jnp.int32, (N_HEADS, CHUNK), 0)

    acts, prods = [], []
    for c in chunks:
        rows = pl.ds(_chunk_start(c), CHUNK)
        kb16 = k_ref[rows, :]
        k = kb16.astype(F32)
        kt16 = k.T.astype(BF16)
        v = v_ref[rows, :].astype(F32)
        if q_ref is not None:
            qb16 = q_ref[rows, :]
            aa = _dot(jnp.concatenate([kb16, qb16], axis=0), kt16)
            acts.append((k, v, qb16.astype(F32)))
            prods.append((aa[:CHUNK], aa[CHUNK:]))
        else:
            acts.append((k, v, None))
            prods.append((_dot(kb16, kt16), None))
    yield

    neg_ms, rhss = [], []
    for i, (c, (k, v, q), (a_kk, a_qk)) in enumerate(zip(chunks, acts, prods)):
        r0 = _chunk_start(c)
        gates = gcol_ref[pl.ds(r0, CHUNK), :]
        for d in range(2):
            beta = jnp.sum(jnp.where(lane == d * N_HEADS + h, gates, 0.0), axis=1, keepdims=True)
            gc = jnp.sum(jnp.where(lane == (2 + d) * N_HEADS + h, gates, 0.0), axis=1, keepdims=True)
            all_heads = grow_ref[(2 + d) * N_HEADS:(3 + d) * N_HEADS, pl.ds(r0, CHUNK)]
            gc_row = jnp.sum(jnp.where(head_row == h, all_heads, 0.0), axis=0, keepdims=True)
            decay_ij = jnp.exp(jnp.minimum(gc - gc_row, 0.0))
            neg_ms.append(a_kk * (-beta) * (decay_ij * tri_ref[TRI_STRICT + d]))
            eg = jnp.exp(gc)
            rhss.append(jnp.concatenate([k * (beta * eg), v * beta], axis=1).astype(BF16))
            g_end = gc[CHUNK - 1:CHUNK, :] if d == 0 else gc[0:1, :]
            kd = k * jnp.exp(g_end - gc)
            decay = jnp.broadcast_to(jnp.exp(g_end), (8, LANES))
            if q is not None:
                early(i, d, kd, decay, q * eg, a_qk * (decay_ij * tri_ref[TRI_INCL + d]))
            else:
                early(i, d, kd, decay, None, None)

    t_invs = yield from _tri_inverse_stages(neg_ms, tri_ref, off_ref)
    wus = [_dot(t, rhs) for t, rhs in zip(t_invs, rhss)]
    yield
    for n, wu in enumerate(wus):
        late(n // 2, n % 2, wu[:, :HEAD_DIM], wu[:, HEAD_DIM:])


def _delta_kernel(pk_ref, pv_ref, pq_ref, gcol_ref, grow_ref, ck_ref, cv_ref, cgcol_ref, cgrow_ref,
                  sz_ref, onorm_ref, tri_ref, off_ref, og_ref,
                  wq_sc, kq_sc, u_sc, ge_sc, s0_sc, o_sc, *, n_chunks, n_ctx_chunks, n_heads_total):
    t = pl.program_id(0)
    h = jnp.minimum(t, n_heads_total - 1) % N_HEADS
    slot_p = t % 2
    slot_s = 1 - slot_p

    @pl.when(t == 0)
    def _():
        wq_sc[1] = jnp.zeros(wq_sc.shape[1:], wq_sc.dtype)
        kq_sc[1] = jnp.zeros(kq_sc.shape[1:], kq_sc.dtype)
        u_sc[1] = jnp.zeros(u_sc.shape[1:], u_sc.dtype)
        ge_sc[1] = jnp.zeros(ge_sc.shape[1:], ge_sc.dtype)
        s0_sc[1] = jnp.zeros(s0_sc.shape[1:], s0_sc.dtype)

    def context_states():
        kept = {}

        def early(c, d, kd, decay, qg, qk):
            kept[c, d] = [kd.T.astype(BF16), decay]

        def late(c, d, w, u):
            kept[c, d] += [w.astype(BF16), u]

        yield from _prep_stages(list(range(n_ctx_chunks)), h, ck_ref, cv_ref, None, cgcol_ref, cgrow_ref,
                                tri_ref, off_ref, early, late)
        states = [jnp.zeros((HEAD_DIM, HEAD_DIM), F32)] * 2
        for c in range(n_ctx_chunks):
            steps = [kept[c, 0], kept[n_ctx_chunks - 1 - c, 1]]
            ws = [_dot(w, s.astype(BF16)) for (_, _, w, _), s in zip(steps, states)]
            yield
            kv = [_dot(kd_t, (u - x).astype(BF16)) for (kd_t, _, _, u), x in zip(steps, ws)]
            yield
            states = [s * decay[0:1, :] + x for s, (_, decay, _, _), x in zip(states, steps, kv)]
        for d in range(2):
            s0_sc[slot_p, d] = states[d]

    def prepare(i):
        chunks = [i * GROUP + g for g in range(GROUP)]

        def early(n, d, kd, decay, qg, qk):
            c = chunks[n]
            kq_sc[slot_p, d, c, 0:CHUNK, :] = kd.T.astype(BF16)
            kq_sc[slot_p, d, c, CHUNK:2 * CHUNK, :] = qk.astype(BF16)
            wq_sc[slot_p, d, c, CHUNK:2 * CHUNK, :] = qg.astype(BF16)
            ge_sc[slot_p, d, c] = decay

        def late(n, d, w, u):
            c = chunks[n]
            wq_sc[slot_p, d, c, 0:CHUNK, :] = w.astype(BF16)
            u_sc[slot_p, d, pl.ds(_chunk_start(c), CHUNK), :] = u

        yield from _prep_stages(chunks, h, pk_ref, pv_ref, pq_ref, gcol_ref, grow_ref, tri_ref, off_ref,
                                early, late)

    def finish(c, o):
        r0 = _chunk_start(c)
        y = o * lax.rsqrt(jnp.mean(o * o, axis=-1, keepdims=True) + NORM_EPS) * onorm_ref[...]
        og_ref[pl.ds(r0, CHUNK), :] = (y * sz_ref[pl.ds(r0, CHUNK), :].astype(F32)).astype(og_ref.dtype)

    def scan(i, states, second_half):
        for g in range(GROUP):
            j = i * GROUP + g
            chunks = (j, n_chunks - 1 - j)
            ws = [_dot(wq_sc[slot_s, d, c], s.astype(BF16)) for d, (c, s) in enumerate(zip(chunks, states))]
            yield
            v_new = [u_sc[slot_s, d, pl.ds(_chunk_start(c), CHUNK), :] - x[:CHUNK]
                     for d, (c, x) in enumerate(zip(chunks, ws))]
            kv = [_dot(kq_sc[slot_s, d, c], vn.astype(BF16)) for d, (c, vn) in enumerate(zip(chunks, v_new))]
            yield
            states = [s * ge_sc[slot_s, d, c][0:1, :] + x[:CHUNK]
                      for d, (c, s, x) in enumerate(zip(chunks, states, kv))]
            for c, x, y in zip(chunks, ws, kv):
                o = x[CHUNK:] + y[CHUNK:]
                rows = pl.ds(_chunk_start(c), CHUNK)
                if second_half:
                    finish(c, o_sc[rows, :] + o)
                else:
                    o_sc[rows, :] = o
        return tuple(states)

    def body(second_half, i, states, *also):
        return _run_interleaved(scan(i, states, second_half), prepare(i), *also)[0]

    n_iter = n_chunks // GROUP
    states = (s0_sc[slot_s, 0], s0_sc[slot_s, 1])
    states = body(False, 0, states, context_states())
    for i in range(1, n_iter):
        states = body(i >= n_iter // 2, i, states)


def _delta_call(p3, gcol, grow, pc3, cgcol, cgrow, onorm_g):
    bsz, length, _ = p3.shape
    ctx_len = pc3.shape[1]
    n_chunks = length // CHUNK
    n_ctx_chunks = ctx_len // CHUNK
    n_heads_total = bsz * N_HEADS
    assert n_chunks % (2 * GROUP) == 0
    tri_masks, off_masks = _mask_constants()

    def prepared(t):
        t = jnp.minimum(t, n_heads_total - 1)
        return t // N_HEADS, t % N_HEADS

    def scanned(t):
        t = jnp.maximum(t - 1, 0)
        return t // N_HEADS, t % N_HEADS

    def head_cols(rows, first_block, which):
        def index(t):
            b, h = which(t)
            return b, 0, first_block + h
        return pl.BlockSpec((None, rows, LANES), index)

    def gates_col(rows):
        return pl.BlockSpec((None, rows, LANES), lambda t: (prepared(t)[0], 0, 0), pipeline_mode=pl.Buffered(1))

    def gates_row(rows):
        return pl.BlockSpec((LANES, rows), lambda t: (0, prepared(t)[0]), pipeline_mode=pl.Buffered(1))

    return pl.pallas_call(
        functools.partial(_delta_kernel, n_chunks=n_chunks, n_ctx_chunks=n_ctx_chunks,
                          n_heads_total=n_heads_total),
        grid=(n_heads_total + 1,),
        in_specs=[head_cols(length, 0, prepared), head_cols(length, N_HEADS, prepared),
                  head_cols(length, 2 * N_HEADS, prepared),
                  gates_col(length), gates_row(length),
                  head_cols(ctx_len, 0, prepared), head_cols(ctx_len, N_HEADS, prepared),
                  gates_col(ctx_len), gates_row(ctx_len),
                  head_cols(length, 3 * N_HEADS, scanned),
                  pl.BlockSpec((1, LANES), lambda t: (0, 0)),
                  pl.BlockSpec(tri_masks.shape, lambda t: (0, 0, 0), pipeline_mode=pl.Buffered(1)),
                  pl.BlockSpec(off_masks.shape, lambda t: (0, 0, 0), pipeline_mode=pl.Buffered(1))],
        out_specs=head_cols(length, 0, scanned),
        out_shape=jax.ShapeDtypeStruct((bsz, length, B_WIDTH), BF16),
        scratch_shapes=[pltpu.VMEM((2, 2, n_chunks, 2 * CHUNK, LANES), BF16),
                        pltpu.VMEM((2, 2, n_chunks, 2 * CHUNK, LANES), BF16),
                        pltpu.VMEM((2, 2, length, LANES), F32),
                        pltpu.VMEM((2, 2, n_chunks, 8, LANES), F32),
                        pltpu.VMEM((2, 2, HEAD_DIM, HEAD_DIM), F32),
                        pltpu.VMEM((length, LANES), F32)],
        compiler_params=pltpu.CompilerParams(
            dimension_semantics=("arbitrary",), vmem_limit_bytes=DELTA_VMEM_LIMIT),
        name="delta",
    )(p3, p3, p3, gcol, grow, pc3, pc3, cgcol, cgrow, p3, onorm_g, tri_masks, off_masks)


def _mix_kernel(x_ref, og_ref, mod_ref, g1_ref, wuv_t_ref, wg_t_ref, lng_ref, lnb_ref, ws_ref, bs_ref,
                wa_ref, wb_ref, wo_ref, o_ref, ua_ref, w_sc, *, n_parts):
    @pl.when(pl.program_id(0) == 0)
    def _():
        _transpose_into(w_sc, (wuv_t_ref, wg_t_ref))

    wuv_ref, wg_ref = w_sc.at[0], w_sc.at[1]
    tm = x_ref.shape[0]
    part = tm // n_parts
    early = []
    for n in range(n_parts):
        rows = slice(n * part, (n + 1) * part)
        x = x_ref[rows, :]
        h = _norm_mod(x, g1_ref[...], mod_ref[:, 0:D_MODEL], mod_ref[:, D_MODEL:2 * D_MODEL]).astype(BF16)
        early.append((x, _dot(h, wuv_ref[...]), _dot(h, wg_ref[...]), _dot(og_ref[rows, :], wb_ref[...])))
    for n, (x, uv, gates_pre, y_b) in enumerate(early):
        v = jax.nn.gelu(uv[:, D_MODEL:])
        vc = v - jnp.mean(v, axis=-1, keepdims=True)
        vn = vc * lax.rsqrt(jnp.mean(vc * vc, axis=-1, keepdims=True) + NORM_EPS) * lng_ref[...] + lnb_ref[...]
        vb = vn.astype(BF16)
        for c in range(part // A_CHUNK):
            rows = slice(c * A_CHUNK, (c + 1) * A_CHUNK)
            out_rows = slice(n * part + c * A_CHUNK, n * part + (c + 1) * A_CHUNK)
            for g in range(A_GROUPS):
                cols = slice(g * LANES, (g + 1) * LANES)
                s = _dot(ws_ref[g], vb[rows, cols]) + bs_ref[:, cols]
                ua_ref[out_rows, cols] = (jax.nn.gelu(uv[rows, cols]) * s).astype(BF16)
        rows = slice(n * part, (n + 1) * part)
        y_a = _dot(ua_ref[rows, :], wa_ref[...])
        gates = jax.nn.sigmoid(gates_pre)
        t = gates[:, :D_MODEL] * y_a + gates[:, D_MODEL:] * y_b
        y = _dot(t.astype(BF16), wo_ref[...])
        o_ref[rows, :] = x + mod_ref[:, 2 * D_MODEL:3 * D_MODEL] * y


def _mix_call(x2d, og, mod3, g1, w_tail16, ln_g, ln_b, ws16, bs_cols, wa16, wb16, wo16, tm, n_parts,
              tiles_per_batch):
    m = x2d.shape[0]
    assert tm % (n_parts * A_CHUNK) == 0
    rows = pl.BlockSpec((tm, D_MODEL), lambda i: (i, 0))

    def tail_block(block):
        return pl.BlockSpec((2 * D_MODEL, D_MODEL), lambda i: (block, 0), pipeline_mode=pl.Buffered(1))

    return pl.pallas_call(
        functools.partial(_mix_kernel, n_parts=n_parts),
        grid=(m // tm,),
        in_specs=[rows, rows,
                  pl.BlockSpec((None, 1, 6 * D_MODEL), lambda i: (i // tiles_per_batch, 0, 0)),
                  _resident((1, D_MODEL)),
                  tail_block(1), tail_block(2),
                  _resident((1, D_MODEL)), _resident((1, D_MODEL)),
                  _resident((A_GROUPS, A_CHUNK, A_CHUNK)), _resident((A_CHUNK, D_MODEL)),
                  _resident((D_MODEL, D_MODEL)), _resident((D_MODEL, D_MODEL)), _resident((D_MODEL, D_MODEL))],
        out_specs=rows,
        out_shape=jax.ShapeDtypeStruct((m, D_MODEL), F32),
        scratch_shapes=[pltpu.VMEM((tm, D_MODEL), BF16), pltpu.VMEM((2, D_MODEL, 2 * D_MODEL), BF16)],
        compiler_params=pltpu.CompilerParams(
            dimension_semantics=("arbitrary",), vmem_limit_bytes=VMEM_LIMIT),
        name="mix",
    )(x2d, og, mod3, g1, w_tail16, w_tail16, ln_g, ln_b, ws16, bs_cols, wa16, wb16, wo16)


def _ffn_kernel(x_ref, mod_ref, g2_ref, wup_ref, cw_ref, cb_ref, wd_ref, gf_ref, o_ref, *, n_parts):
    tm = x_ref.shape[0]
    part = tm // n_parts
    col_in_row = lax.broadcasted_iota(jnp.int32, (part, D_FF), 0) & (GRID_W - 1)
    xs, ups = [], []
    for n in range(n_parts):
        x = x_ref[n * part:(n + 1) * part, :]
        h = _norm_mod(x, g2_ref[...], mod_ref[:, 3 * D_MODEL:4 * D_MODEL],
                      mod_ref[:, 4 * D_MODEL:5 * D_MODEL]).astype(BF16)
        xs.append(x)
        ups.append((_dot(h, wup_ref[:, :D_FF]), _dot(h, wup_ref[:, D_FF:])))
    for n, (x, (a, b)) in enumerate(zip(xs, ups)):
        a_prev = jnp.where(col_in_row == 0, 0.0, pltpu.roll(a, 1, 0))
        a_next = jnp.where(col_in_row == GRID_W - 1, 0.0, pltpu.roll(a, part - 1, 0))
        ac = a_prev * cw_ref[0:1, :] + a * cw_ref[1:2, :] + a_next * cw_ref[2:3, :] + cb_ref[...]
        y = _dot((jax.nn.gelu(ac) * b).astype(BF16), wd_ref[...])
        x2 = x + mod_ref[:, 5 * D_MODEL:6 * D_MODEL] * y
        ms = jnp.mean(x2 * x2, axis=-1, keepdims=True)
        o_ref[n * part:(n + 1) * part, :] = x2 * lax.rsqrt(ms + NORM_EPS) * gf_ref[...]


def _ffn_call(x1, mod3, norm2_g, w_up16, conv_w, conv_b, w_down16, final_g, tm, n_parts, tiles_per_batch):
    m = x1.shape[0]
    assert tm % (n_parts * GRID_W) == 0
    return pl.pallas_call(
        functools.partial(_ffn_kernel, n_parts=n_parts),
        grid=(m // tm,),
        in_specs=[pl.BlockSpec((tm, D_MODEL), lambda i: (i, 0)),
                  pl.BlockSpec((None, 1, 6 * D_MODEL), lambda i: (i // tiles_per_batch, 0, 0)),
                  _resident((1, D_MODEL)),
                  _resident((D_MODEL, 2 * D_FF)),
                  _resident((3, D_FF)), _resident((1, D_FF)),
                  _resident((D_FF, D_MODEL)),
                  _resident((1, D_MODEL))],
        out_specs=pl.BlockSpec((tm, D_MODEL), lambda i: (i, 0)),
        out_shape=jax.ShapeDtypeStruct((m, D_MODEL), F32),
        compiler_params=pltpu.CompilerParams(
            dimension_semantics=("parallel",), vmem_limit_bytes=VMEM_LIMIT),
        name="ffn",
    )(x1, mod3, norm2_g, w_up16, conv_w, conv_b, w_down16, final_g)


def kernel(x, c, ctx, c_ctx, w_mod, b_mod, norm1_g, w_in, conv_qkv, a_log, dt_bias, onorm_g, w_proj_b,
           a_ln_g, a_ln_b, a_ws, a_bs, w_proj_a, w_out, norm2_g, w_up, ffn_conv_w, ffn_conv_b, w_down,
           final_g):
    bsz, length, _ = x.shape
    ctx_len = ctx.shape[1]
    assert w_mod.shape[0] == 1 and bsz <= 7
    assert length % TM_IN == 0 and length % TM == 0 and ctx_len % CHUNK == 0

    wt = jnp.swapaxes(w_in[0], 0, 1)
    w_kv = wt[:OFF_BA].astype(BF16)
    w_ba = jnp.pad(wt[OFF_BA:OFF_Q], ((0, LANES - 4 * N_HEADS), (0, 0))).astype(BF16)
    w_tail = wt[OFF_Q:].astype(BF16)
    lane_pad = (2 * N_HEADS, LANES - 4 * N_HEADS)
    a_lane = jnp.pad(jnp.exp(a_log[0].astype(F32)).reshape(-1), lane_pad).reshape(1, LANES)
    dt_lane = jnp.pad(dt_bias[0].astype(F32).reshape(-1), lane_pad).reshape(1, LANES)
    bs_cols = jnp.repeat(a_bs[0].T, LANES, axis=1)

    cond8 = jnp.concatenate([c, c_ctx[None, :], jnp.zeros((7 - bsz, D_MODEL), F32)], axis=0)
    mod3 = _mod_call(cond8, w_mod[0], b_mod[0]).reshape(8, 1, 6 * D_MODEL)

    x2d = x.reshape(bsz * length, D_MODEL)
    g1 = norm1_g[0].reshape(1, D_MODEL)
    tiles_per_seq = length // TM_IN
    p2d, gcol, grow = _inproj_call(x2d, mod3, lambda i: i // tiles_per_seq, g1, [(w_kv, 0), (w_tail, 0)], w_ba,
                                   conv_qkv[0], a_lane, dt_lane, ('k', 'v', 'q', 'z'), TM_IN, tiles_per_seq,
                                   "in_proj")
    pc2d, cgcol, cgrow = _inproj_call(ctx.reshape(bsz * ctx_len, D_MODEL), mod3, lambda i: bsz, g1, [(w_kv, 0)],
                                      w_ba, conv_qkv[0], a_lane, dt_lane, ('k', 'v'), ctx_len, 1, "in_proj_ctx")
    p3 = p2d.reshape(bsz, length, 4 * B_WIDTH)
    pc3 = pc2d.reshape(bsz, ctx_len, 2 * B_WIDTH)
    og = _delta_call(p3, gcol.reshape(bsz, length, LANES), grow, pc3, cgcol.reshape(bsz, ctx_len, LANES), cgrow,
                     onorm_g[0].reshape(1, HEAD_DIM))

    tiles_per_batch = length // TM
    x1 = _mix_call(x2d, og.reshape(bsz * length, B_WIDTH), mod3, g1, w_tail,
                   a_ln_g[0].reshape(1, -1), a_ln_b[0].reshape(1, -1), a_ws[0].astype(BF16), bs_cols,
                   w_proj_a[0].astype(BF16), w_proj_b[0].astype(BF16), w_out[0].astype(BF16), TM, MIX_ROW_PARTS,
                   tiles_per_batch)
    out = _ffn_call(x1, mod3, norm2_g[0].reshape(1, -1), w_up[0].astype(BF16), ffn_conv_w[0],
                    ffn_conv_b[0].reshape(1, -1), w_down[0].astype(BF16), final_g.reshape(1, -1),
                    TM, FFN_ROW_PARTS, tiles_per_batch)
    return out.reshape(bsz, length, D_MODEL)
```

```python
import functools

import jax
import jax.numpy as jnp
from jax import lax
from jax.experimental import pallas as pl
from jax.experimental.pallas import tpu as pltpu

F32 = jnp.float32
BF16 = jnp.bfloat16

D_MODEL = 1024
GRID_W = 64
NORM_EPS = 1e-6
N_HEADS = 8
HEAD_DIM = 128
B_WIDTH = N_HEADS * HEAD_DIM
A_GROUPS = 8
A_CHUNK = 128
D_FF = 2816
OFF_BA = 2 * B_WIDTH
OFF_Q = OFF_BA + 4 * N_HEADS

LANES = 128
MXU_COLS = 256
VMEM_BYTES = 64 * 1024 * 1024

CHUNK = 128
GROUP = 8
TM_IN = 1024
IN_PART_ROWS = 256
TM = 512
FFN_ROW_PARTS = 2
MIX_ROW_PARTS = 1
VMEM_LIMIT = 56 * 1024 * 1024
DELTA_VMEM_LIMIT = VMEM_BYTES - 2 * 1024 * 1024


def _silu(x):
    return x * jax.nn.sigmoid(x)


def _dot(a, b):
    return jnp.dot(a, b, preferred_element_type=F32)


def _dot16(a, b):
    return jnp.dot(a, b, preferred_element_type=F32).astype(BF16)


def _bf16_terms(x, n_terms):
    terms = []
    for _ in range(n_terms):
        terms.append(x.astype(BF16))
        x = x - terms[-1].astype(F32)
    return terms


def _mod_kernel(c_ref, w_ref, b_ref, o_ref):
    rows = c_ref.shape[0]
    terms = [t.astype(F32) for t in _bf16_terms(_silu(c_ref[...]), 3)]
    cond = jnp.concatenate(terms + [jnp.zeros((rows, D_MODEL), F32)], axis=0).astype(BF16)
    acc = sum(_dot(cond, w_term) for w_term in _bf16_terms(w_ref[...], 2))
    o_ref[...] = acc[0:rows] + acc[rows:2 * rows] + acc[2 * rows:3 * rows] + b_ref[...]


def _mod_call(cond8, w_mod, b_mod):
    n = w_mod.shape[1]
    tn = 1536
    return pl.pallas_call(
        _mod_kernel,
        grid=(n // tn,),
        in_specs=[pl.BlockSpec((8, D_MODEL), lambda j: (0, 0)),
                  pl.BlockSpec((D_MODEL, tn), lambda j: (0, j)),
                  pl.BlockSpec((1, tn), lambda j: (0, j))],
        out_specs=pl.BlockSpec((8, tn), lambda j: (0, j)),
        out_shape=jax.ShapeDtypeStruct((8, n), F32),
        name="mod",
    )(cond8, w_mod, b_mod.reshape(1, n))


def _norm_mod(x, g, shift, scale):
    ms = jnp.mean(x * x, axis=-1, keepdims=True)
    y = x * lax.rsqrt(ms + NORM_EPS) * g
    return y * (1.0 + scale) + shift


def _l2norm(t):
    return t * lax.rsqrt(jnp.sum(t * t, axis=-1, keepdims=True) + NORM_EPS)


def _gate_columns(ba, a_lane, dt_lane):
    n_sub = ba.shape[0] // CHUNK
    lane = lax.broadcasted_iota(jnp.int32, ba.shape, 1)
    ri = lax.broadcasted_iota(jnp.int32, (CHUNK, CHUNK), 0)
    ci = lax.broadcasted_iota(jnp.int32, (CHUNK, CHUNK), 1)
    tri_lo, tri_up = (ri >= ci).astype(F32), (ri <= ci).astype(F32)
    beta = jax.nn.sigmoid(ba)
    y = ba + dt_lane
    softplus = jnp.maximum(y, 0.0) + jnp.log1p(jnp.exp(-jnp.abs(y)))
    g = jnp.where((lane >= 2 * N_HEADS) & (lane < 4 * N_HEADS), -a_lane * softplus, 0.0)
    part = 4 * N_HEADS
    hi = g.astype(BF16).astype(F32)
    mid = (g - hi).astype(BF16).astype(F32)
    lo = (g - hi - mid).astype(BF16).astype(F32)
    packed = (hi + pltpu.roll(mid, part, 1) + pltpu.roll(lo, 2 * part, 1)).astype(BF16)

    def unpack(r):
        return r + pltpu.roll(r, LANES - part, 1) + pltpu.roll(r, LANES - 2 * part, 1)

    tri_lo, tri_up = tri_lo.astype(BF16), tri_up.astype(BF16)
    parts = [packed[s * CHUNK:(s + 1) * CHUNK, :] for s in range(n_sub)]
    pre = [unpack(_dot(tri_lo, x)) for x in parts]
    suf = [unpack(_dot(tri_up, x)) for x in parts]
    lane = lax.broadcasted_iota(jnp.int32, (CHUNK, LANES), 1)
    return [jnp.where(lane < 2 * N_HEADS, beta[s * CHUNK:(s + 1) * CHUNK, :],
                      jnp.where(lane < 3 * N_HEADS, pre[s], suf[s])) for s in range(n_sub)]


def _resident(shape):
    return pl.BlockSpec(shape, lambda i: (0,) * len(shape), pipeline_mode=pl.Buffered(1))


def _row_window(n_rows, first_row):
    return pl.BlockSpec((pl.Element(n_rows), pl.Element(D_MODEL)), lambda i: (first_row, 0),
                        pipeline_mode=pl.Buffered(1))


def _slab_order(kinds, slabs_per_group):
    heavy = [(g, s) for g, kind in enumerate(kinds) if kind != 'z' for s in range(slabs_per_group)]
    light = [(g, s) for g, kind in enumerate(kinds) if kind == 'z' for s in range(slabs_per_group)]
    every = max(len(heavy) // max(len(light), 1), 1)
    order = []
    for n, item in enumerate(heavy):
        order.append(item)
        if light and (n + 1) % every == 0:
            order.append(light.pop(0))
    return order + light


def _transpose_into(w_sc, wt_refs):
    for p, wt_ref in enumerate(wt_refs):
        for c0 in range(0, wt_ref.shape[0], MXU_COLS):
            w_sc[p, :, c0:c0 + MXU_COLS] = wt_ref[c0:c0 + MXU_COLS, :].T


def _inproj_kernel(x_ref, xb_ref, xa_ref, mod_ref, g_ref, wba_ref, cw_ref, a_ref, dt_ref, *rest,
                   tiles_per_seq, kinds):
    n_w = (len(kinds) + 1) // 2
    wt_refs = rest[:n_w]
    o_ref, gcol_ref, grow_ref, h_ref, halo_ref, w_sc = rest[n_w:]
    i = pl.program_id(0)

    @pl.when(i == 0)
    def _():
        _transpose_into(w_sc, wt_refs)

    w_refs = [w_sc.at[p] for p in range(n_w)]
    tm = o_ref.shape[0]
    shift, scale = mod_ref[:, 0:D_MODEL], mod_ref[:, D_MODEL:2 * D_MODEL]
    h_ref[...] = _norm_mod(x_ref[...], g_ref[...], shift, scale).astype(BF16)
    halo = jnp.concatenate([xb_ref[...], xa_ref[...]], axis=0)
    halo_ref[...] = _norm_mod(halo, g_ref[...], shift, scale).astype(BF16)
    ba = lax.dot_general(h_ref[...], wba_ref[...], (((1,), (1,)), ((), ())), preferred_element_type=F32)
    for s, col in enumerate(_gate_columns(ba, a_ref[...], dt_ref[...])):
        rows = slice(s * CHUNK, (s + 1) * CHUNK)
        gcol_ref[rows, :] = col
        grow_ref[:, rows] = col.T

    tile_in_seq = i % tiles_per_seq
    slab = MXU_COLS
    row8 = lax.broadcasted_iota(jnp.int32, (8, slab), 0)
    n_parts = max(tm // IN_PART_ROWS, 1)
    part = tm // n_parts
    tiled = (part // 8, 8, slab)
    for g, s in _slab_order(kinds, B_WIDTH // slab):
        kind = kinds[g]
        col0 = g * B_WIDTH + s * slab
        cols = slice(col0, col0 + slab)
        w_cols = slice(col0 % (2 * B_WIDTH), col0 % (2 * B_WIDTH) + slab)
        w_ref = w_refs[g // 2]
        ps = [_dot(h_ref[n * part:(n + 1) * part, :], w_ref[:, w_cols]) for n in range(n_parts)]
        if kind != 'z':
            edge = _dot(halo_ref[...], w_ref[:, w_cols])
            taps = [jnp.broadcast_to(cw_ref[r:r + 1, cols], (8, slab)) for r in range(3)]
        for n, p in enumerate(ps):
            if kind != 'z':
                prev_row = ps[n - 1][part - 1:part] if n > 0 else jnp.where(tile_in_seq > 0, edge[7:8, :], 0.0)
                next_row = (ps[n + 1][0:1] if n + 1 < n_parts
                            else jnp.where(tile_in_seq < tiles_per_seq - 1, edge[8:9, :], 0.0))
                xp = pltpu.roll(p, 1, 0)
                xp = jnp.concatenate([jnp.where(row8 == 0, prev_row, xp[0:8]), xp[8:]], axis=0)
                xn = pltpu.roll(p, part - 1, 0)
                xn = jnp.concatenate([xn[:part - 8], jnp.where(row8 == 7, next_row, xn[part - 8:])], axis=0)
                p = (xp.reshape(tiled) * taps[0] + p.reshape(tiled) * taps[1]
                     + xn.reshape(tiled) * taps[2]).reshape(part, slab)
            y = _silu(p)
            rows = slice(n * part, (n + 1) * part)
            for hd in range(slab // HEAD_DIM):
                head = y[:, hd * HEAD_DIM:(hd + 1) * HEAD_DIM]
                if kind in ('k', 'q'):
                    head = _l2norm(head)
                if kind == 'q':
                    head = head * (HEAD_DIM ** -0.5)
                o_ref[rows, col0 + hd * HEAD_DIM:col0 + (hd + 1) * HEAD_DIM] = head.astype(o_ref.dtype)


def _inproj_call(x2d, mod3, mod_row_of_tile, norm_g, w_pairs, w_ba, conv_w, a_lane, dt_lane, kinds, tm,
                 tiles_per_seq, name):
    m = x2d.shape[0]
    n = B_WIDTH * len(kinds)
    assert len(kinds) == 2 * len(w_pairs) and tm % CHUNK == 0
    assert all(kind != 'z' for kind in kinds[:conv_w.shape[1] // B_WIDTH]) and 'z' not in kinds[:-1]
    last_halo = m // 8 - 1

    def pair_spec(first_row):
        return _row_window(2 * B_WIDTH, first_row)

    return pl.pallas_call(
        functools.partial(_inproj_kernel, tiles_per_seq=tiles_per_seq, kinds=kinds),
        grid=(m // tm,),
        in_specs=[pl.BlockSpec((tm, D_MODEL), lambda i: (i, 0)),
                  pl.BlockSpec((8, D_MODEL), lambda i: (jnp.maximum(i * (tm // 8) - 1, 0), 0)),
                  pl.BlockSpec((8, D_MODEL), lambda i: (jnp.minimum((i + 1) * (tm // 8), last_halo), 0)),
                  pl.BlockSpec((None, 1, 6 * D_MODEL), lambda i: (mod_row_of_tile(i), 0, 0)),
                  _resident((1, D_MODEL)),
                  _row_window(LANES, w_ba[1]),
                  _resident(conv_w.shape),
                  _resident((1, LANES)), _resident((1, LANES))] + [pair_spec(blk) for _, blk in w_pairs],
        out_specs=[pl.BlockSpec((tm, n), lambda i: (i, 0)),
                   pl.BlockSpec((tm, LANES), lambda i: (i, 0)),
                   pl.BlockSpec((LANES, tm), lambda i: (0, i))],
        out_shape=[jax.ShapeDtypeStruct((m, n), BF16), jax.ShapeDtypeStruct((m, LANES), F32),
                   jax.ShapeDtypeStruct((LANES, m), F32)],
        scratch_shapes=[pltpu.VMEM((tm, D_MODEL), BF16), pltpu.VMEM((16, D_MODEL), BF16),
                        pltpu.VMEM((len(w_pairs), D_MODEL, 2 * B_WIDTH), BF16)],
        compiler_params=pltpu.CompilerParams(
            dimension_semantics=("arbitrary",), vmem_limit_bytes=VMEM_LIMIT),
        name=name,
    )(x2d, x2d, x2d, mod3, norm_g, w_ba[0], conv_w, a_lane, dt_lane, *[arr for arr, _ in w_pairs])


def _run_interleaved(*gens):
    results = [None] * len(gens)
    live = list(range(len(gens)))
    while live:
        for idx in list(live):
            try:
                next(gens[idx])
            except StopIteration as stop:
                results[idx] = stop.value
                live.remove(idx)
    return results


TRI_INCL, TRI_STRICT, TRI_BLK16, TRI_EYE = 0, 2, 4, 5
MERGE_SIZES = (32, 64, 128)


def _mask_constants():
    ri = lax.broadcasted_iota(jnp.int32, (CHUNK, CHUNK), 0)
    ci = lax.broadcasted_iota(jnp.int32, (CHUNK, CHUNK), 1)

    def same_block(size):
        return (ri // size) == (ci // size)

    tri = jnp.stack([ri >= ci, ri <= ci, ri > ci, ri < ci, same_block(16), ri == ci]).astype(F32)
    off = jnp.stack([same_block(s) & ~same_block(s // 2) for s in MERGE_SIZES]).astype(BF16)
    return tri, off


def _tri_inverse_stages(n_mats, tri_ref, off_ref):
    n16 = [n * tri_ref[TRI_BLK16] for n in n_mats]
    ts = [tri_ref[TRI_EYE] + n for n in n16]
    pbs = [n.astype(BF16) for n in n16]
    pbs = [_dot16(p, p) for p in pbs]
    nbs = [n.astype(BF16) for n in n_mats]
    yield
    for _ in range(2):
        ts = [t + _dot(t.astype(BF16), p) for t, p in zip(ts, pbs)]
        pbs = [_dot16(p, p) for p in pbs]
        yield
    ts = [t + _dot(t.astype(BF16), p) for t, p in zip(ts, pbs)]
    yield
    tbs = [t.astype(BF16) for t in ts]
    for level in range(len(MERGE_SIZES)):
        xs = [_dot16(nb * off_ref[level], tb) for nb, tb in zip(nbs, tbs)]
        yield
        tbs = [tb + _dot16(tb, x) for tb, x in zip(tbs, xs)]
        yield
    return tbs


def _chunk_start(c):
    return c * CHUNK if isinstance(c, int) else pl.multiple_of(c * CHUNK, CHUNK)


def _prep_stages(chunks, h, k_ref, v_ref, q_ref, gcol_ref, grow_ref, tri_ref, off_ref, early, late):
    lane = lax.broadcasted_iota(jnp.int32, (CHUNK, LANES), 1)
    head_row = lax.broadcasted_iota(jnp.int32, (N_HEADS, CHUNK), 0)

    acts, prods = [], []
    for c in chunks:
        rows = pl.ds(_chunk_start(c), CHUNK)
        kb16 = k_ref[rows, :]
        k = kb16.astype(F32)
        kt16 = k.T.astype(BF16)
        v = v_ref[rows, :].astype(F32)
        if q_ref is not None:
            qb16 = q_ref[rows, :]
            aa = _dot(jnp.concatenate([kb16, qb16], axis=0), kt16)
            acts.append((k, v, qb16.astype(F32)))
            prods.append((aa[:CHUNK], aa[CHUNK:]))
        else:
            acts.append((k, v, None))
            prods.append((_dot(kb16, kt16), None))
    yield

    neg_ms, rhss = [], []
    for i, (c, (k, v, q), (a_kk, a_qk)) in enumerate(zip(chunks, acts, prods)):
        r0 = _chunk_start(c)
        gates = gcol_ref[pl.ds(r0, CHUNK), :]
        for d in range(2):
            beta = jnp.sum(jnp.where(lane == d * N_HEADS + h, gates, 0.0), axis=1, keepdims=True)
            gc = jnp.sum(jnp.where(lane == (2 + d) * N_HEADS + h, gates, 0.0), axis=1, keepdims=True)
            all_heads = grow_ref[(2 + d) * N_HEADS:(3 + d) * N_HEADS, pl.ds(r0, CHUNK)]
            gc_row = jnp.sum(jnp.where(head_row == h, all_heads, 0.0), axis=0, keepdims=True)
            decay_ij = jnp.exp(jnp.minimum(gc - gc_row, 0.0))
            neg_ms.append(a_kk * (-beta) * (decay_ij * tri_ref[TRI_STRICT + d]))
            eg = jnp.exp(gc)
            rhss.append(jnp.concatenate([k * (beta * eg), v * beta], axis=1).astype(BF16))
            g_end = gc[CHUNK - 1:CHUNK, :] if d == 0 else gc[0:1, :]
            kd = k * jnp.exp(g_end - gc)
            decay = jnp.broadcast_to(jnp.exp(g_end), (8, LANES))
            if q is not None:
                early(i, d, kd, decay, q * eg, a_qk * (decay_ij * tri_ref[TRI_INCL + d]))
            else:
                early(i, d, kd, decay, None, None)

    t_invs = yield from _tri_inverse_stages(neg_ms, tri_ref, off_ref)
    wus = [_dot(t, rhs) for t, rhs in zip(t_invs, rhss)]
    yield
    for n, wu in enumerate(wus):
        late(n // 2, n % 2, wu[:, :HEAD_DIM], wu[:, HEAD_DIM:])


def _delta_kernel(pk_ref, pv_ref, pq_ref, gcol_ref, grow_ref, ck_ref, cv_ref, cgcol_ref, cgrow_ref,
                  sz_ref, onorm_ref, tri_ref, off_ref, og_ref,
                  wq_sc, kq_sc, u_sc, ge_sc, s0_sc, o_sc, *, n_chunks, n_ctx_chunks, n_heads_total):
    t = pl.program_id(0)
    h = jnp.minimum(t, n_heads_total - 1) % N_HEADS
    slot_p = t % 2
    slot_s = 1 - slot_p

    @pl.when(t == 0)
    def _():
        wq_sc[1] = jnp.zeros(wq_sc.shape[1:], wq_sc.dtype)
        kq_sc[1] = jnp.zeros(kq_sc.shape[1:], kq_sc.dtype)
        u_sc[1] = jnp.zeros(u_sc.shape[1:], u_sc.dtype)
        ge_sc[1] = jnp.zeros(ge_sc.shape[1:], ge_sc.dtype)
        s0_sc[1] = jnp.zeros(s0_sc.shape[1:], s0_sc.dtype)

    def context_states():
        kept = {}

        def early(c, d, kd, decay, qg, qk):
            kept[c, d] = [kd.T.astype(BF16), decay]

        def late(c, d, w, u):
            kept[c, d] += [w.astype(BF16), u]

        yield from _prep_stages(list(range(n_ctx_chunks)), h, ck_ref, cv_ref, None, cgcol_ref, cgrow_ref,
                                tri_ref, off_ref, early, late)
        states = [jnp.zeros((HEAD_DIM, HEAD_DIM), F32)] * 2
        for c in range(n_ctx_chunks):
            steps = [kept[c, 0], kept[n_ctx_chunks - 1 - c, 1]]
            ws = [_dot(w, s.astype(BF16)) for (_, _, w, _), s in zip(steps, states)]
            yield
            kv = [_dot(kd_t, (u - x).astype(BF16)) for (kd_t, _, _, u), x in zip(steps, ws)]
            yield
            states = [s * decay[0:1, :] + x for s, (_, decay, _, _), x in zip(states, steps, kv)]
        for d in range(2):
            s0_sc[slot_p, d] = states[d]

    def prepare(i):
        chunks = [i * GROUP + g for g in range(GROUP)]

        def early(n, d, kd, decay, qg, qk):
            c = chunks[n]
            kq_sc[slot_p, d, c, 0:CHUNK, :] = kd.T.astype(BF16)
            kq_sc[slot_p, d, c, CHUNK:2 * CHUNK, :] = qk.astype(BF16)
            wq_sc[slot_p, d, c, CHUNK:2 * CHUNK, :] = qg.astype(BF16)
            ge_sc[slot_p, d, c] = decay

        def late(n, d, w, u):
            c = chunks[n]
            wq_sc[slot_p, d, c, 0:CHUNK, :] = w.astype(BF16)
            u_sc[slot_p, d, pl.ds(_chunk_start(c), CHUNK), :] = u

        yield from _prep_stages(chunks, h, pk_ref, pv_ref, pq_ref, gcol_ref, grow_ref, tri_ref, off_ref,
                                early, late)

    def finish(c, o):
        r0 = _chunk_start(c)
        y = o * lax.rsqrt(jnp.mean(o * o, axis=-1, keepdims=True) + NORM_EPS) * onorm_ref[...]
        og_ref[pl.ds(r0, CHUNK), :] = (y * sz_ref[pl.ds(r0, CHUNK), :].astype(F32)).astype(og_ref.dtype)

    def scan(i, states, second_half):
        for g in range(GROUP):
            j = i * GROUP + g
            chunks = (j, n_chunks - 1 - j)
            ws = [_dot(wq_sc[slot_s, d, c], s.astype(BF16)) for d, (c, s) in enumerate(zip(chunks, states))]
            yield
            v_new = [u_sc[slot_s, d, pl.ds(_chunk_start(c), CHUNK), :] - x[:CHUNK]
                     for d, (c, x) in enumerate(zip(chunks, ws))]
            kv = [_dot(kq_sc[slot_s, d, c], vn.astype(BF16)) for d, (c, vn) in enumerate(zip(chunks, v_new))]
            yield
            states = [s * ge_sc[slot_s, d, c][0:1, :] + x[:CHUNK]
                      for d, (c, s, x) in enumerate(zip(chunks, states, kv))]
            for c, x, y in zip(chunks, ws, kv):
                o = x[CHUNK:] + y[CHUNK:]
                rows = pl.ds(_chunk_start(c), CHUNK)
                if second_half:
                    finish(c, o_sc[rows, :] + o)
                else:
                    o_sc[rows, :] = o
        return tuple(states)

    def body(second_half, i, states, *also):
        return _run_interleaved(scan(i, states, second_half), prepare(i), *also)[0]

    n_iter = n_chunks // GROUP
    states = (s0_sc[slot_s, 0], s0_sc[slot_s, 1])
    states = body(False, 0, states, context_states())
    for i in range(1, n_iter):
        states = body(i >= n_iter // 2, i, states)


def _delta_call(p3, gcol, grow, pc3, cgcol, cgrow, onorm_g):
    bsz, length, _ = p3.shape
    ctx_len = pc3.shape[1]
    n_chunks = length // CHUNK
    n_ctx_chunks = ctx_len // CHUNK
    n_heads_total = bsz * N_HEADS
    assert n_chunks % (2 * GROUP) == 0
    tri_masks, off_masks = _mask_constants()

    def prepared(t):
        t = jnp.minimum(t, n_heads_total - 1)
        return t // N_HEADS, t % N_HEADS

    def scanned(t):
        t = jnp.maximum(t - 1, 0)
        return t // N_HEADS, t % N_HEADS

    def head_cols(rows, first_block, which):
        def index(t):
            b, h = which(t)
            return b, 0, first_block + h
        return pl.BlockSpec((None, rows, LANES), index)

    def gates_col(rows):
        return pl.BlockSpec((None, rows, LANES), lambda t: (prepared(t)[0], 0, 0), pipeline_mode=pl.Buffered(1))

    def gates_row(rows):
        return pl.BlockSpec((LANES, rows), lambda t: (0, prepared(t)[0]), pipeline_mode=pl.Buffered(1))

    return pl.pallas_call(
        functools.partial(_delta_kernel, n_chunks=n_chunks, n_ctx_chunks=n_ctx_chunks,
                          n_heads_total=n_heads_total),
        grid=(n_heads_total + 1,),
        in_specs=[head_cols(length, 0, prepared), head_cols(length, N_HEADS, prepared),
                  head_cols(length, 2 * N_HEADS, prepared),
                  gates_col(length), gates_row(length),
                  head_cols(ctx_len, 0, prepared), head_cols(ctx_len, N_HEADS, prepared),
                  gates_col(ctx_len), gates_row(ctx_len),
                  head_cols(length, 3 * N_HEADS, scanned),
                  pl.BlockSpec((1, LANES), lambda t: (0, 0)),
                  pl.BlockSpec(tri_masks.shape, lambda t: (0, 0, 0), pipeline_mode=pl.Buffered(1)),
                  pl.BlockSpec(off_masks.shape, lambda t: (0, 0, 0), pipeline_mode=pl.Buffered(1))],
        out_specs=head_cols(length, 0, scanned),
        out_shape=jax.ShapeDtypeStruct((bsz, length, B_WIDTH), BF16),
        scratch_shapes=[pltpu.VMEM((2, 2, n_chunks, 2 * CHUNK, LANES), BF16),
                        pltpu.VMEM((2, 2, n_chunks, 2 * CHUNK, LANES), BF16),
                        pltpu.VMEM((2, 2, length, LANES), F32),
                        pltpu.VMEM((2, 2, n_chunks, 8, LANES), F32),
                        pltpu.VMEM((2, 2, HEAD_DIM, HEAD_DIM), F32),
                        pltpu.VMEM((length, LANES), F32)],
        compiler_params=pltpu.CompilerParams(
            dimension_semantics=("arbitrary",), vmem_limit_bytes=DELTA_VMEM_LIMIT),
        name="delta",
    )(p3, p3, p3, gcol, grow, pc3, pc3, cgcol, cgrow, p3, onorm_g, tri_masks, off_masks)


def _mix_kernel(x_ref, og_ref, mod_ref, g1_ref, wuv_t_ref, wg_t_ref, lng_ref, lnb_ref, ws_ref, bs_ref,
                wa_ref, wb_ref, wo_ref, o_ref, ua_ref, w_sc, *, n_parts):
    @pl.when(pl.program_id(0) == 0)
    def _():
        _transpose_into(w_sc, (wuv_t_ref, wg_t_ref))

    wuv_ref, wg_ref = w_sc.at[0], w_sc.at[1]
    tm = x_ref.shape[0]
    part = tm // n_parts
    early = []
    for n in range(n_parts):
        rows = slice(n * part, (n + 1) * part)
        x = x_ref[rows, :]
        h = _norm_mod(x, g1_ref[...], mod_ref[:, 0:D_MODEL], mod_ref[:, D_MODEL:2 * D_MODEL]).astype(BF16)
        early.append((x, _dot(h, wuv_ref[...]), _dot(h, wg_ref[...]), _dot(og_ref[rows, :], wb_ref[...])))
    for n, (x, uv, gates_pre, y_b) in enumerate(early):
        v = jax.nn.gelu(uv[:, D_MODEL:])
        vc = v - jnp.mean(v, axis=-1, keepdims=True)
        vn = vc * lax.rsqrt(jnp.mean(vc * vc, axis=-1, keepdims=True) + NORM_EPS) * lng_ref[...] + lnb_ref[...]
        vb = vn.astype(BF16)
        for c in range(part // A_CHUNK):
            rows = slice(c * A_CHUNK, (c + 1) * A_CHUNK)
            out_rows = slice(n * part + c * A_CHUNK, n * part + (c + 1) * A_CHUNK)
            for g in range(A_GROUPS):
                cols = slice(g * LANES, (g + 1) * LANES)
                s = _dot(ws_ref[g], vb[rows, cols]) + bs_ref[:, cols]
                ua_ref[out_rows, cols] = (jax.nn.gelu(uv[rows, cols]) * s).astype(BF16)
        rows = slice(n * part, (n + 1) * part)
        y_a = _dot(ua_ref[rows, :], wa_ref[...])
        gates = jax.nn.sigmoid(gates_pre)
        t = gates[:, :D_MODEL] * y_a + gates[:, D_MODEL:] * y_b
        y = _dot(t.astype(BF16), wo_ref[...])
        o_ref[rows, :] = x + mod_ref[:, 2 * D_MODEL:3 * D_MODEL] * y


def _mix_call(x2d, og, mod3, g1, wt16, row_uv, row_gates, ln_g, ln_b, ws16, bs_cols, wa16, wb16, wo16, tm,
              n_parts, tiles_per_batch):
    m = x2d.shape[0]
    assert tm % (n_parts * A_CHUNK) == 0
    rows = pl.BlockSpec((tm, D_MODEL), lambda i: (i, 0))

    return pl.pallas_call(
        functools.partial(_mix_kernel, n_parts=n_parts),
        grid=(m // tm,),
        in_specs=[rows, rows,
                  pl.BlockSpec((None, 1, 6 * D_MODEL), lambda i: (i // tiles_per_batch, 0, 0)),
                  _resident((1, D_MODEL)),
                  _row_window(2 * D_MODEL, row_uv), _row_window(2 * D_MODEL, row_gates),
                  _resident((1, D_MODEL)), _resident((1, D_MODEL)),
                  _resident((A_GROUPS, A_CHUNK, A_CHUNK)), _resident((A_CHUNK, D_MODEL)),
                  _resident((D_MODEL, D_MODEL)), _resident((D_MODEL, D_MODEL)), _resident((D_MODEL, D_MODEL))],
        out_specs=rows,
        out_shape=jax.ShapeDtypeStruct((m, D_MODEL), F32),
        scratch_shapes=[pltpu.VMEM((tm, D_MODEL), BF16), pltpu.VMEM((2, D_MODEL, 2 * D_MODEL), BF16)],
        compiler_params=pltpu.CompilerParams(
            dimension_semantics=("arbitrary",), vmem_limit_bytes=VMEM_LIMIT),
        name="mix",
    )(x2d, og, mod3, g1, wt16, wt16, ln_g, ln_b, ws16, bs_cols, wa16, wb16, wo16)


def _ffn_kernel(x_ref, mod_ref, g2_ref, wup_ref, cw_ref, cb_ref, wd_ref, gf_ref, o_ref, *, n_parts):
    tm = x_ref.shape[0]
    part = tm // n_parts
    col_in_row = lax.broadcasted_iota(jnp.int32, (part, D_FF), 0) & (GRID_W - 1)
    xs, ups = [], []
    for n in range(n_parts):
        x = x_ref[n * part:(n + 1) * part, :]
        h = _norm_mod(x, g2_ref[...], mod_ref[:, 3 * D_MODEL:4 * D_MODEL],
                      mod_ref[:, 4 * D_MODEL:5 * D_MODEL]).astype(BF16)
        xs.append(x)
        ups.append((_dot(h, wup_ref[:, :D_FF]), _dot(h, wup_ref[:, D_FF:])))
    for n, (x, (a, b)) in enumerate(zip(xs, ups)):
        a_prev = jnp.where(col_in_row == 0, 0.0, pltpu.roll(a, 1, 0))
        a_next = jnp.where(col_in_row == GRID_W - 1, 0.0, pltpu.roll(a, part - 1, 0))
        ac = a_prev * cw_ref[0:1, :] + a * cw_ref[1:2, :] + a_next * cw_ref[2:3, :] + cb_ref[...]
        y = _dot((jax.nn.gelu(ac) * b).astype(BF16), wd_ref[...])
        x2 = x + mod_ref[:, 5 * D_MODEL:6 * D_MODEL] * y
        ms = jnp.mean(x2 * x2, axis=-1, keepdims=True)
        o_ref[n * part:(n + 1) * part, :] = x2 * lax.rsqrt(ms + NORM_EPS) * gf_ref[...]


def _ffn_call(x1, mod3, norm2_g, w_up16, conv_w, conv_b, w_down16, final_g, tm, n_parts, tiles_per_batch):
    m = x1.shape[0]
    assert tm % (n_parts * GRID_W) == 0
    return pl.pallas_call(
        functools.partial(_ffn_kernel, n_parts=n_parts),
        grid=(m // tm,),
        in_specs=[pl.BlockSpec((tm, D_MODEL), lambda i: (i, 0)),
                  pl.BlockSpec((None, 1, 6 * D_MODEL), lambda i: (i // tiles_per_batch, 0, 0)),
                  _resident((1, D_MODEL)),
                  _resident((D_MODEL, 2 * D_FF)),
                  _resident((3, D_FF)), _resident((1, D_FF)),
                  _resident((D_FF, D_MODEL)),
                  _resident((1, D_MODEL))],
        out_specs=pl.BlockSpec((tm, D_MODEL), lambda i: (i, 0)),
        out_shape=jax.ShapeDtypeStruct((m, D_MODEL), F32),
        compiler_params=pltpu.CompilerParams(
            dimension_semantics=("parallel",), vmem_limit_bytes=VMEM_LIMIT),
        name="ffn",
    )(x1, mod3, norm2_g, w_up16, conv_w, conv_b, w_down16, final_g)


def kernel(x, c, ctx, c_ctx, w_mod, b_mod, norm1_g, w_in, conv_qkv, a_log, dt_bias, onorm_g, w_proj_b,
           a_ln_g, a_ln_b, a_ws, a_bs, w_proj_a, w_out, norm2_g, w_up, ffn_conv_w, ffn_conv_b, w_down,
           final_g):
    bsz, length, _ = x.shape
    ctx_len = ctx.shape[1]
    assert w_mod.shape[0] == 1 and bsz <= 7
    assert length % TM_IN == 0 and length % TM == 0 and ctx_len % CHUNK == 0

    wt = jnp.swapaxes(w_in[0], 0, 1).astype(BF16)
    w_ba = (wt, OFF_BA)
    row_uv, row_gates = OFF_Q + 2 * B_WIDTH, OFF_Q + 2 * B_WIDTH + 2 * D_MODEL
    lane_pad = (2 * N_HEADS, LANES - 4 * N_HEADS)
    a_lane = jnp.pad(jnp.exp(a_log[0].astype(F32)).reshape(-1), lane_pad).reshape(1, LANES)
    dt_lane = jnp.pad(dt_bias[0].astype(F32).reshape(-1), lane_pad).reshape(1, LANES)
    bs_cols = jnp.repeat(a_bs[0].T, LANES, axis=1)

    cond8 = jnp.concatenate([c, c_ctx[None, :], jnp.zeros((7 - bsz, D_MODEL), F32)], axis=0)
    mod3 = _mod_call(cond8, w_mod[0], b_mod[0]).reshape(8, 1, 6 * D_MODEL)

    x2d = x.reshape(bsz * length, D_MODEL)
    g1 = norm1_g[0].reshape(1, D_MODEL)
    tiles_per_seq = length // TM_IN
    p2d, gcol, grow = _inproj_call(x2d, mod3, lambda i: i // tiles_per_seq, g1, [(wt, 0), (wt, OFF_Q)], w_ba,
                                   conv_qkv[0], a_lane, dt_lane, ('k', 'v', 'q', 'z'), TM_IN, tiles_per_seq,
                                   "in_proj")
    pc2d, cgcol, cgrow = _inproj_call(ctx.reshape(bsz * ctx_len, D_MODEL), mod3, lambda i: bsz, g1, [(wt, 0)],
                                      w_ba, conv_qkv[0], a_lane, dt_lane, ('k', 'v'), ctx_len, 1, "in_proj_ctx")
    p3 = p2d.reshape(bsz, length, 4 * B_WIDTH)
    pc3 = pc2d.reshape(bsz, ctx_len, 2 * B_WIDTH)
    og = _delta_call(p3, gcol.reshape(bsz, length, LANES), grow, pc3, cgcol.reshape(bsz, ctx_len, LANES), cgrow,
                     onorm_g[0].reshape(1, HEAD_DIM))

    tiles_per_batch = length // TM
    x1 = _mix_call(x2d, og.reshape(bsz * length, B_WIDTH), mod3, g1, wt, row_uv, row_gates,
                   a_ln_g[0].reshape(1, -1), a_ln_b[0].reshape(1, -1), a_ws[0].astype(BF16), bs_cols,
                   w_proj_a[0].astype(BF16), w_proj_b[0].astype(BF16), w_out[0].astype(BF16), TM, MIX_ROW_PARTS,
                   tiles_per_batch)
    out = _ffn_call(x1, mod3, norm2_g[0].reshape(1, -1), w_up[0].astype(BF16), ffn_conv_w[0],
                    ffn_conv_b[0].reshape(1, -1), w_down[0].astype(BF16), final_g.reshape(1, -1),
                    TM, FFN_ROW_PARTS, tiles_per_batch)
    return out.reshape(bsz, length, D_MODEL)
```

```python
import functools

import jax
import jax.numpy as jnp
from jax import lax
from jax.experimental import pallas as pl
from jax.experimental.pallas import tpu as pltpu

F32 = jnp.float32
BF16 = jnp.bfloat16

D_MODEL = 1024
GRID_W = 64
NORM_EPS = 1e-6
N_HEADS = 8
HEAD_DIM = 128
B_WIDTH = N_HEADS * HEAD_DIM
A_GROUPS = 8
A_CHUNK = 128
D_FF = 2816
OFF_BA = 2 * B_WIDTH
OFF_Q = OFF_BA + 4 * N_HEADS

LANES = 128
MXU_COLS = 256
VMEM_BYTES = 64 * 1024 * 1024

CHUNK = 128
GROUP = 8
TM_IN = 1024
IN_PART_ROWS = 256
TM = 512
FFN_ROW_PARTS = 2
MIX_ROW_PARTS = 1
VMEM_LIMIT = 56 * 1024 * 1024
DELTA_VMEM_LIMIT = VMEM_BYTES - 2 * 1024 * 1024


def _silu(x):
    return x * jax.nn.sigmoid(x)


def _dot(a, b):
    return jnp.dot(a, b, preferred_element_type=F32)


def _dot16(a, b):
    return jnp.dot(a, b, preferred_element_type=F32).astype(BF16)


def _bf16_terms(x, n_terms):
    terms = []
    for _ in range(n_terms):
        terms.append(x.astype(BF16))
        x = x - terms[-1].astype(F32)
    return terms


def _mod_kernel(c_ref, w_ref, b_ref, o_ref):
    rows = c_ref.shape[0]
    terms = [t.astype(F32) for t in _bf16_terms(_silu(c_ref[...]), 3)]
    cond = jnp.concatenate(terms + [jnp.zeros((rows, D_MODEL), F32)], axis=0).astype(BF16)
    acc = sum(_dot(cond, w_term) for w_term in _bf16_terms(w_ref[...], 2))
    o_ref[...] = acc[0:rows] + acc[rows:2 * rows] + acc[2 * rows:3 * rows] + b_ref[...]


def _mod_call(cond8, w_mod, b_mod):
    n = w_mod.shape[1]
    tn = 1536
    return pl.pallas_call(
        _mod_kernel,
        grid=(n // tn,),
        in_specs=[pl.BlockSpec((8, D_MODEL), lambda j: (0, 0)),
                  pl.BlockSpec((D_MODEL, tn), lambda j: (0, j)),
                  pl.BlockSpec((1, tn), lambda j: (0, j))],
        out_specs=pl.BlockSpec((8, tn), lambda j: (0, j)),
        out_shape=jax.ShapeDtypeStruct((8, n), F32),
        name="mod",
    )(cond8, w_mod, b_mod.reshape(1, n))


def _norm_mod(x, g, shift, scale):
    ms = jnp.mean(x * x, axis=-1, keepdims=True)
    y = x * lax.rsqrt(ms + NORM_EPS) * g
    return y * (1.0 + scale) + shift


def _l2norm(t):
    return t * lax.rsqrt(jnp.sum(t * t, axis=-1, keepdims=True) + NORM_EPS)


def _gate_columns(ba, a_lane, dt_lane):
    n_sub = ba.shape[0] // CHUNK
    lane = lax.broadcasted_iota(jnp.int32, ba.shape, 1)
    ri = lax.broadcasted_iota(jnp.int32, (CHUNK, CHUNK), 0)
    ci = lax.broadcasted_iota(jnp.int32, (CHUNK, CHUNK), 1)
    tri_lo, tri_up = (ri >= ci).astype(F32), (ri <= ci).astype(F32)
    beta = jax.nn.sigmoid(ba)
    y = ba + dt_lane
    softplus = jnp.maximum(y, 0.0) + jnp.log1p(jnp.exp(-jnp.abs(y)))
    g = jnp.where((lane >= 2 * N_HEADS) & (lane < 4 * N_HEADS), -a_lane * softplus, 0.0)
    part = 4 * N_HEADS
    hi = g.astype(BF16).astype(F32)
    mid = (g - hi).astype(BF16).astype(F32)
    lo = (g - hi - mid).astype(BF16).astype(F32)
    packed = (hi + pltpu.roll(mid, part, 1) + pltpu.roll(lo, 2 * part, 1)).astype(BF16)

    def unpack(r):
        return r + pltpu.roll(r, LANES - part, 1) + pltpu.roll(r, LANES - 2 * part, 1)

    tri_lo, tri_up = tri_lo.astype(BF16), tri_up.astype(BF16)
    parts = [packed[s * CHUNK:(s + 1) * CHUNK, :] for s in range(n_sub)]
    pre = [unpack(_dot(tri_lo, x)) for x in parts]
    suf = [unpack(_dot(tri_up, x)) for x in parts]
    lane = lax.broadcasted_iota(jnp.int32, (CHUNK, LANES), 1)
    return [jnp.where(lane < 2 * N_HEADS, beta[s * CHUNK:(s + 1) * CHUNK, :],
                      jnp.where(lane < 3 * N_HEADS, pre[s], suf[s])) for s in range(n_sub)]


def _resident(shape):
    return pl.BlockSpec(shape, lambda i: (0,) * len(shape), pipeline_mode=pl.Buffered(1))


def _row_window(n_rows, first_row):
    return pl.BlockSpec((pl.Element(n_rows), pl.Element(D_MODEL)), lambda i: (first_row, 0),
                        pipeline_mode=pl.Buffered(1))


def _slab_order(kinds, slabs_per_group):
    heavy = [(g, s) for g, kind in enumerate(kinds) if kind != 'z' for s in range(slabs_per_group)]
    light = [(g, s) for g, kind in enumerate(kinds) if kind == 'z' for s in range(slabs_per_group)]
    every = max(len(heavy) // max(len(light), 1), 1)
    order = []
    for n, item in enumerate(heavy):
        order.append(item)
        if light and (n + 1) % every == 0:
            order.append(light.pop(0))
    return order + light


def _transpose_into(w_sc, wt_refs):
    for p, wt_ref in enumerate(wt_refs):
        for c0 in range(0, wt_ref.shape[0], MXU_COLS):
            w_sc[p, :, c0:c0 + MXU_COLS] = wt_ref[c0:c0 + MXU_COLS, :].T


def _inproj_kernel(x_ref, xb_ref, xa_ref, mod_ref, g_ref, wba_ref, cw_ref, a_ref, dt_ref, *rest,
                   tiles_per_seq, kinds):
    n_w = (len(kinds) + 1) // 2
    wt_refs = rest[:n_w]
    o_ref, gcol_ref, grow_ref, h_ref, halo_ref, w_sc = rest[n_w:]
    i = pl.program_id(0)

    @pl.when(i == 0)
    def _():
        _transpose_into(w_sc, wt_refs)

    w_refs = [w_sc.at[p] for p in range(n_w)]
    tm = o_ref.shape[1]
    shift, scale = mod_ref[:, 0:D_MODEL], mod_ref[:, D_MODEL:2 * D_MODEL]
    h_ref[...] = _norm_mod(x_ref[...], g_ref[...], shift, scale).astype(BF16)
    halo = jnp.concatenate([xb_ref[...], xa_ref[...]], axis=0)
    halo_ref[...] = _norm_mod(halo, g_ref[...], shift, scale).astype(BF16)
    ba = lax.dot_general(h_ref[...], wba_ref[...], (((1,), (1,)), ((), ())), preferred_element_type=F32)
    for s, col in enumerate(_gate_columns(ba, a_ref[...], dt_ref[...])):
        rows = slice(s * CHUNK, (s + 1) * CHUNK)
        gcol_ref[rows, :] = col
        grow_ref[:, rows] = col.T

    tile_in_seq = i % tiles_per_seq
    slab = MXU_COLS
    row8 = lax.broadcasted_iota(jnp.int32, (8, slab), 0)
    n_parts = max(tm // IN_PART_ROWS, 1)
    part = tm // n_parts
    tiled = (part // 8, 8, slab)
    for g, s in _slab_order(kinds, B_WIDTH // slab):
        kind = kinds[g]
        col0 = g * B_WIDTH + s * slab
        cols = slice(col0, col0 + slab)
        w_cols = slice(col0 % (2 * B_WIDTH), col0 % (2 * B_WIDTH) + slab)
        w_ref = w_refs[g // 2]
        ps = [_dot(h_ref[n * part:(n + 1) * part, :], w_ref[:, w_cols]) for n in range(n_parts)]
        if kind != 'z':
            edge = _dot(halo_ref[...], w_ref[:, w_cols])
            taps = [jnp.broadcast_to(cw_ref[r:r + 1, cols], (8, slab)) for r in range(3)]
        for n, p in enumerate(ps):
            if kind != 'z':
                prev_row = ps[n - 1][part - 1:part] if n > 0 else jnp.where(tile_in_seq > 0, edge[7:8, :], 0.0)
                next_row = (ps[n + 1][0:1] if n + 1 < n_parts
                            else jnp.where(tile_in_seq < tiles_per_seq - 1, edge[8:9, :], 0.0))
                xp = pltpu.roll(p, 1, 0)
                xp = jnp.concatenate([jnp.where(row8 == 0, prev_row, xp[0:8]), xp[8:]], axis=0)
                xn = pltpu.roll(p, part - 1, 0)
                xn = jnp.concatenate([xn[:part - 8], jnp.where(row8 == 7, next_row, xn[part - 8:])], axis=0)
                p = (xp.reshape(tiled) * taps[0] + p.reshape(tiled) * taps[1]
                     + xn.reshape(tiled) * taps[2]).reshape(part, slab)
            y = _silu(p)
            rows = slice(n * part, (n + 1) * part)
            for hd in range(slab // HEAD_DIM):
                head = y[:, hd * HEAD_DIM:(hd + 1) * HEAD_DIM]
                if kind in ('k', 'q'):
                    head = _l2norm(head)
                if kind == 'q':
                    head = head * (HEAD_DIM ** -0.5)
                o_ref[col0 // HEAD_DIM + hd, rows, :] = head.astype(o_ref.dtype)


def _inproj_call(x2d, mod3, mod_row_of_tile, norm_g, w_pairs, w_ba, conv_w, a_lane, dt_lane, kinds, tm,
                 tiles_per_seq, name):
    m = x2d.shape[0]
    n = B_WIDTH * len(kinds)
    assert len(kinds) == 2 * len(w_pairs) and tm % CHUNK == 0
    assert all(kind != 'z' for kind in kinds[:conv_w.shape[1] // B_WIDTH]) and 'z' not in kinds[:-1]
    last_halo = m // 8 - 1

    def pair_spec(first_row):
        return _row_window(2 * B_WIDTH, first_row)

    return pl.pallas_call(
        functools.partial(_inproj_kernel, tiles_per_seq=tiles_per_seq, kinds=kinds),
        grid=(m // tm,),
        in_specs=[pl.BlockSpec((tm, D_MODEL), lambda i: (i, 0)),
                  pl.BlockSpec((8, D_MODEL), lambda i: (jnp.maximum(i * (tm // 8) - 1, 0), 0)),
                  pl.BlockSpec((8, D_MODEL), lambda i: (jnp.minimum((i + 1) * (tm // 8), last_halo), 0)),
                  pl.BlockSpec((None, 1, 6 * D_MODEL), lambda i: (mod_row_of_tile(i), 0, 0)),
                  _resident((1, D_MODEL)),
                  _row_window(LANES, w_ba[1]),
                  _resident(conv_w.shape),
                  _resident((1, LANES)), _resident((1, LANES))] + [pair_spec(blk) for _, blk in w_pairs],
        out_specs=[pl.BlockSpec((None, n // HEAD_DIM, tm, HEAD_DIM),
                                lambda i: (i // tiles_per_seq, 0, i % tiles_per_seq, 0)),
                   pl.BlockSpec((tm, LANES), lambda i: (i, 0)),
                   pl.BlockSpec((LANES, tm), lambda i: (0, i))],
        out_shape=[jax.ShapeDtypeStruct((m // (tm * tiles_per_seq), n // HEAD_DIM, tm * tiles_per_seq, HEAD_DIM),
                                        BF16),
                   jax.ShapeDtypeStruct((m, LANES), F32),
                   jax.ShapeDtypeStruct((LANES, m), F32)],
        scratch_shapes=[pltpu.VMEM((tm, D_MODEL), BF16), pltpu.VMEM((16, D_MODEL), BF16),
                        pltpu.VMEM((len(w_pairs), D_MODEL, 2 * B_WIDTH), BF16)],
        compiler_params=pltpu.CompilerParams(
            dimension_semantics=("arbitrary",), vmem_limit_bytes=VMEM_LIMIT),
        name=name,
    )(x2d, x2d, x2d, mod3, norm_g, w_ba[0], conv_w, a_lane, dt_lane, *[arr for arr, _ in w_pairs])


def _run_interleaved(*gens):
    results = [None] * len(gens)
    live = list(range(len(gens)))
    while live:
        for idx in list(live):
            try:
                next(gens[idx])
            except StopIteration as stop:
                results[idx] = stop.value
                live.remove(idx)
    return results


TRI_INCL, TRI_STRICT, TRI_BLK16, TRI_EYE = 0, 2, 4, 5
MERGE_SIZES = (32, 64, 128)


def _mask_constants():
    ri = lax.broadcasted_iota(jnp.int32, (CHUNK, CHUNK), 0)
    ci = lax.broadcasted_iota(jnp.int32, (CHUNK, CHUNK), 1)

    def same_block(size):
        return (ri // size) == (ci // size)

    tri = jnp.stack([ri >= ci, ri <= ci, ri > ci, ri < ci, same_block(16), ri == ci]).astype(F32)
    off = jnp.stack([same_block(s) & ~same_block(s // 2) for s in MERGE_SIZES]).astype(BF16)
    return tri, off


def _tri_inverse_stages(n_mats, tri_ref, off_ref):
    n16 = [n * tri_ref[TRI_BLK16] for n in n_mats]
    ts = [tri_ref[TRI_EYE] + n for n in n16]
    pbs = [n.astype(BF16) for n in n16]
    pbs = [_dot16(p, p) for p in pbs]
    nbs = [n.astype(BF16) for n in n_mats]
    yield
    for _ in range(2):
        ts = [t + _dot(t.astype(BF16), p) for t, p in zip(ts, pbs)]
        pbs = [_dot16(p, p) for p in pbs]
        yield
    ts = [t + _dot(t.astype(BF16), p) for t, p in zip(ts, pbs)]
    yield
    tbs = [t.astype(BF16) for t in ts]
    for level in range(len(MERGE_SIZES)):
        xs = [_dot16(nb * off_ref[level], tb) for nb, tb in zip(nbs, tbs)]
        yield
        tbs = [tb + _dot16(tb, x) for tb, x in zip(tbs, xs)]
        yield
    return tbs


def _chunk_start(c):
    return c * CHUNK if isinstance(c, int) else pl.multiple_of(c * CHUNK, CHUNK)


def _prep_stages(chunks, h, k_ref, v_ref, q_ref, gcol_ref, grow_ref, tri_ref, off_ref, early, late):
    lane = lax.broadcasted_iota(jnp.int32, (CHUNK, LANES), 1)
    head_row = lax.broadcasted_iota(jnp.int32, (N_HEADS, CHUNK), 0)

    acts, prods = [], []
    for c in chunks:
        rows = pl.ds(_chunk_start(c), CHUNK)
        kb16 = k_ref[rows, :]
        k = kb16.astype(F32)
        kt16 = k.T.astype(BF16)
        v = v_ref[rows, :].astype(F32)
        if q_ref is not None:
            qb16 = q_ref[rows, :]
            aa = _dot(jnp.concatenate([kb16, qb16], axis=0), kt16)
            acts.append((k, v, qb16.astype(F32)))
            prods.append((aa[:CHUNK], aa[CHUNK:]))
        else:
            acts.append((k, v, None))
            prods.append((_dot(kb16, kt16), None))
    yield

    neg_ms, rhss = [], []
    for i, (c, (k, v, q), (a_kk, a_qk)) in enumerate(zip(chunks, acts, prods)):
        r0 = _chunk_start(c)
        gates = gcol_ref[pl.ds(r0, CHUNK), :]
        for d in range(2):
            beta = jnp.sum(jnp.where(lane == d * N_HEADS + h, gates, 0.0), axis=1, keepdims=True)
            gc = jnp.sum(jnp.where(lane == (2 + d) * N_HEADS + h, gates, 0.0), axis=1, keepdims=True)
            all_heads = grow_ref[(2 + d) * N_HEADS:(3 + d) * N_HEADS, pl.ds(r0, CHUNK)]
            gc_row = jnp.sum(jnp.where(head_row == h, all_heads, 0.0), axis=0, keepdims=True)
            decay_ij = jnp.exp(jnp.minimum(gc - gc_row, 0.0))
            neg_ms.append(a_kk * (-beta) * (decay_ij * tri_ref[TRI_STRICT + d]))
            eg = jnp.exp(gc)
            rhss.append(jnp.concatenate([k * (beta * eg), v * beta], axis=1).astype(BF16))
            g_end = gc[CHUNK - 1:CHUNK, :] if d == 0 else gc[0:1, :]
            kd = k * jnp.exp(g_end - gc)
            decay = jnp.broadcast_to(jnp.exp(g_end), (8, LANES))
            if q is not None:
                early(i, d, kd, decay, q * eg, a_qk * (decay_ij * tri_ref[TRI_INCL + d]))
            else:
                early(i, d, kd, decay, None, None)

    t_invs = yield from _tri_inverse_stages(neg_ms, tri_ref, off_ref)
    wus = [_dot(t, rhs) for t, rhs in zip(t_invs, rhss)]
    yield
    for n, wu in enumerate(wus):
        late(n // 2, n % 2, wu[:, :HEAD_DIM], wu[:, HEAD_DIM:])


def _delta_kernel(pk_ref, pv_ref, pq_ref, gcol_ref, grow_ref, ck_ref, cv_ref, cgcol_ref, cgrow_ref,
                  sz_ref, onorm_ref, tri_ref, off_ref, og_ref,
                  wq_sc, kq_sc, u_sc, ge_sc, s0_sc, o_sc, *, n_chunks, n_ctx_chunks, n_heads_total):
    t = pl.program_id(0)
    h = jnp.minimum(t, n_heads_total - 1) % N_HEADS
    slot_p = t % 2
    slot_s = 1 - slot_p

    @pl.when(t == 0)
    def _():
        wq_sc[1] = jnp.zeros(wq_sc.shape[1:], wq_sc.dtype)
        kq_sc[1] = jnp.zeros(kq_sc.shape[1:], kq_sc.dtype)
        u_sc[1] = jnp.zeros(u_sc.shape[1:], u_sc.dtype)
        ge_sc[1] = jnp.zeros(ge_sc.shape[1:], ge_sc.dtype)
        s0_sc[1] = jnp.zeros(s0_sc.shape[1:], s0_sc.dtype)

    def context_states():
        kept = {}

        def early(c, d, kd, decay, qg, qk):
            kept[c, d] = [kd.T.astype(BF16), decay]

        def late(c, d, w, u):
            kept[c, d] += [w.astype(BF16), u]

        yield from _prep_stages(list(range(n_ctx_chunks)), h, ck_ref, cv_ref, None, cgcol_ref, cgrow_ref,
                                tri_ref, off_ref, early, late)
        states = [jnp.zeros((HEAD_DIM, HEAD_DIM), F32)] * 2
        for c in range(n_ctx_chunks):
            steps = [kept[c, 0], kept[n_ctx_chunks - 1 - c, 1]]
            ws = [_dot(w, s.astype(BF16)) for (_, _, w, _), s in zip(steps, states)]
            yield
            kv = [_dot(kd_t, (u - x).astype(BF16)) for (kd_t, _, _, u), x in zip(steps, ws)]
            yield
            states = [s * decay[0:1, :] + x for s, (_, decay, _, _), x in zip(states, steps, kv)]
        for d in range(2):
            s0_sc[slot_p, d] = states[d]

    def prepare(i):
        chunks = [i * GROUP + g for g in range(GROUP)]

        def early(n, d, kd, decay, qg, qk):
            c = chunks[n]
            kq_sc[slot_p, d, c, 0:CHUNK, :] = kd.T.astype(BF16)
            kq_sc[slot_p, d, c, CHUNK:2 * CHUNK, :] = qk.astype(BF16)
            wq_sc[slot_p, d, c, CHUNK:2 * CHUNK, :] = qg.astype(BF16)
            ge_sc[slot_p, d, c] = decay

        def late(n, d, w, u):
            c = chunks[n]
            wq_sc[slot_p, d, c, 0:CHUNK, :] = w.astype(BF16)
            u_sc[slot_p, d, pl.ds(_chunk_start(c), CHUNK), :] = u

        yield from _prep_stages(chunks, h, pk_ref, pv_ref, pq_ref, gcol_ref, grow_ref, tri_ref, off_ref,
                                early, late)

    def finish(c, o):
        r0 = _chunk_start(c)
        y = o * lax.rsqrt(jnp.mean(o * o, axis=-1, keepdims=True) + NORM_EPS) * onorm_ref[...]
        og_ref[pl.ds(r0, CHUNK), :] = (y * sz_ref[pl.ds(r0, CHUNK), :].astype(F32)).astype(og_ref.dtype)

    def scan(i, states, second_half):
        for g in range(GROUP):
            j = i * GROUP + g
            chunks = (j, n_chunks - 1 - j)
            ws = [_dot(wq_sc[slot_s, d, c], s.astype(BF16)) for d, (c, s) in enumerate(zip(chunks, states))]
            yield
            v_new = [u_sc[slot_s, d, pl.ds(_chunk_start(c), CHUNK), :] - x[:CHUNK]
                     for d, (c, x) in enumerate(zip(chunks, ws))]
            kv = [_dot(kq_sc[slot_s, d, c], vn.astype(BF16)) for d, (c, vn) in enumerate(zip(chunks, v_new))]
            yield
            states = [s * ge_sc[slot_s, d, c][0:1, :] + x[:CHUNK]
                      for d, (c, s, x) in enumerate(zip(chunks, states, kv))]
            for c, x, y in zip(chunks, ws, kv):
                o = x[CHUNK:] + y[CHUNK:]
                rows = pl.ds(_chunk_start(c), CHUNK)
                if second_half:
                    finish(c, o_sc[rows, :] + o)
                else:
                    o_sc[rows, :] = o
        return tuple(states)

    def body(second_half, i, states, *also):
        return _run_interleaved(scan(i, states, second_half), prepare(i), *also)[0]

    n_iter = n_chunks // GROUP
    states = (s0_sc[slot_s, 0], s0_sc[slot_s, 1])
    states = body(False, 0, states, context_states())
    for i in range(1, n_iter):
        states = body(i >= n_iter // 2, i, states)


def _delta_call(p3, gcol, grow, pc3, cgcol, cgrow, onorm_g):
    bsz, _, length, _ = p3.shape
    ctx_len = pc3.shape[2]
    n_chunks = length // CHUNK
    n_ctx_chunks = ctx_len // CHUNK
    n_heads_total = bsz * N_HEADS
    assert n_chunks % (2 * GROUP) == 0
    tri_masks, off_masks = _mask_constants()

    def prepared(t):
        t = jnp.minimum(t, n_heads_total - 1)
        return t // N_HEADS, t % N_HEADS

    def scanned(t):
        t = jnp.maximum(t - 1, 0)
        return t // N_HEADS, t % N_HEADS

    def head_cols(rows, first_block, which):
        def index(t):
            b, h = which(t)
            return b, first_block + h, 0, 0
        return pl.BlockSpec((None, None, rows, LANES), index)

    def out_cols(rows):
        def index(t):
            b, h = scanned(t)
            return b, 0, h
        return pl.BlockSpec((None, rows, LANES), index)

    def gates_col(rows):
        return pl.BlockSpec((None, rows, LANES), lambda t: (prepared(t)[0], 0, 0), pipeline_mode=pl.Buffered(1))

    def gates_row(rows):
        return pl.BlockSpec((LANES, rows), lambda t: (0, prepared(t)[0]), pipeline_mode=pl.Buffered(1))

    return pl.pallas_call(
        functools.partial(_delta_kernel, n_chunks=n_chunks, n_ctx_chunks=n_ctx_chunks,
                          n_heads_total=n_heads_total),
        grid=(n_heads_total + 1,),
        in_specs=[head_cols(length, 0, prepared), head_cols(length, N_HEADS, prepared),
                  head_cols(length, 2 * N_HEADS, prepared),
                  gates_col(length), gates_row(length),
                  head_cols(ctx_len, 0, prepared), head_cols(ctx_len, N_HEADS, prepared),
                  gates_col(ctx_len), gates_row(ctx_len),
                  head_cols(length, 3 * N_HEADS, scanned),
                  pl.BlockSpec((1, LANES), lambda t: (0, 0)),
                  pl.BlockSpec(tri_masks.shape, lambda t: (0, 0, 0), pipeline_mode=pl.Buffered(1)),
                  pl.BlockSpec(off_masks.shape, lambda t: (0, 0, 0), pipeline_mode=pl.Buffered(1))],
        out_specs=out_cols(length),
        out_shape=jax.ShapeDtypeStruct((bsz, length, B_WIDTH), BF16),
        scratch_shapes=[pltpu.VMEM((2, 2, n_chunks, 2 * CHUNK, LANES), BF16),
                        pltpu.VMEM((2, 2, n_chunks, 2 * CHUNK, LANES), BF16),
                        pltpu.VMEM((2, 2, length, LANES), F32),
                        pltpu.VMEM((2, 2, n_chunks, 8, LANES), F32),
                        pltpu.VMEM((2, 2, HEAD_DIM, HEAD_DIM), F32),
                        pltpu.VMEM((length, LANES), F32)],
        compiler_params=pltpu.CompilerParams(
            dimension_semantics=("arbitrary",), vmem_limit_bytes=DELTA_VMEM_LIMIT),
        name="delta",
    )(p3, p3, p3, gcol, grow, pc3, pc3, cgcol, cgrow, p3, onorm_g, tri_masks, off_masks)


def _mix_kernel(x_ref, og_ref, mod_ref, g1_ref, wuv_t_ref, wg_t_ref, lng_ref, lnb_ref, ws_ref, bs_ref,
                wa_ref, wb_ref, wo_ref, o_ref, ua_ref, w_sc, *, n_parts):
    @pl.when(pl.program_id(0) == 0)
    def _():
        _transpose_into(w_sc, (wuv_t_ref, wg_t_ref))

    wuv_ref, wg_ref = w_sc.at[0], w_sc.at[1]
    tm = x_ref.shape[0]
    part = tm // n_parts
    early = []
    for n in range(n_parts):
        rows = slice(n * part, (n + 1) * part)
        x = x_ref[rows, :]
        h = _norm_mod(x, g1_ref[...], mod_ref[:, 0:D_MODEL], mod_ref[:, D_MODEL:2 * D_MODEL]).astype(BF16)
        early.append((x, _dot(h, wuv_ref[...]), _dot(h, wg_ref[...]), _dot(og_ref[rows, :], wb_ref[...])))
    for n, (x, uv, gates_pre, y_b) in enumerate(early):
        v = jax.nn.gelu(uv[:, D_MODEL:])
        vc = v - jnp.mean(v, axis=-1, keepdims=True)
        vn = vc * lax.rsqrt(jnp.mean(vc * vc, axis=-1, keepdims=True) + NORM_EPS) * lng_ref[...] + lnb_ref[...]
        vb = vn.astype(BF16)
        for c in range(part // A_CHUNK):
            rows = slice(c * A_CHUNK, (c + 1) * A_CHUNK)
            out_rows = slice(n * part + c * A_CHUNK, n * part + (c + 1) * A_CHUNK)
            for g in range(A_GROUPS):
                cols = slice(g * LANES, (g + 1) * LANES)
                s = _dot(ws_ref[g], vb[rows, cols]) + bs_ref[:, cols]
                ua_ref[out_rows, cols] = (jax.nn.gelu(uv[rows, cols]) * s).astype(BF16)
        rows = slice(n * part, (n + 1) * part)
        y_a = _dot(ua_ref[rows, :], wa_ref[...])
        gates = jax.nn.sigmoid(gates_pre)
        t = gates[:, :D_MODEL] * y_a + gates[:, D_MODEL:] * y_b
        y = _dot(t.astype(BF16), wo_ref[...])
        o_ref[rows, :] = x + mod_ref[:, 2 * D_MODEL:3 * D_MODEL] * y


def _mix_call(x2d, og, mod3, g1, wt16, row_uv, row_gates, ln_g, ln_b, ws16, bs_cols, wa16, wb16, wo16, tm,
              n_parts, tiles_per_batch):
    m = x2d.shape[0]
    assert tm % (n_parts * A_CHUNK) == 0
    rows = pl.BlockSpec((tm, D_MODEL), lambda i: (i, 0))

    return pl.pallas_call(
        functools.partial(_mix_kernel, n_parts=n_parts),
        grid=(m // tm,),
        in_specs=[rows, rows,
                  pl.BlockSpec((None, 1, 6 * D_MODEL), lambda i: (i // tiles_per_batch, 0, 0)),
                  _resident((1, D_MODEL)),
                  _row_window(2 * D_MODEL, row_uv), _row_window(2 * D_MODEL, row_gates),
                  _resident((1, D_MODEL)), _resident((1, D_MODEL)),
                  _resident((A_GROUPS, A_CHUNK, A_CHUNK)), _resident((A_CHUNK, D_MODEL)),
                  _resident((D_MODEL, D_MODEL)), _resident((D_MODEL, D_MODEL)), _resident((D_MODEL, D_MODEL))],
        out_specs=rows,
        out_shape=jax.ShapeDtypeStruct((m, D_MODEL), F32),
        scratch_shapes=[pltpu.VMEM((tm, D_MODEL), BF16), pltpu.VMEM((2, D_MODEL, 2 * D_MODEL), BF16)],
        compiler_params=pltpu.CompilerParams(
            dimension_semantics=("arbitrary",), vmem_limit_bytes=VMEM_LIMIT),
        name="mix",
    )(x2d, og, mod3, g1, wt16, wt16, ln_g, ln_b, ws16, bs_cols, wa16, wb16, wo16)


def _ffn_kernel(x_ref, mod_ref, g2_ref, wup_ref, cw_ref, cb_ref, wd_ref, gf_ref, o_ref, *, n_parts):
    tm = x_ref.shape[0]
    part = tm // n_parts
    col_in_row = lax.broadcasted_iota(jnp.int32, (part, D_FF), 0) & (GRID_W - 1)
    xs, ups = [], []
    for n in range(n_parts):
        x = x_ref[n * part:(n + 1) * part, :]
        h = _norm_mod(x, g2_ref[...], mod_ref[:, 3 * D_MODEL:4 * D_MODEL],
                      mod_ref[:, 4 * D_MODEL:5 * D_MODEL]).astype(BF16)
        xs.append(x)
        ups.append((_dot(h, wup_ref[:, :D_FF]), _dot(h, wup_ref[:, D_FF:])))
    for n, (x, (a, b)) in enumerate(zip(xs, ups)):
        a_prev = jnp.where(col_in_row == 0, 0.0, pltpu.roll(a, 1, 0))
        a_next = jnp.where(col_in_row == GRID_W - 1, 0.0, pltpu.roll(a, part - 1, 0))
        ac = a_prev * cw_ref[0:1, :] + a * cw_ref[1:2, :] + a_next * cw_ref[2:3, :] + cb_ref[...]
        y = _dot((jax.nn.gelu(ac) * b).astype(BF16), wd_ref[...])
        x2 = x + mod_ref[:, 5 * D_MODEL:6 * D_MODEL] * y
        ms = jnp.mean(x2 * x2, axis=-1, keepdims=True)
        o_ref[n * part:(n + 1) * part, :] = x2 * lax.rsqrt(ms + NORM_EPS) * gf_ref[...]


def _ffn_call(x1, mod3, norm2_g, w_up16, conv_w, conv_b, w_down16, final_g, tm, n_parts, tiles_per_batch):
    m = x1.shape[0]
    assert tm % (n_parts * GRID_W) == 0
    return pl.pallas_call(
        functools.partial(_ffn_kernel, n_parts=n_parts),
        grid=(m // tm,),
        in_specs=[pl.BlockSpec((tm, D_MODEL), lambda i: (i, 0)),
                  pl.BlockSpec((None, 1, 6 * D_MODEL), lambda i: (i // tiles_per_batch, 0, 0)),
                  _resident((1, D_MODEL)),
                  _resident((D_MODEL, 2 * D_FF)),
                  _resident((3, D_FF)), _resident((1, D_FF)),
                  _resident((D_FF, D_MODEL)),
                  _resident((1, D_MODEL))],
        out_specs=pl.BlockSpec((tm, D_MODEL), lambda i: (i, 0)),
        out_shape=jax.ShapeDtypeStruct((m, D_MODEL), F32),
        compiler_params=pltpu.CompilerParams(
            dimension_semantics=("parallel",), vmem_limit_bytes=VMEM_LIMIT),
        name="ffn",
    )(x1, mod3, norm2_g, w_up16, conv_w, conv_b, w_down16, final_g)


def kernel(x, c, ctx, c_ctx, w_mod, b_mod, norm1_g, w_in, conv_qkv, a_log, dt_bias, onorm_g, w_proj_b,
           a_ln_g, a_ln_b, a_ws, a_bs, w_proj_a, w_out, norm2_g, w_up, ffn_conv_w, ffn_conv_b, w_down,
           final_g):
    bsz, length, _ = x.shape
    ctx_len = ctx.shape[1]
    assert w_mod.shape[0] == 1 and bsz <= 7
    assert length % TM_IN == 0 and length % TM == 0 and ctx_len % CHUNK == 0

    wt = jnp.swapaxes(w_in[0], 0, 1).astype(BF16)
    w_ba = (wt, OFF_BA)
    row_uv, row_gates = OFF_Q + 2 * B_WIDTH, OFF_Q + 2 * B_WIDTH + 2 * D_MODEL
    lane_pad = (2 * N_HEADS, LANES - 4 * N_HEADS)
    a_lane = jnp.pad(jnp.exp(a_log[0].astype(F32)).reshape(-1), lane_pad).reshape(1, LANES)
    dt_lane = jnp.pad(dt_bias[0].astype(F32).reshape(-1), lane_pad).reshape(1, LANES)
    bs_cols = jnp.repeat(a_bs[0].T, LANES, axis=1)

    cond8 = jnp.concatenate([c, c_ctx[None, :], jnp.zeros((7 - bsz, D_MODEL), F32)], axis=0)
    mod3 = _mod_call(cond8, w_mod[0], b_mod[0]).reshape(8, 1, 6 * D_MODEL)

    x2d = x.reshape(bsz * length, D_MODEL)
    g1 = norm1_g[0].reshape(1, D_MODEL)
    tiles_per_seq = length // TM_IN
    p2d, gcol, grow = _inproj_call(x2d, mod3, lambda i: i // tiles_per_seq, g1, [(wt, 0), (wt, OFF_Q)], w_ba,
                                   conv_qkv[0], a_lane, dt_lane, ('k', 'v', 'q', 'z'), TM_IN, tiles_per_seq,
                                   "in_proj")
    pc2d, cgcol, cgrow = _inproj_call(ctx.reshape(bsz * ctx_len, D_MODEL), mod3, lambda i: bsz, g1, [(wt, 0)],
                                      w_ba, conv_qkv[0], a_lane, dt_lane, ('k', 'v'), ctx_len, 1, "in_proj_ctx")
    og = _delta_call(p2d, gcol.reshape(bsz, length, LANES), grow, pc2d, cgcol.reshape(bsz, ctx_len, LANES), cgrow,
                     onorm_g[0].reshape(1, HEAD_DIM))

    tiles_per_batch = length // TM
    x1 = _mix_call(x2d, og.reshape(bsz * length, B_WIDTH), mod3, g1, wt, row_uv, row_gates,
                   a_ln_g[0].reshape(1, -1), a_ln_b[0].reshape(1, -1), a_ws[0].astype(BF16), bs_cols,
                   w_proj_a[0].astype(BF16), w_proj_b[0].astype(BF16), w_out[0].astype(BF16), TM, MIX_ROW_PARTS,
                   tiles_per_batch)
    out = _ffn_call(x1, mod3, norm2_g[0].reshape(1, -1), w_up[0].astype(BF16), ffn_conv_w[0],
                    ffn_conv_b[0].reshape(1, -1), w_down[0].astype(BF16), final_g.reshape(1, -1),
                    TM, FFN_ROW_PARTS, tiles_per_batch)
    return out.reshape(bsz, length, D_MODEL)
```
